```python
import jax
import jax.numpy as jnp
from jax import lax
import numpy as np


D_MODEL = 1024
BATCH = 4
SEQ = 8192
DEPTH = 2
DEC_BATCH = 8
DEC_SEQ = 2048
PAST_LEN = 128

GRID_W = 64
NA_HEADS = 8
NA_HEAD_DIM = 64
NA_WIDTH = NA_HEADS * NA_HEAD_DIM
NA_WIN_ROWS = 8
NA_WIN_COLS = 16
RW_HEADS = 8
RW_HEAD_DIM = 64
RW_WIDTH = RW_HEADS * RW_HEAD_DIM
DECAY_LORA = 64
AAA_LORA = 64
GATE_LORA = 160
MEM_TOKENS = 256
MEM_HEADS = 4
MEM_HEAD_DIM = 128
MEM_WIDTH = MEM_HEADS * MEM_HEAD_DIM
BRANCH_WIDTH = 512
N_BRANCH = 3
D_FF = 2816
RMS_EPS = 1e-6
GN_EPS = 64e-5
RW_COLS = 3 * RW_WIDTH + 2 * DECAY_LORA + 2 * AAA_LORA + GATE_LORA
IN_COLS = 3 * NA_WIDTH + RW_COLS + MEM_WIDTH + N_BRANCH * D_MODEL

kernel_name = 'hybrid_natten_rwkv7_memory_encoder'


def rmsnorm(x, gain):
    xf = x.astype(jnp.float32)
    y = xf * lax.rsqrt(jnp.mean(xf * xf, axis=-1, keepdims=True) + RMS_EPS)
    return (y * gain.astype(jnp.float32)).astype(x.dtype)


def dwconv3(u, w):
    up = jnp.pad(u, ((0, 0), (1, 1), (0, 0)))
    return up[:, :-2] * w[0] + up[:, 1:-1] * w[1] + up[:, 2:] * w[2]


def neighbourhood_attention(q, k, v, rpb):
    bsz, l = q.shape[0], q.shape[1]
    rows = l // GRID_W
    wr = min(NA_WIN_ROWS, rows)
    wc = NA_WIN_COLS
    qg = q.reshape(bsz, rows, GRID_W, NA_HEADS, NA_HEAD_DIM)
    kg = k.reshape(bsz, rows, GRID_W, NA_HEADS, NA_HEAD_DIM)
    vg = v.reshape(bsz, rows, GRID_W, NA_HEADS, NA_HEAD_DIM)
    cols = np.arange(GRID_W)
    c0 = np.clip(cols - wc // 2, 0, GRID_W - wc)
    col_idx = c0[:, None] + np.arange(wc)[None, :]
    col_off = col_idx - cols[:, None] + (NA_WIN_COLS - 1)
    rpb_c = rpb[:, :, col_off]
    scale = NA_HEAD_DIM ** -0.5

    def row_block(i):
        r0 = jnp.clip(i - wr // 2, 0, rows - wr)
        q_row = lax.dynamic_index_in_dim(qg, i, axis=1, keepdims=False)
        k_band = lax.dynamic_slice_in_dim(kg, r0, wr, axis=1)
        v_band = lax.dynamic_slice_in_dim(vg, r0, wr, axis=1)
        k_win = k_band[:, :, col_idx]
        v_win = v_band[:, :, col_idx]
        row_off = r0 + jnp.arange(wr) - i + (NA_WIN_ROWS - 1)
        bias = jnp.take(rpb_c, row_off, axis=1).transpose(0, 2, 1, 3)
        s = jnp.einsum('bqhd,brqchd->bhqrc', q_row, k_win).astype(jnp.float32) * scale
        s = s + bias[None].astype(jnp.float32)
        p = jax.nn.softmax(s.reshape(bsz, NA_HEADS, GRID_W, wr * wc), axis=-1)
        p = p.reshape(s.shape).astype(v.dtype)
        return jnp.einsum('bhqrc,brqchd->bqhd', p, v_win)

    out = lax.map(row_block, jnp.arange(rows))
    return out.transpose(1, 0, 2, 3, 4).reshape(bsz, l, NA_WIDTH)


def wkv7_scan(r, w, k, v, a, b, reverse):
    bsz = r.shape[0]
    s0 = jnp.zeros((bsz, RW_HEADS, RW_HEAD_DIM, RW_HEAD_DIM), jnp.float32)

    def step(s, inp):
        r_t, w_t, k_t, v_t, a_t, b_t = inp
        sa = jnp.einsum('bhij,bhj->bhi', s, a_t)
        s = s * w_t[:, :, None, :] + sa[..., None] * b_t[:, :, None, :] + v_t[..., None] * k_t[:, :, None, :]
        return s, jnp.einsum('bhij,bhj->bhi', s, r_t)

    xs = tuple(t.transpose(1, 0, 2, 3) for t in (r, w, k, v, a, b))
    _, y = lax.scan(step, s0, xs, reverse=reverse)
    return y.transpose(1, 0, 2, 3)


def rwkv7_bidirectional(z, conv_w, decay0, decay2, a0, a2, g2, k_k, k_a, r_k, lnx_w, lnx_b):
    bsz, l = z.shape[0], z.shape[1]
    zf = dwconv3(z, conv_w).astype(jnp.float32)
    o1 = RW_WIDTH
    o2 = 2 * RW_WIDTH
    o3 = 3 * RW_WIDTH
    o4 = o3 + 2 * DECAY_LORA
    o5 = o4 + 2 * AAA_LORA
    r, k, v, xw, xa, xg = jnp.split(zf, [o1, o2, o3, o4, o5], axis=-1)

    def heads(t):
        return t.reshape(bsz, l, RW_HEADS, RW_HEAD_DIM)

    kk = heads(k * k_k)
    kk = kk * lax.rsqrt(jnp.maximum(jnp.sum(kk * kk, axis=-1, keepdims=True), 1e-24))
    rh = heads(r)
    vh = heads(v)
    g = jax.nn.sigmoid(xg) @ g2
    ys = []
    bonus = []
    for d in range(2):
        xw_d = xw[..., d * DECAY_LORA:(d + 1) * DECAY_LORA]
        xa_d = xa[..., d * AAA_LORA:(d + 1) * AAA_LORA]
        w_log = -jax.nn.softplus(-(decay0[d] + jnp.tanh(xw_d) @ decay2[d])) - 0.5
        decay = heads(jnp.exp(-jnp.exp(w_log)))
        a = jax.nn.sigmoid(a0[d] + xa_d @ a2[d])
        k_d = heads(k * (1.0 + (a - 1.0) * k_a))
        a_h = heads(a)
        ys.append(wkv7_scan(rh, decay, k_d, vh, -kk, kk * a_h, reverse=(d == 1)))
        bonus.append(jnp.sum(rh * k_d * r_k, axis=-1, keepdims=True) * vh)
    y = ys[0] + ys[1]
    mu = jnp.mean(y, axis=-1, keepdims=True)
    var = jnp.mean(jnp.square(y - mu), axis=-1, keepdims=True)
    y = ((y - mu) * lax.rsqrt(var + GN_EPS)).reshape(bsz, l, RW_WIDTH)
    y = y * lnx_w + lnx_b + (bonus[0] + bonus[1]).reshape(bsz, l, RW_WIDTH)
    return (y * g).astype(z.dtype)


def memory_cross_attention(q, mem_n, w_mem_kv):
    bsz, l = q.shape[0], q.shape[1]
    km, vm = jnp.split(mem_n @ w_mem_kv, 2, axis=-1)
    qh = q.reshape(bsz, l, MEM_HEADS, MEM_HEAD_DIM)
    kh = km.reshape(bsz, MEM_TOKENS, MEM_HEADS, MEM_HEAD_DIM)
    vh = vm.reshape(bsz, MEM_TOKENS, MEM_HEADS, MEM_HEAD_DIM)
    s = jnp.einsum('blhd,bmhd->bhlm', qh, kh).astype(jnp.float32) * (MEM_HEAD_DIM ** -0.5)
    p = jax.nn.softmax(s, axis=-1).astype(q.dtype)
    return jnp.einsum('bhlm,bmhd->blhd', p, vh).reshape(bsz, l, MEM_WIDTH)


def encoder_layer(x, mem, attn_norm, w_in, na_rpb, rw_conv, rw_decay0, rw_decay2, rw_a0, rw_a2,
                  rw_g2, rw_k_k, rw_k_a, rw_r_k, rw_lnx_w, rw_lnx_b, mem_norm, w_mem_kv,
                  w_branch, w_out, ffn_norm, w_up, ffn_conv, ffn_conv_b, w_down):
    bsz, l = x.shape[0], x.shape[1]
    h = rmsnorm(x, attn_norm)
    z = h @ w_in
    c1 = 3 * NA_WIDTH
    c2 = c1 + RW_COLS
    c3 = c2 + MEM_WIDTH
    z_na, z_rw, z_mem, z_gate = jnp.split(z, [c1, c2, c3], axis=-1)
    q, k, v = (t.reshape(bsz, l, NA_HEADS, NA_HEAD_DIM) for t in jnp.split(z_na, 3, axis=-1))
    o_na = neighbourhood_attention(q, k, v, na_rpb)
    o_rw = rwkv7_bidirectional(z_rw, rw_conv, rw_decay0, rw_decay2, rw_a0, rw_a2, rw_g2,
                               rw_k_k, rw_k_a, rw_r_k, rw_lnx_w, rw_lnx_b)
    o_mem = memory_cross_attention(z_mem, rmsnorm(mem, mem_norm), w_mem_kv)
    gates = jax.nn.sigmoid(z_gate.astype(jnp.float32)).astype(x.dtype).reshape(bsz, l, N_BRANCH, D_MODEL)
    merged = (gates[:, :, 0] * (o_na @ w_branch[0])
              + gates[:, :, 1] * (o_rw @ w_branch[1])
              + gates[:, :, 2] * (o_mem @ w_branch[2]))
    x = x + merged @ w_out
    h = rmsnorm(x, ffn_norm)
    u = dwconv3(h @ w_up, ffn_conv) + ffn_conv_b
    u_val, u_gate = jnp.split(u, 2, axis=-1)
    return x + (jax.nn.silu(u_gate) * u_val) @ w_down


def encoder_trunk(x, mem, layer_params, final_norm):
    for i in range(DEPTH):
        x = encoder_layer(x, mem, *[p[i] for p in layer_params])
    return rmsnorm(x, final_norm)


def setup_inputs(seed: int = 0) -> dict:
    key = jax.random.key(seed)
    ks = jax.random.split(key, 32)
    f32 = jnp.float32

    def nrm(k, shape, scale):
        return jax.random.normal(k, shape, f32) * scale

    conv_base = jnp.array([0.2, 0.6, 0.2], f32)[None, :, None]
    return {
        'x_prompt': nrm(ks[0], (BATCH, SEQ, D_MODEL), 1.0),
        'x_sample': nrm(ks[1], (DEC_BATCH, DEC_SEQ, D_MODEL), 1.0),
        'mem_prompt': nrm(ks[2], (BATCH, MEM_TOKENS, D_MODEL), 1.0),
        'mem_sample': nrm(ks[3], (DEC_BATCH, MEM_TOKENS, D_MODEL), 1.0),
        'attn_norm': 1.0 + nrm(ks[4], (DEPTH, D_MODEL), 0.05),
        'w_in': nrm(ks[5], (DEPTH, D_MODEL, IN_COLS), D_MODEL ** -0.5),
        'na_rpb': nrm(ks[6], (DEPTH, NA_HEADS, 2 * NA_WIN_ROWS - 1, 2 * NA_WIN_COLS - 1), 0.3),
        'rw_conv': conv_base + nrm(ks[7], (DEPTH, 3, RW_COLS), 0.1),
        'rw_decay0': jax.random.uniform(ks[8], (DEPTH, 2, RW_WIDTH), f32, -6.0, 1.0),
        'rw_decay2': nrm(ks[9], (DEPTH, 2, DECAY_LORA, RW_WIDTH), 0.1),
        'rw_a0': nrm(ks[10], (DEPTH, 2, RW_WIDTH), 0.5),
        'rw_a2': nrm(ks[11], (DEPTH, 2, AAA_LORA, RW_WIDTH), AAA_LORA ** -0.5),
        'rw_g2': nrm(ks[12], (DEPTH, GATE_LORA, RW_WIDTH), GATE_LORA ** -0.5),
        'rw_k_k': 0.85 + nrm(ks[13], (DEPTH, RW_WIDTH), 0.05),
        'rw_k_a': 1.0 + nrm(ks[14], (DEPTH, RW_WIDTH), 0.05),
        'rw_r_k': nrm(ks[15], (DEPTH, RW_HEADS, RW_HEAD_DIM), 0.1),
        'rw_lnx_w': 1.0 + nrm(ks[16], (DEPTH, RW_WIDTH), 0.05),
        'rw_lnx_b': nrm(ks[17], (DEPTH, RW_WIDTH), 0.01),
        'mem_norm': 1.0 + nrm(ks[18], (DEPTH, D_MODEL), 0.05),
        'w_mem_kv': nrm(ks[19], (DEPTH, D_MODEL, 2 * MEM_WIDTH), D_MODEL ** -0.5),
        'w_branch': nrm(ks[20], (DEPTH, N_BRANCH, BRANCH_WIDTH, D_MODEL), BRANCH_WIDTH ** -0.5),
        'w_out': nrm(ks[21], (DEPTH, D_MODEL, D_MODEL), D_MODEL ** -0.5),
        'ffn_norm': 1.0 + nrm(ks[22], (DEPTH, D_MODEL), 0.05),
        'w_up': nrm(ks[23], (DEPTH, D_MODEL, 2 * D_FF), D_MODEL ** -0.5),
        'ffn_conv': conv_base + nrm(ks[24], (DEPTH, 3, 2 * D_FF), 0.1),
        'ffn_conv_b': nrm(ks[25], (DEPTH, 2 * D_FF), 0.01),
        'w_down': nrm(ks[26], (DEPTH, D_FF, D_MODEL), D_FF ** -0.5),
        'final_norm': 1.0 + nrm(ks[27], (D_MODEL,), 0.05),
    }


def reference(x_prompt, x_sample, mem_prompt, mem_sample, attn_norm, w_in, na_rpb, rw_conv,
              rw_decay0, rw_decay2, rw_a0, rw_a2, rw_g2, rw_k_k, rw_k_a, rw_r_k, rw_lnx_w,
              rw_lnx_b, mem_norm, w_mem_kv, w_branch, w_out, ffn_norm, w_up, ffn_conv,
              ffn_conv_b, w_down, final_norm):
    layer_params = (attn_norm, w_in, na_rpb, rw_conv, rw_decay0, rw_decay2, rw_a0, rw_a2,
                    rw_g2, rw_k_k, rw_k_a, rw_r_k, rw_lnx_w, rw_lnx_b, mem_norm, w_mem_kv,
                    w_branch, w_out, ffn_norm, w_up, ffn_conv, ffn_conv_b, w_down)
    y_prompt = encoder_trunk(x_prompt, mem_prompt, layer_params, final_norm)
    y_sample = encoder_trunk(x_sample, mem_sample, layer_params, final_norm)
    return (y_prompt, y_sample)
```

```python
import functools

import jax
import jax.numpy as jnp
import numpy as np
from jax import lax
from jax.experimental import pallas as pl
from jax.experimental.pallas import tpu as pltpu

F32 = jnp.float32
BF16 = jnp.bfloat16

D_MODEL = 1024
GRID_W = 64
NA_HEADS = 8
NA_HEAD_DIM = 64
NA_WIDTH = 512
NA_WIN_ROWS = 8
NA_WIN_COLS = 16
RW_HEADS = 8
RW_HEAD_DIM = 64
RW_WIDTH = 512
DECAY_LORA = 64
AAA_LORA = 64
GATE_LORA = 160
GATE_LORA_PAD = 256
MEM_TOKENS = 256
MEM_HEADS = 4
MEM_HEAD_DIM = 128
MEM_WIDTH = 512
N_BRANCH = 3
D_FF = 2816
RMS_EPS = 1e-6
GN_EPS = 64e-5

RW_COLS = 3 * RW_WIDTH + 2 * DECAY_LORA + 2 * AAA_LORA + GATE_LORA
RW_COLS_PAD = 3 * RW_WIDTH + 2 * DECAY_LORA + 2 * AAA_LORA + GATE_LORA_PAD
Z_RW = 0
Z_NA = RW_COLS_PAD
Z_MEM = Z_NA + 3 * NA_WIDTH
Z_GATE = Z_MEM + MEM_WIDTH
Z_COLS = Z_GATE + N_BRANCH * D_MODEL

CHUNK = 64
PAIR = 2 * RW_HEAD_DIM
N_PAIRS = RW_HEADS // 2
INV_LEVELS = 6
NEG_BIG = -1e30
VMEM_LIMIT = 56 * 1024 * 1024


def _cparams(sem):
    return pltpu.CompilerParams(dimension_semantics=sem, vmem_limit_bytes=VMEM_LIMIT)


def _rms(x, gain):
    ms = jnp.mean(x * x, axis=-1, keepdims=True)
    return x * lax.rsqrt(ms + RMS_EPS) * gain


def _dot(a, b):
    return jnp.dot(a, b, preferred_element_type=F32)


def _dot_nt(a, b):
    return lax.dot_general(a, b, (((1,), (1,)), ((), ())), preferred_element_type=F32)


def _split_hi_lo(x):
    hi = x.astype(BF16)
    lo = (x - hi.astype(F32)).astype(BF16)
    return hi, lo


def _sel_left(sel, x):
    hi, lo = _split_hi_lo(x)
    return _dot(sel, hi) + _dot(sel, lo)


def _sel_right(x, sel):
    hi, lo = _split_hi_lo(x)
    return _dot(hi, sel) + _dot(lo, sel)


def _inproj_kernel(x_ref, g_ref, w_ref, o_ref, h_ref):
    @pl.when(pl.program_id(1) == 0)
    def _():
        h_ref[...] = _rms(x_ref[...], g_ref[...]).astype(BF16)

    o_ref[...] = _dot(h_ref[...], w_ref[...]).astype(o_ref.dtype)


def _in_proj(x2d, gain, w):
    t, n = x2d.shape[0], w.shape[1]
    tm = min(1024, t)
    tn = 1024
    return pl.pallas_call(
        _inproj_kernel,
        grid=(t // tm, n // tn),
        in_specs=[
            pl.BlockSpec((tm, D_MODEL), lambda i, j: (i, 0)),
            pl.BlockSpec((1, D_MODEL), lambda i, j: (0, 0)),
            pl.BlockSpec((D_MODEL, tn), lambda i, j: (0, j)),
        ],
        out_specs=pl.BlockSpec((tm, tn), lambda i, j: (i, j)),
        out_shape=jax.ShapeDtypeStruct((t, n), BF16),
        scratch_shapes=[pltpu.VMEM((tm, D_MODEL), BF16)],
        compiler_params=_cparams(("parallel", "arbitrary")),
        name="in_proj",
    )(x2d, gain, w)


def _na_kernel(q_ref, k_ref, v_ref, b_ref, o_ref):
    band = NA_WIN_ROWS * GRID_W
    q = q_ref[...]
    k = k_ref[...].reshape(band, NA_WIDTH)
    v = v_ref[...].reshape(band, NA_WIDTH)
    scale = NA_HEAD_DIM ** -0.5
    outs = []
    for h in range(NA_HEADS):
        sl = slice(h * NA_HEAD_DIM, (h + 1) * NA_HEAD_DIM)
        s = _dot_nt(q[:, sl], k[:, sl]) * scale + b_ref[h]
        m = jnp.max(s, axis=-1, keepdims=True)
        p = jnp.exp(s - m)
        l = jnp.sum(p, axis=-1, keepdims=True)
        outs.append(_dot(p.astype(BF16), v[:, sl]) / l)
    o_ref[...] = jnp.concatenate(outs, axis=-1).astype(o_ref.dtype)


def _na_bias_table(rpb):
    e = np.arange(NA_WIN_ROWS)[:, None]
    j = np.arange(NA_WIN_ROWS)[None, :]
    dr = j - e + (NA_WIN_ROWS - 1)
    x = np.arange(GRID_W)[:, None]
    c = np.arange(GRID_W)[None, :]
    c0 = np.clip(x - NA_WIN_COLS // 2, 0, GRID_W - NA_WIN_COLS)
    valid = (c >= c0) & (c < c0 + NA_WIN_COLS)
    dc = np.clip(c - x + (NA_WIN_COLS - 1), 0, 2 * NA_WIN_COLS - 2)
    tab = rpb[:, dr[:, :, None, None], dc[None, None, :, :]]
    tab = jnp.where(valid[None, None, None], tab, NEG_BIG)
    tab = tab.transpose(1, 0, 3, 2, 4)
    return tab.reshape(NA_WIN_ROWS, NA_HEADS, GRID_W, NA_WIN_ROWS * GRID_W).astype(F32)


def _na_attention(z, bias_tab, bsz, seq):
    rows = seq // GRID_W
    assert rows >= NA_WIN_ROWS
    z4 = z.reshape(bsz, rows, GRID_W, Z_COLS)
    half = NA_WIN_ROWS // 2

    def band_start(i):
        return jnp.clip(i - half, 0, rows - NA_WIN_ROWS)

    def kv_spec(col):
        return pl.BlockSpec(
            (pl.Squeezed(), pl.Element(NA_WIN_ROWS), pl.Element(GRID_W), pl.Element(NA_WIDTH)),
            lambda b, i: (b, band_start(i), 0, col))

    qblk = Z_NA // NA_WIDTH
    out = pl.pallas_call(
        _na_kernel,
        grid=(bsz, rows),
        in_specs=[
            pl.BlockSpec((None, None, GRID_W, NA_WIDTH), lambda b, i: (b, i, 0, qblk)),
            kv_spec(Z_NA + NA_WIDTH),
            kv_spec(Z_NA + 2 * NA_WIDTH),
            pl.BlockSpec((None, NA_HEADS, GRID_W, NA_WIN_ROWS * GRID_W),
                         lambda b, i: (i - band_start(i), 0, 0, 0)),
        ],
        out_specs=pl.BlockSpec((None, None, GRID_W, NA_WIDTH), lambda b, i: (b, i, 0, 0)),
        out_shape=jax.ShapeDtypeStruct((bsz, rows, GRID_W, NA_WIDTH), BF16),
        compiler_params=_cparams(("parallel", "arbitrary")),
        name="na_attn",
    )(z4, z4, z4, bias_tab)
    return out.reshape(bsz * seq, NA_WIDTH)


def _softplus(u):
    return jnp.maximum(u, 0.0) + jnp.log1p(jnp.exp(-jnp.abs(u)))


def _rwkv_kernel(zc_ref, zp_ref, zn_ref, cw_ref, d0_ref, d2_ref, a0_ref, a2_ref, g2_ref,
                 kk_ref, ka_ref, rk_ref, bones_ref, tri_ref, obd_ref,
                 y_ref, bonus_ref, gate_ref,
                 s_ref, rt_ref, at_ref, bt_ref, kt_ref, bh_ref, kh_ref, v_ref, lw_ref,
                 *, tile):
    d = pl.program_id(0)
    i = pl.program_id(2)
    nt = pl.num_programs(2)
    ti = jnp.where(d == 0, i, nt - 1 - i)
    n_chunks = tile // CHUNK

    @pl.when(i == 0)
    def _():
        s_ref[...] = jnp.zeros_like(s_ref)

    zc = zc_ref[...].astype(F32)
    prev_row = jnp.where(ti == 0, 0.0, zp_ref[...].astype(F32)[7:8, :])
    next_row = jnp.where(ti == nt - 1, 0.0, zn_ref[...].astype(F32)[0:1, :])
    rows = lax.broadcasted_iota(jnp.int32, (tile, 1), 0)
    z_prev = jnp.where(rows == 0, prev_row, pltpu.roll(zc, 1, axis=0))
    z_next = jnp.where(rows == tile - 1, next_row, pltpu.roll(zc, tile - 1, axis=0))
    zf = z_prev * cw_ref[0:1, :] + zc * cw_ref[1:2, :] + z_next * cw_ref[2:3, :]

    o1, o2, o3 = RW_WIDTH, 2 * RW_WIDTH, 3 * RW_WIDTH
    o4 = o3 + 2 * DECAY_LORA
    o5 = o4 + 2 * AAA_LORA
    r = zf[:, 0:o1]
    k = zf[:, o1:o2]
    v = zf[:, o2:o3]
    xw = zf[:, o3:o4]
    xa = zf[:, o4:o5]
    xg = zf[:, o5:RW_COLS_PAD]

    bones = bones_ref[...]
    kk = k * kk_ref[...]
    kk = kk * lax.rsqrt(jnp.maximum(_sel_right(kk * kk, bones), 1e-24))
    gate_ref[...] = _dot(jax.nn.sigmoid(xg).astype(BF16), g2_ref[...]).astype(gate_ref.dtype)

    w_log = -_softplus(-(d0_ref[...] + _dot(jnp.tanh(xw).astype(BF16), d2_ref[...]))) - 0.5
    logw = -jnp.exp(w_log)
    a = jax.nn.sigmoid(a0_ref[...] + _dot(xa.astype(BF16), a2_ref[...]))
    k_d = k * (1.0 + (a - 1.0) * ka_ref[...])
    b = kk * a
    bonus_ref[...] = (_sel_right(r * k_d * rk_ref[...], bones) * v).astype(bonus_ref.dtype)

    g = _sel_left(tri_ref[...], logw)
    tot = _sel_left(obd_ref[...], logw)
    eng = jnp.exp(-g)
    ehat = jnp.exp(tot - g)
    rt_ref[...] = r * jnp.exp(g)
    at_ref[...] = -kk * jnp.exp(g - logw)
    bt_ref[...] = b * eng
    kt_ref[...] = k_d * eng
    bh_ref[...] = b * ehat
    kh_ref[...] = k_d * ehat
    v_ref[...] = v
    lw_ref[...] = logw

    ri = lax.broadcasted_iota(jnp.int32, (PAIR, PAIR), 0)
    ci = lax.broadcasted_iota(jnp.int32, (PAIR, PAIR), 1)
    same_head = jnp.where(jnp.right_shift(ri, 6) == jnp.right_shift(ci, 6), 1.0, 0.0).astype(F32)
    order = (ri - ci) * jnp.where(d == 0, 1, -1)
    m_strict = jnp.where(order > 0, same_head, 0.0)
    m_incl = jnp.where(order >= 0, same_head, 0.0)
    eye = jnp.where(ci == ri, 1.0, 0.0).astype(F32)
    lvl = []
    for q in range(INV_LEVELS):
        in_block = jnp.where(jnp.right_shift(ri, q + 1) == jnp.right_shift(ci, q + 1), 1.0, 0.0).astype(F32)
        lvl.append(jnp.where(jnp.right_shift(ri, q) != jnp.right_shift(ci, q), in_block, 0.0))
    lane = lax.broadcasted_iota(jnp.int32, (CHUNK, PAIR), 1)
    left = lane < RW_HEAD_DIM

    def stack(x):
        return jnp.concatenate([jnp.where(left, x, 0.0), jnp.where(left, 0.0, x)], axis=0)

    def chunk_step(j, carry):
        cc = jnp.where(d == 0, j, n_chunks - 1 - j)
        off = pl.multiple_of(cc * CHUNK, CHUNK)
        for p in range(N_PAIRS):
            ls = slice(p * PAIR, (p + 1) * PAIR)
            rt = stack(rt_ref[pl.ds(off, CHUNK), ls])
            at = stack(at_ref[pl.ds(off, CHUNK), ls])
            vs = stack(v_ref[pl.ds(off, CHUNK), ls])
            bt = bt_ref[pl.ds(off, CHUNK), ls].astype(BF16)
            kt = kt_ref[pl.ds(off, CHUNK), ls].astype(BF16)
            bh_t = stack(bh_ref[pl.ds(off, CHUNK), ls]).T
            kh_t = stack(kh_ref[pl.ds(off, CHUNK), ls]).T
            lw_t = stack(lw_ref[pl.ds(off, CHUNK), ls]).T
            w_tot = jnp.exp(jnp.sum(lw_t, axis=1, keepdims=True))

            at_b = at.astype(BF16)
            lhs1 = jnp.concatenate([at_b, rt.astype(BF16)], axis=0)
            rhs1 = jnp.concatenate([bt, bt, kt, kt], axis=0)
            o1_ = _dot_nt(lhs1, rhs1)
            n_ab = o1_[0:PAIR, 0:PAIR] * m_strict
            a_ak = o1_[0:PAIR, PAIR:] * m_strict
            a_rb = o1_[PAIR:, 0:PAIR] * m_incl
            a_rk = o1_[PAIR:, PAIR:] * m_incl

            t_inv = eye + n_ab * lvl[0]
            for q in range(1, INV_LEVELS):
                xq = _dot((n_ab * lvl[q]).astype(BF16), t_inv.astype(BF16))
                t_inv = t_inv + _dot(t_inv.astype(BF16), xq.astype(BF16))

            vs_b = vs.astype(BF16)
            av = _dot(a_ak.astype(BF16), vs_b)
            pu = _dot(t_inv.astype(BF16), jnp.concatenate([at_b, av.astype(BF16)], axis=1))
            w_top = pu.astype(BF16)
            w_bot = jnp.concatenate([jnp.zeros((PAIR, PAIR), BF16), vs_b], axis=1)
            wmat = jnp.concatenate([w_top, w_bot], axis=0)
            lhs2 = jnp.concatenate(
                [jnp.concatenate([a_rb, a_rk], axis=1), jnp.concatenate([bh_t, kh_t], axis=1)],
                axis=0).astype(BF16)
            o2_ = _dot(lhs2, wmat)
            q_mat = rt + o2_[0:PAIR, 0:PAIR]
            y_loc = o2_[0:PAIR, PAIR:]
            btp = o2_[PAIR:, 0:PAIR]
            n_st = o2_[PAIR:, PAIR:]

            st = s_ref[p]
            o3_ = _dot(jnp.concatenate([q_mat, btp], axis=0).astype(BF16), st.astype(BF16))
            y_bs = o3_[0:PAIR] + y_loc
            s_ref[p] = st * w_tot + o3_[PAIR:] + n_st
            y_ref[pl.ds(off, CHUNK), ls] = y_bs[0:CHUNK] + y_bs[CHUNK:]
        return carry

    lax.fori_loop(0, n_chunks, chunk_step, 0)


def _rwkv_scan(z, lp, bsz, seq):
    tile = min(256, seq)
    nt = seq // tile
    z3 = z.reshape(bsz, seq, Z_COLS)
    t8 = tile // 8

    def tidx(d, i):
        return jnp.where(d == 0, i, nt - 1 - i)

    cidx = np.arange(tile)
    same = (cidx[:, None] // CHUNK) == (cidx[None, :] // CHUNK)
    tri = np.stack([same & (cidx[None, :] <= cidx[:, None]), same & (cidx[None, :] >= cidx[:, None])])
    tri = jnp.asarray(tri, BF16)
    obd = jnp.asarray(same, BF16)

    def full(shape):
        return pl.BlockSpec(shape, lambda d, b, i: (0,) * len(shape))

    def per_dir(shape):
        return pl.BlockSpec((None,) + shape, lambda d, b, i: (d,) + (0,) * len(shape))

    out_sds = jax.ShapeDtypeStruct((2, bsz, seq, RW_WIDTH), F32)
    out_bf = jax.ShapeDtypeStruct((2, bsz, seq, RW_WIDTH), BF16)
    out_spec = pl.BlockSpec((None, None, tile, RW_WIDTH), lambda d, b, i: (d, b, tidx(d, i), 0))
    nat = pltpu.VMEM((tile, RW_WIDTH), F32)
    return pl.pallas_call(
        functools.partial(_rwkv_kernel, tile=tile),
        grid=(2, bsz, nt),
        in_specs=[
            pl.BlockSpec((None, tile, RW_COLS_PAD), lambda d, b, i: (b, tidx(d, i), 0)),
            pl.BlockSpec((None, 8, RW_COLS_PAD),
                         lambda d, b, i: (b, jnp.maximum(tidx(d, i) * t8 - 1, 0), 0)),
            pl.BlockSpec((None, 8, RW_COLS_PAD),
                         lambda d, b, i: (b, jnp.minimum((tidx(d, i) + 1) * t8, seq // 8 - 1), 0)),
            full((3, RW_COLS_PAD)),
            per_dir((1, RW_WIDTH)), per_dir((2 * DECAY_LORA, RW_WIDTH)),
            per_dir((1, RW_WIDTH)), per_dir((2 * AAA_LORA, RW_WIDTH)),
            full((GATE_LORA_PAD, RW_WIDTH)),
            full((1, RW_WIDTH)), full((1, RW_WIDTH)), full((1, RW_WIDTH)),
            full((RW_WIDTH, RW_WIDTH)),
            per_dir((tile, tile)), full((tile, tile)),
        ],
        out_specs=[out_spec, out_spec, out_spec],
        out_shape=[out_sds, out_bf, out_bf],
        scratch_shapes=[pltpu.VMEM((N_PAIRS, PAIR, PAIR), F32)] + [nat] * 8,
        compiler_params=_cparams(("arbitrary", "arbitrary", "arbitrary")),
        name="rwkv_scan",
    )(z3, z3, z3, lp["rw_conv"], lp["rw_decay0"], lp["rw_decay2"], lp["rw_a0"], lp["rw_a2"],
      lp["rw_g2"], lp["rw_k_k"], lp["rw_k_a"], lp["rw_r_k"], lp["bones"], tri, obd)


def _memkv_kernel(m_ref, g_ref, w_ref, o_ref):
    o_ref[...] = _dot(_rms(m_ref[...], g_ref[...]).astype(BF16), w_ref[...]).astype(o_ref.dtype)


def _mem_kv(mem, gain, w):
    bsz = mem.shape[0]
    return pl.pallas_call(
        _memkv_kernel,
        grid=(bsz,),
        in_specs=[
            pl.BlockSpec((None, MEM_TOKENS, D_MODEL), lambda b: (b, 0, 0)),
            pl.BlockSpec((1, D_MODEL), lambda b: (0, 0)),
            pl.BlockSpec((D_MODEL, 2 * MEM_WIDTH), lambda b: (0, 0)),
        ],
        out_specs=pl.BlockSpec((None, MEM_TOKENS, 2 * MEM_WIDTH), lambda b: (b, 0, 0)),
        out_shape=jax.ShapeDtypeStruct((bsz, MEM_TOKENS, 2 * MEM_WIDTH), BF16),
        compiler_params=_cparams(("parallel",)),
        name="mem_kv",
    )(mem, gain, w)


def _memattn_kernel(q_ref, kv_ref, o_ref):
    scale = MEM_HEAD_DIM ** -0.5
    outs = []
    for h in range(MEM_HEADS):
        sl = slice(h * MEM_HEAD_DIM, (h + 1) * MEM_HEAD_DIM)
        vsl = slice(MEM_WIDTH + h * MEM_HEAD_DIM, MEM_WIDTH + (h + 1) * MEM_HEAD_DIM)
        s = _dot_nt(q_ref[:, sl], kv_ref[:, sl]) * scale
        m = jnp.max(s, axis=-1, keepdims=True)
        p = jnp.exp(s - m)
        l = jnp.sum(p, axis=-1, keepdims=True)
        outs.append(_dot(p.astype(BF16), kv_ref[:, vsl]) / l)
    o_ref[...] = jnp.concatenate(outs, axis=-1).astype(o_ref.dtype)


def _mem_attention(z, kv, bsz, seq):
    tm = min(1024, seq)
    z3 = z.reshape(bsz, seq, Z_COLS)
    out = pl.pallas_call(
        _memattn_kernel,
        grid=(bsz, seq // tm),
        in_specs=[
            pl.BlockSpec((None, tm, MEM_WIDTH), lambda b, i: (b, i, Z_MEM // MEM_WIDTH)),
            pl.BlockSpec((None, MEM_TOKENS, 2 * MEM_WIDTH), lambda b, i: (b, 0, 0)),
        ],
        out_specs=pl.BlockSpec((None, tm, MEM_WIDTH), lambda b, i: (b, i, 0)),
        out_shape=jax.ShapeDtypeStruct((bsz, seq, MEM_WIDTH), BF16),
        compiler_params=_cparams(("parallel", "parallel")),
        name="mem_attn",
    )(z3, kv)
    return out.reshape(bsz * seq, MEM_WIDTH)


def _merge_kernel(x_ref, ona_ref, y0_ref, y1_ref, b0_ref, b1_ref, gt_ref, omem_ref,
                  g0_ref, g1_ref, g2_ref, wb_ref, wo_ref, lw_ref, lb_ref, bones_ref, o_ref):
    bones = bones_ref[...]
    inv_n = 1.0 / RW_HEAD_DIM
    y = y0_ref[...] + y1_ref[...]
    mu = _sel_right(y, bones) * inv_n
    dl = y - mu
    var = _sel_right(dl * dl, bones) * inv_n
    yn = dl * lax.rsqrt(var + GN_EPS)
    bonus = b0_ref[...].astype(F32) + b1_ref[...].astype(F32)
    o_rw = (yn * lw_ref[...] + lb_ref[...] + bonus) * gt_ref[...].astype(F32)

    merged = jax.nn.sigmoid(g0_ref[...].astype(F32)) * _dot(ona_ref[...], wb_ref[0])
    merged = merged + jax.nn.sigmoid(g1_ref[...].astype(F32)) * _dot(o_rw.astype(BF16), wb_ref[1])
    merged = merged + jax.nn.sigmoid(g2_ref[...].astype(F32)) * _dot(omem_ref[...], wb_ref[2])
    o_ref[...] = x_ref[...] + _dot(merged.astype(BF16), wo_ref[...])


def _merge(x2d, z, o_na, y, bonus, gate, o_mem, lp):
    t = x2d.shape[0]
    tm = min(512, t)
    y2 = y.reshape(2, t, RW_WIDTH)
    b2 = bonus.reshape(2, t, RW_WIDTH)
    g2 = gate.reshape(2, t, RW_WIDTH)
    gblk = Z_GATE // D_MODEL

    def tok(width):
        return pl.BlockSpec((tm, width), lambda i: (i, 0))

    def dir_tok(dd):
        return pl.BlockSpec((None, tm, RW_WIDTH), lambda i: (dd, i, 0))

    def const(shape):
        return pl.BlockSpec(shape, lambda i: (0,) * len(shape))

    return pl.pallas_call(
        _merge_kernel,
        grid=(t // tm,),
        in_specs=[
            tok(D_MODEL), tok(NA_WIDTH), dir_tok(0), dir_tok(1), dir_tok(0), dir_tok(1), dir_tok(0),
            tok(MEM_WIDTH),
            pl.BlockSpec((tm, D_MODEL), lambda i: (i, gblk)),
            pl.BlockSpec((tm, D_MODEL), lambda i: (i, gblk + 1)),
            pl.BlockSpec((tm, D_MODEL), lambda i: (i, gblk + 2)),
            const((N_BRANCH, NA_WIDTH, D_MODEL)), const((D_MODEL, D_MODEL)),
            const((1, RW_WIDTH)), const((1, RW_WIDTH)), const((RW_WIDTH, RW_WIDTH)),
        ],
        out_specs=tok(D_MODEL),
        out_shape=jax.ShapeDtypeStruct((t, D_MODEL), F32),
        compiler_params=_cparams(("parallel",)),
        name="merge",
    )(x2d, o_na, y2, y2, b2, b2, g2, o_mem, z, z, z, lp["w_branch"], lp["w_out"],
      lp["rw_lnx_w"], lp["rw_lnx_b"], lp["bones"])


def _ffn_kernel(x_ref, xp_ref, xn_ref, g_ref, wv_ref, wg_ref, cv_ref, cg_ref, bv_ref, bg_ref, wd_ref,
                fg_ref, o_ref, h_ref, acc_ref, *, tm, tiles_per_seq, final_norm):
    i = pl.program_id(0)
    j = pl.program_id(1)
    ext = tm + 16

    @pl.when(j == 0)
    def _():
        gain = g_ref[...]
        h_ref[0:8, :] = _rms(xp_ref[...], gain).astype(BF16)
        h_ref[8:8 + tm, :] = _rms(x_ref[...], gain).astype(BF16)
        h_ref[8 + tm:ext, :] = _rms(xn_ref[...], gain).astype(BF16)
        acc_ref[...] = jnp.zeros_like(acc_ref)

    seq_pos = i % tiles_per_seq
    rows = lax.broadcasted_iota(jnp.int32, (tm, 1), 0)
    kill_prev = (rows == 0) & (seq_pos == 0)
    kill_next = (rows == tm - 1) & (seq_pos == tiles_per_seq - 1)
    hx = h_ref[...]

    def conv(u, cw, bias):
        up = jnp.where(kill_prev, 0.0, pltpu.roll(u, 1, axis=0)[8:8 + tm])
        un = jnp.where(kill_next, 0.0, pltpu.roll(u, ext - 1, axis=0)[8:8 + tm])
        return up * cw[0:1, :] + u[8:8 + tm] * cw[1:2, :] + un * cw[2:3, :] + bias

    u_val = conv(_dot(hx, wv_ref[...]), cv_ref[...], bv_ref[...])
    u_gate = conv(_dot(hx, wg_ref[...]), cg_ref[...], bg_ref[...])
    act = (u_gate * jax.nn.sigmoid(u_gate) * u_val).astype(BF16)
    acc_ref[...] += _dot(act, wd_ref[...])

    @pl.when(j == pl.num_programs(1) - 1)
    def _():
        out = x_ref[...] + acc_ref[...]
        if final_norm:
            out = _rms(out, fg_ref[...])
        o_ref[...] = out


def _ffn(x2d, lp, final_gain, seq, final_norm):
    t = x2d.shape[0]
    tm = min(1024, seq)
    tn = 256
    nj = D_FF // tn
    t8 = tm // 8
    return pl.pallas_call(
        functools.partial(_ffn_kernel, tm=tm, tiles_per_seq=seq // tm, final_norm=final_norm),
        grid=(t // tm, nj),
        in_specs=[
            pl.BlockSpec((tm, D_MODEL), lambda i, j: (i, 0)),
            pl.BlockSpec((8, D_MODEL), lambda i, j: (jnp.maximum(i * t8 - 1, 0), 0)),
            pl.BlockSpec((8, D_MODEL), lambda i, j: (jnp.minimum((i + 1) * t8, t // 8 - 1), 0)),
            pl.BlockSpec((1, D_MODEL), lambda i, j: (0, 0)),
            pl.BlockSpec((D_MODEL, tn), lambda i, j: (0, j)),
            pl.BlockSpec((D_MODEL, tn), lambda i, j: (0, nj + j)),
            pl.BlockSpec((3, tn), lambda i, j: (0, j)),
            pl.BlockSpec((3, tn), lambda i, j: (0, nj + j)),
            pl.BlockSpec((1, tn), lambda i, j: (0, j)),
            pl.BlockSpec((1, tn), lambda i, j: (0, nj + j)),
            pl.BlockSpec((tn, D_MODEL), lambda i, j: (j, 0)),
            pl.BlockSpec((1, D_MODEL), lambda i, j: (0, 0)),
        ],
        out_specs=pl.BlockSpec((tm, D_MODEL), lambda i, j: (i, 0)),
        out_shape=jax.ShapeDtypeStruct((t, D_MODEL), F32),
        scratch_shapes=[pltpu.VMEM((tm + 16, D_MODEL), BF16), pltpu.VMEM((tm, D_MODEL), F32)],
        compiler_params=_cparams(("parallel", "arbitrary")),
        name="ffn",
    )(x2d, x2d, x2d, lp["ffn_norm"], lp["w_up"], lp["w_up"], lp["ffn_conv"], lp["ffn_conv"],
      lp["ffn_conv_b"], lp["ffn_conv_b"], lp["w_down"], final_gain)


def _prep_layer(l, attn_norm, w_in, na_rpb, rw_conv, rw_decay0, rw_decay2, rw_a0, rw_a2, rw_g2, rw_k_k,
                rw_k_a, rw_r_k, rw_lnx_w, rw_lnx_b, mem_norm, w_mem_kv, w_branch, w_out, ffn_norm, w_up,
                ffn_conv, ffn_conv_b, w_down):
    c1 = 3 * NA_WIDTH
    c2 = c1 + RW_COLS
    w = w_in[l]
    pad_cols = RW_COLS_PAD - RW_COLS
    w_new = jnp.concatenate(
        [w[:, c1:c2], jnp.zeros((D_MODEL, pad_cols), w.dtype), w[:, :c1], w[:, c2:]], axis=1)

    def lora_rows(m):
        zero = jnp.zeros_like(m[0])
        return jnp.stack([jnp.concatenate([m[0], zero], 0), jnp.concatenate([zero, m[1]], 0)]).astype(BF16)

    head = np.arange(RW_WIDTH) // RW_HEAD_DIM
    return dict(
        attn_norm=attn_norm[l][None, :],
        w_in=w_new.astype(BF16),
        na_bias=_na_bias_table(na_rpb[l]),
        rw_conv=jnp.pad(rw_conv[l], ((0, 0), (0, pad_cols))),
        rw_decay0=rw_decay0[l][:, None, :],
        rw_decay2=lora_rows(rw_decay2[l]),
        rw_a0=rw_a0[l][:, None, :],
        rw_a2=lora_rows(rw_a2[l]),
        rw_g2=jnp.pad(rw_g2[l], ((0, GATE_LORA_PAD - GATE_LORA), (0, 0))).astype(BF16),
        rw_k_k=rw_k_k[l][None, :],
        rw_k_a=rw_k_a[l][None, :],
        rw_r_k=rw_r_k[l].reshape(1, RW_WIDTH),
        rw_lnx_w=rw_lnx_w[l][None, :],
        rw_lnx_b=rw_lnx_b[l][None, :],
        bones=jnp.asarray(head[:, None] == head[None, :], BF16),
        mem_norm=mem_norm[l][None, :],
        w_mem_kv=w_mem_kv[l].astype(BF16),
        w_branch=w_branch[l].astype(BF16),
        w_out=w_out[l].astype(BF16),
        ffn_norm=ffn_norm[l][None, :],
        w_up=w_up[l].astype(BF16),
        ffn_conv=ffn_conv[l],
        ffn_conv_b=ffn_conv_b[l][None, :],
        w_down=w_down[l].astype(BF16),
    )


def _layer(x2d, mem, lp, bsz, seq, final_gain, final_norm):
    z = _in_proj(x2d, lp["attn_norm"], lp["w_in"])
    o_na = _na_attention(z, lp["na_bias"], bsz, seq)
    y, bonus, gate = _rwkv_scan(z, lp, bsz, seq)
    o_mem = _mem_attention(z, _mem_kv(mem, lp["mem_norm"], lp["w_mem_kv"]), bsz, seq)
    x2d = _merge(x2d, z, o_na, y, bonus, gate, o_mem, lp)
    return _ffn(x2d, lp, final_gain, seq, final_norm)


def _trunk(x, mem, layers, final_gain):
    bsz, seq, _ = x.shape
    x2d = x.reshape(bsz * seq, D_MODEL)
    for l, lp in enumerate(layers):
        x2d = _layer(x2d, mem, lp, bsz, seq, final_gain, l == len(layers) - 1)
    return x2d.reshape(bsz, seq, D_MODEL)


def kernel(x_prompt, x_sample, mem_prompt, mem_sample, attn_norm, w_in, na_rpb, rw_conv, rw_decay0, rw_decay2, rw_a0, rw_a2, rw_g2, rw_k_k, rw_k_a, rw_r_k, rw_lnx_w, rw_lnx_b, mem_norm, w_mem_kv, w_branch, w_out, ffn_norm, w_up, ffn_conv, ffn_conv_b, w_down, final_norm):
    params = (attn_norm, w_in, na_rpb, rw_conv, rw_decay0, rw_decay2, rw_a0, rw_a2, rw_g2, rw_k_k, rw_k_a,
              rw_r_k, rw_lnx_w, rw_lnx_b, mem_norm, w_mem_kv, w_branch, w_out, ffn_norm, w_up, ffn_conv,
              ffn_conv_b, w_down)
    layers = [_prep_layer(l, *params) for l in range(attn_norm.shape[0])]
    final_gain = final_norm[None, :]
    y_prompt = _trunk(x_prompt, mem_prompt, layers, final_gain)
    y_sample = _trunk(x_sample, mem_sample, layers, final_gain)
    return (y_prompt, y_sample)
```

```python
import functools

import jax
import jax.numpy as jnp
import numpy as np
from jax import lax
from jax.experimental import pallas as pl
from jax.experimental.pallas import tpu as pltpu

F32 = jnp.float32
BF16 = jnp.bfloat16

D_MODEL = 1024
GRID_W = 64
NA_HEADS = 8
NA_HEAD_DIM = 64
NA_WIDTH = 512
NA_WIN_ROWS = 8
NA_WIN_COLS = 16
RW_HEADS = 8
RW_HEAD_DIM = 64
RW_WIDTH = 512
DECAY_LORA = 64
AAA_LORA = 64
GATE_LORA = 160
GATE_LORA_PAD = 256
MEM_TOKENS = 256
MEM_HEADS = 4
MEM_HEAD_DIM = 128
MEM_WIDTH = 512
N_BRANCH = 3
D_FF = 2816
RMS_EPS = 1e-6
GN_EPS = 64e-5

RW_COLS = 3 * RW_WIDTH + 2 * DECAY_LORA + 2 * AAA_LORA + GATE_LORA
RW_COLS_PAD = 3 * RW_WIDTH + 2 * DECAY_LORA + 2 * AAA_LORA + GATE_LORA_PAD
Z_RW = 0
Z_NA = RW_COLS_PAD
Z_MEM = Z_NA + 3 * NA_WIDTH
Z_GATE = Z_MEM + MEM_WIDTH
Z_COLS = Z_GATE + N_BRANCH * D_MODEL

CHUNK = 64
PAIR = 2 * RW_HEAD_DIM
N_PAIRS = RW_HEADS // 2
INV_LEVELS = 6
NEG_BIG = -1e30
VMEM_LIMIT = 56 * 1024 * 1024


def _cparams(sem):
    return pltpu.CompilerParams(dimension_semantics=sem, vmem_limit_bytes=VMEM_LIMIT)


def _rms(x, gain):
    ms = jnp.mean(x * x, axis=-1, keepdims=True)
    return x * lax.rsqrt(ms + RMS_EPS) * gain


def _dot(a, b):
    return jnp.dot(a, b, preferred_element_type=F32)


def _dot_nt(a, b):
    return lax.dot_general(a, b, (((1,), (1,)), ((), ())), preferred_element_type=F32)


def _split_hi_lo(x):
    hi = x.astype(BF16)
    lo = (x - hi.astype(F32)).astype(BF16)
    return hi, lo


def _sel_left(sel, x):
    hi, lo = _split_hi_lo(x)
    return _dot(sel, hi) + _dot(sel, lo)


def _sel_right(x, sel):
    hi, lo = _split_hi_lo(x)
    return _dot(hi, sel) + _dot(lo, sel)


def _inproj_kernel(x_ref, g_ref, w_ref, o_ref, h_ref):
    @pl.when(pl.program_id(1) == 0)
    def _():
        h_ref[...] = _rms(x_ref[...], g_ref[...]).astype(BF16)

    o_ref[...] = _dot(h_ref[...], w_ref[...]).astype(o_ref.dtype)


def _in_proj(x2d, gain, w):
    t, n = x2d.shape[0], w.shape[1]
    tm = min(1024, t)
    tn = 1024
    return pl.pallas_call(
        _inproj_kernel,
        grid=(t // tm, n // tn),
        in_specs=[
            pl.BlockSpec((tm, D_MODEL), lambda i, j: (i, 0)),
            pl.BlockSpec((1, D_MODEL), lambda i, j: (0, 0)),
            pl.BlockSpec((D_MODEL, tn), lambda i, j: (0, j)),
        ],
        out_specs=pl.BlockSpec((tm, tn), lambda i, j: (i, j)),
        out_shape=jax.ShapeDtypeStruct((t, n), BF16),
        scratch_shapes=[pltpu.VMEM((tm, D_MODEL), BF16)],
        compiler_params=_cparams(("parallel", "arbitrary")),
        name="in_proj",
    )(x2d, gain, w)


def _na_kernel(q_ref, k_ref, v_ref, b_ref, o_ref):
    band = NA_WIN_ROWS * GRID_W
    q = q_ref[...]
    k = k_ref[...].reshape(band, NA_WIDTH)
    v = v_ref[...].reshape(band, NA_WIDTH)
    scale = NA_HEAD_DIM ** -0.5
    outs = []
    for h in range(NA_HEADS):
        sl = slice(h * NA_HEAD_DIM, (h + 1) * NA_HEAD_DIM)
        s = _dot_nt(q[:, sl], k[:, sl]) * scale + b_ref[h]
        m = jnp.max(s, axis=-1, keepdims=True)
        p = jnp.exp(s - m)
        l = jnp.sum(p, axis=-1, keepdims=True)
        outs.append(_dot(p.astype(BF16), v[:, sl]) / l)
    o_ref[...] = jnp.concatenate(outs, axis=-1).astype(o_ref.dtype)


def _na_bias_table(rpb):
    e = np.arange(NA_WIN_ROWS)[:, None]
    j = np.arange(NA_WIN_ROWS)[None, :]
    dr = j - e + (NA_WIN_ROWS - 1)
    x = np.arange(GRID_W)[:, None]
    c = np.arange(GRID_W)[None, :]
    c0 = np.clip(x - NA_WIN_COLS // 2, 0, GRID_W - NA_WIN_COLS)
    valid = (c >= c0) & (c < c0 + NA_WIN_COLS)
    dc = np.clip(c - x + (NA_WIN_COLS - 1), 0, 2 * NA_WIN_COLS - 2)
    n_dc = 2 * NA_WIN_COLS - 1
    onehot = (dc.reshape(-1)[None, :] == np.arange(n_dc)[:, None]) & valid.reshape(-1)[None, :]
    tab = jnp.einsum('hejd,dn->hejn', rpb[:, dr], jnp.asarray(onehot, F32),
                     precision=lax.Precision.HIGHEST)
    tab = tab + jnp.asarray(np.where(valid.reshape(-1), 0.0, NEG_BIG), F32)
    tab = tab.reshape(NA_HEADS, NA_WIN_ROWS, NA_WIN_ROWS, GRID_W, GRID_W)
    tab = tab.transpose(1, 0, 3, 2, 4)
    return tab.reshape(NA_WIN_ROWS, NA_HEADS, GRID_W, NA_WIN_ROWS * GRID_W).astype(F32)


def _na_attention(z, bias_tab, bsz, seq):
    rows = seq // GRID_W
    assert rows >= NA_WIN_ROWS
    z4 = z.reshape(bsz, rows, GRID_W, Z_COLS)
    half = NA_WIN_ROWS // 2

    def band_start(i):
        return jnp.clip(i - half, 0, rows - NA_WIN_ROWS)

    def kv_spec(col):
        return pl.BlockSpec(
            (pl.Squeezed(), pl.Element(NA_WIN_ROWS), pl.Element(GRID_W), pl.Element(NA_WIDTH)),
            lambda b, i: (b, band_start(i), 0, col))

    qblk = Z_NA // NA_WIDTH
    out = pl.pallas_call(
        _na_kernel,
        grid=(bsz, rows),
        in_specs=[
            pl.BlockSpec((None, None, GRID_W, NA_WIDTH), lambda b, i: (b, i, 0, qblk)),
            kv_spec(Z_NA + NA_WIDTH),
            kv_spec(Z_NA + 2 * NA_WIDTH),
            pl.BlockSpec((None, NA_HEADS, GRID_W, NA_WIN_ROWS * GRID_W),
                         lambda b, i: (i - band_start(i), 0, 0, 0)),
        ],
        out_specs=pl.BlockSpec((None, None, GRID_W, NA_WIDTH), lambda b, i: (b, i, 0, 0)),
        out_shape=jax.ShapeDtypeStruct((bsz, rows, GRID_W, NA_WIDTH), BF16),
        compiler_params=_cparams(("parallel", "arbitrary")),
        name="na_attn",
    )(z4, z4, z4, bias_tab)
    return out.reshape(bsz * seq, NA_WIDTH)


def _softplus(u):
    return jnp.maximum(u, 0.0) + jnp.log1p(jnp.exp(-jnp.abs(u)))


def _rwkv_kernel(zc_ref, zp_ref, zn_ref, cw_ref, d0_ref, d2_ref, a0_ref, a2_ref, g2_ref,
                 kk_ref, ka_ref, rk_ref, bones_ref, tri_ref, obd_ref,
                 y_ref, bonus_ref, gate_ref,
                 s_ref, rt_ref, at_ref, bt_ref, kt_ref, bh_ref, kh_ref, v_ref, tot_ref,
                 qb_ref, yl_ref, ns_ref, wm_ref,
                 *, tile):
    d = pl.program_id(0)
    i = pl.program_id(2)
    nt = pl.num_programs(2)
    ti = jnp.where(d == 0, i, nt - 1 - i)
    n_chunks = tile // CHUNK

    @pl.when(i == 0)
    def _():
        s_ref[...] = jnp.zeros_like(s_ref)

    zc = zc_ref[...].astype(F32)
    prev_row = jnp.where(ti == 0, 0.0, zp_ref[...].astype(F32)[7:8, :])
    next_row = jnp.where(ti == nt - 1, 0.0, zn_ref[...].astype(F32)[0:1, :])
    rows = lax.broadcasted_iota(jnp.int32, (tile, 1), 0)
    z_prev = jnp.where(rows == 0, prev_row, pltpu.roll(zc, 1, axis=0))
    z_next = jnp.where(rows == tile - 1, next_row, pltpu.roll(zc, tile - 1, axis=0))
    zf = z_prev * cw_ref[0:1, :] + zc * cw_ref[1:2, :] + z_next * cw_ref[2:3, :]

    o1, o2, o3 = RW_WIDTH, 2 * RW_WIDTH, 3 * RW_WIDTH
    o4 = o3 + 2 * DECAY_LORA
    o5 = o4 + 2 * AAA_LORA
    r = zf[:, 0:o1]
    k = zf[:, o1:o2]
    v = zf[:, o2:o3]
    xw = zf[:, o3:o4]
    xa = zf[:, o4:o5]
    xg = zf[:, o5:RW_COLS_PAD]

    bones = bones_ref[...]
    kk = k * kk_ref[...]
    kk = kk * lax.rsqrt(jnp.maximum(_sel_right(kk * kk, bones), 1e-24))
    gate_ref[...] = _dot(jax.nn.sigmoid(xg).astype(BF16), g2_ref[...]).astype(gate_ref.dtype)

    w_log = -_softplus(-(d0_ref[...] + _dot(jnp.tanh(xw).astype(BF16), d2_ref[...]))) - 0.5
    logw = -jnp.exp(w_log)
    a = jax.nn.sigmoid(a0_ref[...] + _dot(xa.astype(BF16), a2_ref[...]))
    k_d = k * (1.0 + (a - 1.0) * ka_ref[...])
    b = kk * a
    bonus_ref[...] = (_sel_right(r * k_d * rk_ref[...], bones) * v).astype(bonus_ref.dtype)

    g = _sel_left(tri_ref[...], logw)
    tot = _sel_left(obd_ref[...], logw)
    eng = jnp.exp(-g)
    ehat = jnp.exp(tot - g)
    rt_ref[...] = r * jnp.exp(g)
    at_ref[...] = -kk * jnp.exp(g - logw)
    bt_ref[...] = b * eng
    kt_ref[...] = k_d * eng
    bh_ref[...] = b * ehat
    kh_ref[...] = k_d * ehat
    v_ref[...] = v
    tot_ref[...] = tot

    ri = lax.broadcasted_iota(jnp.int32, (PAIR, PAIR), 0)
    ci = lax.broadcasted_iota(jnp.int32, (PAIR, PAIR), 1)
    same_head = jnp.where(jnp.right_shift(ri, 6) == jnp.right_shift(ci, 6), 1.0, 0.0).astype(F32)
    order = (ri - ci) * jnp.where(d == 0, 1, -1)
    m_strict = jnp.where(order > 0, same_head, 0.0)
    m_incl = jnp.where(order >= 0, same_head, 0.0)
    eye = jnp.where(ci == ri, 1.0, 0.0).astype(F32)
    lvl = []
    for q in range(INV_LEVELS):
        in_block = jnp.where(jnp.right_shift(ri, q + 1) == jnp.right_shift(ci, q + 1), 1.0, 0.0).astype(F32)
        lvl.append(jnp.where(jnp.right_shift(ri, q) != jnp.right_shift(ci, q), in_block, 0.0))
    lane = lax.broadcasted_iota(jnp.int32, (CHUNK, PAIR), 1)
    left = lane < RW_HEAD_DIM

    def stack(x):
        return jnp.concatenate([jnp.where(left, x, 0.0), jnp.where(left, 0.0, x)], axis=0)

    groups = [(c, p) for c in range(n_chunks) for p in range(N_PAIRS)]
    eye_b = eye.astype(BF16)

    def tile_of(ref, c, p):
        return ref[c * CHUNK:(c + 1) * CHUNK, p * PAIR:(p + 1) * PAIR]

    rt = [stack(tile_of(rt_ref, c, p)) for c, p in groups]
    at_b = [stack(tile_of(at_ref, c, p)).astype(BF16) for c, p in groups]
    vs_b = [stack(tile_of(v_ref, c, p)).astype(BF16) for c, p in groups]
    o1_ = []
    for gi, (c, p) in enumerate(groups):
        bt = tile_of(bt_ref, c, p).astype(BF16)
        kt = tile_of(kt_ref, c, p).astype(BF16)
        lhs1 = jnp.concatenate([at_b[gi], rt[gi].astype(BF16)], axis=0)
        o1_.append(_dot_nt(lhs1, jnp.concatenate([bt, bt, kt, kt], axis=0)))
    n_ab = [o[0:PAIR, 0:PAIR] * m_strict for o in o1_]
    a_ak = [(o[0:PAIR, PAIR:] * m_strict).astype(BF16) for o in o1_]
    a_r = [jnp.concatenate([o[PAIR:, 0:PAIR] * m_incl, o[PAIR:, PAIR:] * m_incl], axis=1).astype(BF16)
           for o in o1_]

    t_inv = [eye + n * lvl[0] for n in n_ab]
    for q in range(1, INV_LEVELS):
        xq = [_dot((n * lvl[q]).astype(BF16), t.astype(BF16)) for n, t in zip(n_ab, t_inv)]
        t_inv = [t + _dot(t.astype(BF16), x.astype(BF16)) for t, x in zip(t_inv, xq)]

    av = [_dot(a, v_) for a, v_ in zip(a_ak, vs_b)]
    pu = [_dot(t.astype(BF16), jnp.concatenate([a, x.astype(BF16)], axis=1))
          for t, a, x in zip(t_inv, at_b, av)]
    zero_b = jnp.zeros((PAIR, PAIR), BF16)
    for gi, (c, p) in enumerate(groups):
        bh_t = stack(tile_of(bh_ref, c, p)).T
        kh_t = stack(tile_of(kh_ref, c, p)).T
        wmat = jnp.concatenate(
            [pu[gi].astype(BF16), jnp.concatenate([zero_b, vs_b[gi]], axis=1)], axis=0)
        lhs2 = jnp.concatenate([a_r[gi], jnp.concatenate([bh_t, kh_t], axis=1).astype(BF16)], axis=0)
        o2_ = _dot(lhs2, wmat)
        q_mat = rt[gi] + o2_[0:PAIR, 0:PAIR]
        qb_ref[c, p] = jnp.concatenate([q_mat, o2_[PAIR:, 0:PAIR]], axis=0).astype(BF16)
        yl_ref[c, p] = o2_[0:PAIR, PAIR:]
        ns_ref[c, p] = o2_[PAIR:, PAIR:]
        tot_c = tile_of(tot_ref, c, p)
        tot2 = jnp.concatenate([tot_c, tot_c], axis=0)
        hi = tot2.astype(BF16)
        r1 = tot2 - hi.astype(F32)
        mid = r1.astype(BF16)
        lo = (r1 - mid.astype(F32)).astype(BF16)
        wm_ref[c, p] = jnp.exp(_dot_nt(eye_b, hi) + _dot_nt(eye_b, mid) + _dot_nt(eye_b, lo))

    def chunk_step(j, carry):
        cc = jnp.where(d == 0, j, n_chunks - 1 - j)
        off = pl.multiple_of(cc * CHUNK, CHUNK)
        st = [s_ref[p] for p in range(N_PAIRS)]
        o3_ = [_dot(qb_ref[cc, p], st[p].astype(BF16)) for p in range(N_PAIRS)]
        for p in range(N_PAIRS):
            y_bs = o3_[p][0:PAIR] + yl_ref[cc, p]
            s_ref[p] = st[p] * wm_ref[cc, p] + o3_[p][PAIR:] + ns_ref[cc, p]
            y_ref[pl.ds(off, CHUNK), p * PAIR:(p + 1) * PAIR] = y_bs[0:CHUNK] + y_bs[CHUNK:]
        return carry

    lax.fori_loop(0, n_chunks, chunk_step, 0)


def _rwkv_scan(z, lp, bsz, seq):
    tile = min(256, seq)
    nt = seq // tile
    z3 = z.reshape(bsz, seq, Z_COLS)
    t8 = tile // 8

    def tidx(d, i):
        return jnp.where(d == 0, i, nt - 1 - i)

    cidx = np.arange(tile)
    same = (cidx[:, None] // CHUNK) == (cidx[None, :] // CHUNK)
    tri = np.stack([same & (cidx[None, :] <= cidx[:, None]), same & (cidx[None, :] >= cidx[:, None])])
    tri = jnp.asarray(tri, BF16)
    obd = jnp.asarray(same, BF16)

    def full(shape):
        return pl.BlockSpec(shape, lambda d, b, i: (0,) * len(shape))

    def per_dir(shape):
        return pl.BlockSpec((None,) + shape, lambda d, b, i: (d,) + (0,) * len(shape))

    out_sds = jax.ShapeDtypeStruct((2, bsz, seq, RW_WIDTH), F32)
    out_bf = jax.ShapeDtypeStruct((2, bsz, seq, RW_WIDTH), BF16)
    out_spec = pl.BlockSpec((None, None, tile, RW_WIDTH), lambda d, b, i: (d, b, tidx(d, i), 0))
    nat = pltpu.VMEM((tile, RW_WIDTH), F32)
    return pl.pallas_call(
        functools.partial(_rwkv_kernel, tile=tile),
        grid=(2, bsz, nt),
        in_specs=[
            pl.BlockSpec((None, tile, RW_COLS_PAD), lambda d, b, i: (b, tidx(d, i), 0)),
            pl.BlockSpec((None, 8, RW_COLS_PAD),
                         lambda d, b, i: (b, jnp.maximum(tidx(d, i) * t8 - 1, 0), 0)),
            pl.BlockSpec((None, 8, RW_COLS_PAD),
                         lambda d, b, i: (b, jnp.minimum((tidx(d, i) + 1) * t8, seq // 8 - 1), 0)),
            full((3, RW_COLS_PAD)),
            per_dir((1, RW_WIDTH)), per_dir((2 * DECAY_LORA, RW_WIDTH)),
            per_dir((1, RW_WIDTH)), per_dir((2 * AAA_LORA, RW_WIDTH)),
            full((GATE_LORA_PAD, RW_WIDTH)),
            full((1, RW_WIDTH)), full((1, RW_WIDTH)), full((1, RW_WIDTH)),
            full((RW_WIDTH, RW_WIDTH)),
            per_dir((tile, tile)), full((tile, tile)),
        ],
        out_specs=[out_spec, out_spec, out_spec],
        out_shape=[out_sds, out_bf, out_bf],
        scratch_shapes=[pltpu.VMEM((N_PAIRS, PAIR, PAIR), F32)] + [nat] * 8 + [
            pltpu.VMEM((tile // CHUNK, N_PAIRS, 2 * PAIR, PAIR), BF16),
            pltpu.VMEM((tile // CHUNK, N_PAIRS, PAIR, PAIR), F32),
            pltpu.VMEM((tile // CHUNK, N_PAIRS, PAIR, PAIR), F32),
            pltpu.VMEM((tile // CHUNK, N_PAIRS, PAIR, PAIR), F32)],
        compiler_params=_cparams(("arbitrary", "arbitrary", "arbitrary")),
        name="rwkv_scan",
    )(z3, z3, z3, lp["rw_conv"], lp["rw_decay0"], lp["rw_decay2"], lp["rw_a0"], lp["rw_a2"],
      lp["rw_g2"], lp["rw_k_k"], lp["rw_k_a"], lp["rw_r_k"], lp["bones"], tri, obd)


def _memkv_kernel(m_ref, g_ref, w_ref, o_ref):
    o_ref[...] = _dot(_rms(m_ref[...], g_ref[...]).astype(BF16), w_ref[...]).astype(o_ref.dtype)


def _mem_kv(mem, gain, w):
    bsz = mem.shape[0]
    return pl.pallas_call(
        _memkv_kernel,
        grid=(bsz,),
        in_specs=[
            pl.BlockSpec((None, MEM_TOKENS, D_MODEL), lambda b: (b, 0, 0)),
            pl.BlockSpec((1, D_MODEL), lambda b: (0, 0)),
            pl.BlockSpec((D_MODEL, 2 * MEM_WIDTH), lambda b: (0, 0)),
        ],
        out_specs=pl.BlockSpec((None, MEM_TOKENS, 2 * MEM_WIDTH), lambda b: (b, 0, 0)),
        out_shape=jax.ShapeDtypeStruct((bsz, MEM_TOKENS, 2 * MEM_WIDTH), BF16),
        compiler_params=_cparams(("parallel",)),
        name="mem_kv",
    )(mem, gain, w)


def _memattn_kernel(q_ref, kv_ref, o_ref):
    scale = MEM_HEAD_DIM ** -0.5
    outs = []
    for h in range(MEM_HEADS):
        sl = slice(h * MEM_HEAD_DIM, (h + 1) * MEM_HEAD_DIM)
        vsl = slice(MEM_WIDTH + h * MEM_HEAD_DIM, MEM_WIDTH + (h + 1) * MEM_HEAD_DIM)
        s = _dot_nt(q_ref[:, sl], kv_ref[:, sl]) * scale
        m = jnp.max(s, axis=-1, keepdims=True)
        p = jnp.exp(s - m)
        l = jnp.sum(p, axis=-1, keepdims=True)
        outs.append(_dot(p.astype(BF16), kv_ref[:, vsl]) / l)
    o_ref[...] = jnp.concatenate(outs, axis=-1).astype(o_ref.dtype)


def _mem_attention(z, kv, bsz, seq):
    tm = min(1024, seq)
    z3 = z.reshape(bsz, seq, Z_COLS)
    out = pl.pallas_call(
        _memattn_kernel,
        grid=(bsz, seq // tm),
        in_specs=[
            pl.BlockSpec((None, tm, MEM_WIDTH), lambda b, i: (b, i, Z_MEM // MEM_WIDTH)),
            pl.BlockSpec((None, MEM_TOKENS, 2 * MEM_WIDTH), lambda b, i: (b, 0, 0)),
        ],
        out_specs=pl.BlockSpec((None, tm, MEM_WIDTH), lambda b, i: (b, i, 0)),
        out_shape=jax.ShapeDtypeStruct((bsz, seq, MEM_WIDTH), BF16),
        compiler_params=_cparams(("parallel", "parallel")),
        name="mem_attn",
    )(z3, kv)
    return out.reshape(bsz * seq, MEM_WIDTH)


def _merge_kernel(x_ref, ona_ref, y0_ref, y1_ref, b0_ref, b1_ref, gt_ref, omem_ref,
                  g0_ref, g1_ref, g2_ref, wb_ref, wo_ref, lw_ref, lb_ref, bones_ref, o_ref):
    bones = bones_ref[...]
    inv_n = 1.0 / RW_HEAD_DIM
    y = y0_ref[...] + y1_ref[...]
    mu = _sel_right(y, bones) * inv_n
    dl = y - mu
    var = _sel_right(dl * dl, bones) * inv_n
    yn = dl * lax.rsqrt(var + GN_EPS)
    bonus = b0_ref[...].astype(F32) + b1_ref[...].astype(F32)
    o_rw = (yn * lw_ref[...] + lb_ref[...] + bonus) * gt_ref[...].astype(F32)

    merged = jax.nn.sigmoid(g0_ref[...].astype(F32)) * _dot(ona_ref[...], wb_ref[0])
    merged = merged + jax.nn.sigmoid(g1_ref[...].astype(F32)) * _dot(o_rw.astype(BF16), wb_ref[1])
    merged = merged + jax.nn.sigmoid(g2_ref[...].astype(F32)) * _dot(omem_ref[...], wb_ref[2])
    o_ref[...] = x_ref[...] + _dot(merged.astype(BF16), wo_ref[...])


def _merge(x2d, z, o_na, y, bonus, gate, o_mem, lp):
    t = x2d.shape[0]
    tm = min(512, t)
    y2 = y.reshape(2, t, RW_WIDTH)
    b2 = bonus.reshape(2, t, RW_WIDTH)
    g2 = gate.reshape(2, t, RW_WIDTH)
    gblk = Z_GATE // D_MODEL

    def tok(width):
        return pl.BlockSpec((tm, width), lambda i: (i, 0))

    def dir_tok(dd):
        return pl.BlockSpec((None, tm, RW_WIDTH), lambda i: (dd, i, 0))

    def const(shape):
        return pl.BlockSpec(shape, lambda i: (0,) * len(shape))

    return pl.pallas_call(
        _merge_kernel,
        grid=(t // tm,),
        in_specs=[
            tok(D_MODEL), tok(NA_WIDTH), dir_tok(0), dir_tok(1), dir_tok(0), dir_tok(1), dir_tok(0),
            tok(MEM_WIDTH),
            pl.BlockSpec((tm, D_MODEL), lambda i: (i, gblk)),
            pl.BlockSpec((tm, D_MODEL), lambda i: (i, gblk + 1)),
            pl.BlockSpec((tm, D_MODEL), lambda i: (i, gblk + 2)),
            const((N_BRANCH, NA_WIDTH, D_MODEL)), const((D_MODEL, D_MODEL)),
            const((1, RW_WIDTH)), const((1, RW_WIDTH)), const((RW_WIDTH, RW_WIDTH)),
        ],
        out_specs=tok(D_MODEL),
        out_shape=jax.ShapeDtypeStruct((t, D_MODEL), F32),
        compiler_params=_cparams(("parallel",)),
        name="merge",
    )(x2d, o_na, y2, y2, b2, b2, g2, o_mem, z, z, z, lp["w_branch"], lp["w_out"],
      lp["rw_lnx_w"], lp["rw_lnx_b"], lp["bones"])


def _ffn_kernel(x_ref, xp_ref, xn_ref, g_ref, wv_ref, wg_ref, cv_ref, cg_ref, bv_ref, bg_ref, wd_ref,
                fg_ref, o_ref, h_ref, acc_ref, *, tm, tiles_per_seq, final_norm):
    i = pl.program_id(0)
    j = pl.program_id(1)
    ext = tm + 16

    @pl.when(j == 0)
    def _():
        gain = g_ref[...]
        h_ref[0:8, :] = _rms(xp_ref[...], gain).astype(BF16)
        h_ref[8:8 + tm, :] = _rms(x_ref[...], gain).astype(BF16)
        h_ref[8 + tm:ext, :] = _rms(xn_ref[...], gain).astype(BF16)
        acc_ref[...] = jnp.zeros_like(acc_ref)

    seq_pos = i % tiles_per_seq
    rows = lax.broadcasted_iota(jnp.int32, (tm, 1), 0)
    kill_prev = (rows == 0) & (seq_pos == 0)
    kill_next = (rows == tm - 1) & (seq_pos == tiles_per_seq - 1)
    hx = h_ref[...]

    def conv(u, cw, bias):
        up = jnp.where(kill_prev, 0.0, pltpu.roll(u, 1, axis=0)[8:8 + tm])
        un = jnp.where(kill_next, 0.0, pltpu.roll(u, ext - 1, axis=0)[8:8 + tm])
        return up * cw[0:1, :] + u[8:8 + tm] * cw[1:2, :] + un * cw[2:3, :] + bias

    u_val = conv(_dot(hx, wv_ref[...]), cv_ref[...], bv_ref[...])
    u_gate = conv(_dot(hx, wg_ref[...]), cg_ref[...], bg_ref[...])
    act = (u_gate * jax.nn.sigmoid(u_gate) * u_val).astype(BF16)
    acc_ref[...] += _dot(act, wd_ref[...])

    @pl.when(j == pl.num_programs(1) - 1)
    def _():
        out = x_ref[...] + acc_ref[...]
        if final_norm:
            out = _rms(out, fg_ref[...])
        o_ref[...] = out


def _ffn(x2d, lp, final_gain, seq, final_norm):
    t = x2d.shape[0]
    tm = min(1024, seq)
    tn = 256
    nj = D_FF // tn
    t8 = tm // 8
    return pl.pallas_call(
        functools.partial(_ffn_kernel, tm=tm, tiles_per_seq=seq // tm, final_norm=final_norm),
        grid=(t // tm, nj),
        in_specs=[
            pl.BlockSpec((tm, D_MODEL), lambda i, j: (i, 0)),
            pl.BlockSpec((8, D_MODEL), lambda i, j: (jnp.maximum(i * t8 - 1, 0), 0)),
            pl.BlockSpec((8, D_MODEL), lambda i, j: (jnp.minimum((i + 1) * t8, t // 8 - 1), 0)),
            pl.BlockSpec((1, D_MODEL), lambda i, j: (0, 0)),
            pl.BlockSpec((D_MODEL, tn), lambda i, j: (0, j)),
            pl.BlockSpec((D_MODEL, tn), lambda i, j: (0, nj + j)),
            pl.BlockSpec((3, tn), lambda i, j: (0, j)),
            pl.BlockSpec((3, tn), lambda i, j: (0, nj + j)),
            pl.BlockSpec((1, tn), lambda i, j: (0, j)),
            pl.BlockSpec((1, tn), lambda i, j: (0, nj + j)),
            pl.BlockSpec((tn, D_MODEL), lambda i, j: (j, 0)),
            pl.BlockSpec((1, D_MODEL), lambda i, j: (0, 0)),
        ],
        out_specs=pl.BlockSpec((tm, D_MODEL), lambda i, j: (i, 0)),
        out_shape=jax.ShapeDtypeStruct((t, D_MODEL), F32),
        scratch_shapes=[pltpu.VMEM((tm + 16, D_MODEL), BF16), pltpu.VMEM((tm, D_MODEL), F32)],
        compiler_params=_cparams(("parallel", "arbitrary")),
        name="ffn",
    )(x2d, x2d, x2d, lp["ffn_norm"], lp["w_up"], lp["w_up"], lp["ffn_conv"], lp["ffn_conv"],
      lp["ffn_conv_b"], lp["ffn_conv_b"], lp["w_down"], final_gain)


def _prep_layer(l, attn_norm, w_in, na_rpb, rw_conv, rw_decay0, rw_decay2, rw_a0, rw_a2, rw_g2, rw_k_k,
                rw_k_a, rw_r_k, rw_lnx_w, rw_lnx_b, mem_norm, w_mem_kv, w_branch, w_out, ffn_norm, w_up,
                ffn_conv, ffn_conv_b, w_down):
    c1 = 3 * NA_WIDTH
    c2 = c1 + RW_COLS
    w = w_in[l]
    pad_cols = RW_COLS_PAD - RW_COLS
    w_new = jnp.concatenate(
        [w[:, c1:c2], jnp.zeros((D_MODEL, pad_cols), w.dtype), w[:, :c1], w[:, c2:]], axis=1)

    def lora_rows(m):
        zero = jnp.zeros_like(m[0])
        return jnp.stack([jnp.concatenate([m[0], zero], 0), jnp.concatenate([zero, m[1]], 0)]).astype(BF16)

    head = np.arange(RW_WIDTH) // RW_HEAD_DIM
    return dict(
        attn_norm=attn_norm[l][None, :],
        w_in=w_new.astype(BF16),
        na_bias=_na_bias_table(na_rpb[l]),
        rw_conv=jnp.pad(rw_conv[l], ((0, 0), (0, pad_cols))),
        rw_decay0=rw_decay0[l][:, None, :],
        rw_decay2=lora_rows(rw_decay2[l]),
        rw_a0=rw_a0[l][:, None, :],
        rw_a2=lora_rows(rw_a2[l]),
        rw_g2=jnp.pad(rw_g2[l], ((0, GATE_LORA_PAD - GATE_LORA), (0, 0))).astype(BF16),
        rw_k_k=rw_k_k[l][None, :],
        rw_k_a=rw_k_a[l][None, :],
        rw_r_k=rw_r_k[l].reshape(1, RW_WIDTH),
        rw_lnx_w=rw_lnx_w[l][None, :],
        rw_lnx_b=rw_lnx_b[l][None, :],
        bones=jnp.asarray(head[:, None] == head[None, :], BF16),
        mem_norm=mem_norm[l][None, :],
        w_mem_kv=w_mem_kv[l].astype(BF16),
        w_branch=w_branch[l].astype(BF16),
        w_out=w_out[l].astype(BF16),
        ffn_norm=ffn_norm[l][None, :],
        w_up=w_up[l].astype(BF16),
        ffn_conv=ffn_conv[l],
        ffn_conv_b=ffn_conv_b[l][None, :],
        w_down=w_down[l].astype(BF16),
    )


def _layer(x2d, mem, lp, bsz, seq, final_gain, final_norm):
    z = _in_proj(x2d, lp["attn_norm"], lp["w_in"])
    o_na = _na_attention(z, lp["na_bias"], bsz, seq)
    y, bonus, gate = _rwkv_scan(z, lp, bsz, seq)
    o_mem = _mem_attention(z, _mem_kv(mem, lp["mem_norm"], lp["w_mem_kv"]), bsz, seq)
    x2d = _merge(x2d, z, o_na, y, bonus, gate, o_mem, lp)
    return _ffn(x2d, lp, final_gain, seq, final_norm)


def _trunk(x, mem, layers, final_gain):
    bsz, seq, _ = x.shape
    x2d = x.reshape(bsz * seq, D_MODEL)
    for l, lp in enumerate(layers):
        x2d = _layer(x2d, mem, lp, bsz, seq, final_gain, l == len(layers) - 1)
    return x2d.reshape(bsz, seq, D_MODEL)


def kernel(x_prompt, x_sample, mem_prompt, mem_sample, attn_norm, w_in, na_rpb, rw_conv, rw_decay0, rw_decay2, rw_a0, rw_a2, rw_g2, rw_k_k, rw_k_a, rw_r_k, rw_lnx_w, rw_lnx_b, mem_norm, w_mem_kv, w_branch, w_out, ffn_norm, w_up, ffn_conv, ffn_conv_b, w_down, final_norm):
    params = (attn_norm, w_in, na_rpb, rw_conv, rw_decay0, rw_decay2, rw_a0, rw_a2, rw_g2, rw_k_k, rw_k_a,
              rw_r_k, rw_lnx_w, rw_lnx_b, mem_norm, w_mem_kv, w_branch, w_out, ffn_norm, w_up, ffn_conv,
              ffn_conv_b, w_down)
    layers = [_prep_layer(l, *params) for l in range(attn_norm.shape[0])]
    final_gain = final_norm[None, :]
    y_prompt = _trunk(x_prompt, mem_prompt, layers, final_gain)
    y_sample = _trunk(x_sample, mem_sample, layers, final_gain)
    return (y_prompt, y_sample)
```

```python
import functools

import jax
import jax.numpy as jnp
import numpy as np
from jax import lax
from jax.experimental import pallas as pl
from jax.experimental.pallas import tpu as pltpu

F32 = jnp.float32
BF16 = jnp.bfloat16

D_MODEL = 1024
GRID_W = 64
NA_HEADS = 8
NA_HEAD_DIM = 64
NA_WIDTH = 512
NA_WIN_ROWS = 8
NA_WIN_COLS = 16
RW_HEADS = 8
RW_HEAD_DIM = 64
RW_WIDTH = 512
DECAY_LORA = 64
AAA_LORA = 64
GATE_LORA = 160
GATE_LORA_PAD = 256
MEM_TOKENS = 256
MEM_HEADS = 4
MEM_HEAD_DIM = 128
MEM_WIDTH = 512
N_BRANCH = 3
D_FF = 2816
RMS_EPS = 1e-6
GN_EPS = 64e-5

RW_COLS = 3 * RW_WIDTH + 2 * DECAY_LORA + 2 * AAA_LORA + GATE_LORA
RW_COLS_PAD = 3 * RW_WIDTH + 2 * DECAY_LORA + 2 * AAA_LORA + GATE_LORA_PAD
Z_RW = 0
Z_NA = RW_COLS_PAD
Z_MEM = Z_NA + 3 * NA_WIDTH
Z_GATE = Z_MEM + MEM_WIDTH
Z_COLS = Z_GATE + N_BRANCH * D_MODEL

CHUNK = 64
PAIR = 2 * RW_HEAD_DIM
N_PAIRS = RW_HEADS // 2
INV_LEVELS = 6
NEG_BIG = -1e30
VMEM_LIMIT = 56 * 1024 * 1024


def _cparams(sem):
    return pltpu.CompilerParams(dimension_semantics=sem, vmem_limit_bytes=VMEM_LIMIT)


def _rms(x, gain):
    ms = jnp.mean(x * x, axis=-1, keepdims=True)
    return x * lax.rsqrt(ms + RMS_EPS) * gain


def _dot(a, b):
    return jnp.dot(a, b, preferred_element_type=F32)


def _dot_nt(a, b):
    return lax.dot_general(a, b, (((1,), (1,)), ((), ())), preferred_element_type=F32)


def _split_hi_lo(x):
    hi = x.astype(BF16)
    lo = (x - hi.astype(F32)).astype(BF16)
    return hi, lo


def _sel_left(sel, x):
    hi, lo = _split_hi_lo(x)
    return _dot(sel, hi) + _dot(sel, lo)


def _sel_right(x, sel):
    hi, lo = _split_hi_lo(x)
    return _dot(hi, sel) + _dot(lo, sel)


def _inproj_kernel(x_ref, g_ref, w_ref, o_ref, h_ref):
    @pl.when(pl.program_id(1) == 0)
    def _():
        h_ref[...] = _rms(x_ref[...], g_ref[...]).astype(BF16)

    o_ref[...] = _dot(h_ref[...], w_ref[...]).astype(o_ref.dtype)


def _in_proj(x2d, gain, w):
    t, n = x2d.shape[0], w.shape[1]
    tm = min(1024, t)
    tn = 1024
    return pl.pallas_call(
        _inproj_kernel,
        grid=(t // tm, n // tn),
        in_specs=[
            pl.BlockSpec((tm, D_MODEL), lambda i, j: (i, 0)),
            pl.BlockSpec((1, D_MODEL), lambda i, j: (0, 0)),
            pl.BlockSpec((D_MODEL, tn), lambda i, j: (0, j)),
        ],
        out_specs=pl.BlockSpec((tm, tn), lambda i, j: (i, j)),
        out_shape=jax.ShapeDtypeStruct((t, n), BF16),
        scratch_shapes=[pltpu.VMEM((tm, D_MODEL), BF16)],
        compiler_params=_cparams(("parallel", "arbitrary")),
        name="in_proj",
    )(x2d, gain, w)


NA_ROW_BLOCK = 8
NA_BAND_ROWS = NA_ROW_BLOCK + NA_WIN_ROWS - 1


def _na_kernel(q_ref, k_ref, v_ref, t2_ref, o_ref, *, rows):
    half = NA_WIN_ROWS // 2
    band = NA_WIN_ROWS * GRID_W
    i0 = pl.program_id(1) * NA_ROW_BLOCK
    block_start = jnp.clip(i0 - half, 0, rows - NA_BAND_ROWS)
    lane = lax.broadcasted_iota(jnp.int32, (1, PAIR), 1)
    left = lane < NA_HEAD_DIM
    scale = NA_HEAD_DIM ** -0.5

    def row_step(m, carry):
        i = i0 + m
        r0 = jnp.clip(i - half, 0, rows - NA_WIN_ROWS)
        off = r0 - block_start
        dr0 = (NA_WIN_ROWS - 1) - (i - r0)
        q = q_ref[m] * scale
        kb = k_ref[pl.ds(off, NA_WIN_ROWS)].reshape(band, NA_WIDTH)
        vb = v_ref[pl.ds(off, NA_WIN_ROWS)].reshape(band, NA_WIDTH)
        s = []
        for h in range(NA_HEADS):
            ps = slice((h // 2) * PAIR, (h // 2 + 1) * PAIR)
            qm = jnp.where(left if h % 2 == 0 else ~left, q[:, ps], 0.0).astype(BF16)
            bias = jnp.concatenate([t2_ref[h, dr0 + 2 * jj] for jj in range(NA_WIN_ROWS // 2)], axis=1)
            s.append(_dot_nt(qm, kb[:, ps]) + bias)
        mx = [jnp.max(x, axis=-1, keepdims=True) for x in s]
        p = [jnp.exp(x - m_) for x, m_ in zip(s, mx)]
        inv = [1.0 / jnp.sum(x, axis=-1, keepdims=True) for x in p]
        outs = []
        for pp in range(NA_HEADS // 2):
            vp = vb[:, pp * PAIR:(pp + 1) * PAIR]
            o_l = _dot(p[2 * pp].astype(BF16), jnp.where(left, vp, 0.0).astype(BF16))
            o_r = _dot(p[2 * pp + 1].astype(BF16), jnp.where(left, 0.0, vp).astype(BF16))
            outs.append(o_l * inv[2 * pp] + o_r * inv[2 * pp + 1])
        o_ref[m] = jnp.concatenate(outs, axis=-1).astype(o_ref.dtype)
        return carry

    lax.fori_loop(0, NA_ROW_BLOCK, row_step, 0)


def _na_bias_table(rpb):
    x = np.arange(GRID_W)[:, None]
    c = np.arange(GRID_W)[None, :]
    c0 = np.clip(x - NA_WIN_COLS // 2, 0, GRID_W - NA_WIN_COLS)
    valid = ((c >= c0) & (c < c0 + NA_WIN_COLS)).reshape(-1)
    dc = np.clip(c - x + (NA_WIN_COLS - 1), 0, 2 * NA_WIN_COLS - 2).reshape(-1)
    onehot = (dc[None, :] == np.arange(2 * NA_WIN_COLS - 1)[:, None]) & valid[None, :]
    tab = jnp.einsum('hrd,dn->hrn', rpb, jnp.asarray(onehot, F32), precision=lax.Precision.HIGHEST)
    tab = tab + jnp.asarray(np.where(valid, 0.0, NEG_BIG), F32)
    tab = tab.reshape(NA_HEADS, 2 * NA_WIN_ROWS - 1, GRID_W, GRID_W)
    return jnp.concatenate([tab[:, :-1], tab[:, 1:]], axis=-1)


def _na_attention(z, bias_tab, bsz, seq):
    rows = seq // GRID_W
    assert rows >= NA_BAND_ROWS and rows % NA_ROW_BLOCK == 0
    z4 = z.reshape(bsz, rows, GRID_W, Z_COLS)
    half = NA_WIN_ROWS // 2

    def kv_spec(col):
        return pl.BlockSpec(
            (pl.Squeezed(), pl.Element(NA_BAND_ROWS), pl.Element(GRID_W), pl.Element(NA_WIDTH)),
            lambda b, i: (b, jnp.clip(i * NA_ROW_BLOCK - half, 0, rows - NA_BAND_ROWS), 0, col))

    qblk = Z_NA // NA_WIDTH
    out = pl.pallas_call(
        functools.partial(_na_kernel, rows=rows),
        grid=(bsz, rows // NA_ROW_BLOCK),
        in_specs=[
            pl.BlockSpec((None, NA_ROW_BLOCK, GRID_W, NA_WIDTH), lambda b, i: (b, i, 0, qblk)),
            kv_spec(Z_NA + NA_WIDTH),
            kv_spec(Z_NA + 2 * NA_WIDTH),
            pl.BlockSpec(bias_tab.shape, lambda b, i: (0, 0, 0, 0)),
        ],
        out_specs=pl.BlockSpec((None, NA_ROW_BLOCK, GRID_W, NA_WIDTH), lambda b, i: (b, i, 0, 0)),
        out_shape=jax.ShapeDtypeStruct((bsz, rows, GRID_W, NA_WIDTH), BF16),
        compiler_params=_cparams(("parallel", "arbitrary")),
        name="na_attn",
    )(z4, z4, z4, bias_tab)
    return out.reshape(bsz * seq, NA_WIDTH)


def _softplus(u):
    return jnp.maximum(u, 0.0) + jnp.log1p(jnp.exp(-jnp.abs(u)))


def _rwkv_kernel(zc_ref, zp_ref, zn_ref, cw_ref, d0_ref, d2_ref, a0_ref, a2_ref, g2_ref,
                 kk_ref, ka_ref, rk_ref, bones_ref, tri_ref, obd_ref,
                 y_ref, bonus_ref, gate_ref,
                 s_ref, rt_ref, at_ref, bt_ref, kt_ref, bh_ref, kh_ref, v_ref, tot_ref,
                 qb_ref, yl_ref, ns_ref, wm_ref,
                 *, tile):
    d = pl.program_id(0)
    i = pl.program_id(2)
    nt = pl.num_programs(2)
    ti = jnp.where(d == 0, i, nt - 1 - i)
    n_chunks = tile // CHUNK

    @pl.when(i == 0)
    def _():
        s_ref[...] = jnp.zeros_like(s_ref)

    zc = zc_ref[...].astype(F32)
    prev_row = jnp.where(ti == 0, 0.0, zp_ref[...].astype(F32)[7:8, :])
    next_row = jnp.where(ti == nt - 1, 0.0, zn_ref[...].astype(F32)[0:1, :])
    rows = lax.broadcasted_iota(jnp.int32, (tile, 1), 0)
    z_prev = jnp.where(rows == 0, prev_row, pltpu.roll(zc, 1, axis=0))
    z_next = jnp.where(rows == tile - 1, next_row, pltpu.roll(zc, tile - 1, axis=0))
    zf = z_prev * cw_ref[0:1, :] + zc * cw_ref[1:2, :] + z_next * cw_ref[2:3, :]

    o1, o2, o3 = RW_WIDTH, 2 * RW_WIDTH, 3 * RW_WIDTH
    o4 = o3 + 2 * DECAY_LORA
    o5 = o4 + 2 * AAA_LORA
    r = zf[:, 0:o1]
    k = zf[:, o1:o2]
    v = zf[:, o2:o3]
    xw = zf[:, o3:o4]
    xa = zf[:, o4:o5]
    xg = zf[:, o5:RW_COLS_PAD]

    bones = bones_ref[...]
    kk = k * kk_ref[...]
    kk = kk * lax.rsqrt(jnp.maximum(_sel_right(kk * kk, bones), 1e-24))
    gate_ref[...] = _dot(jax.nn.sigmoid(xg).astype(BF16), g2_ref[...]).astype(gate_ref.dtype)

    w_log = -_softplus(-(d0_ref[...] + _dot(jnp.tanh(xw).astype(BF16), d2_ref[...]))) - 0.5
    logw = -jnp.exp(w_log)
    a = jax.nn.sigmoid(a0_ref[...] + _dot(xa.astype(BF16), a2_ref[...]))
    k_d = k * (1.0 + (a - 1.0) * ka_ref[...])
    b = kk * a
    bonus_ref[...] = (_sel_right(r * k_d * rk_ref[...], bones) * v).astype(bonus_ref.dtype)

    g = _sel_left(tri_ref[...], logw)
    tot = _sel_left(obd_ref[...], logw)
    eng = jnp.exp(-g)
    ehat = jnp.exp(tot - g)
    rt_ref[...] = r * jnp.exp(g)
    at_ref[...] = -kk * jnp.exp(g - logw)
    bt_ref[...] = b * eng
    kt_ref[...] = k_d * eng
    bh_ref[...] = b * ehat
    kh_ref[...] = k_d * ehat
    v_ref[...] = v
    tot_ref[...] = tot

    ri = lax.broadcasted_iota(jnp.int32, (PAIR, PAIR), 0)
    ci = lax.broadcasted_iota(jnp.int32, (PAIR, PAIR), 1)
    same_head = jnp.where(jnp.right_shift(ri, 6) == jnp.right_shift(ci, 6), 1.0, 0.0).astype(F32)
    order = (ri - ci) * jnp.where(d == 0, 1, -1)
    m_strict = jnp.where(order > 0, same_head, 0.0)
    m_incl = jnp.where(order >= 0, same_head, 0.0)
    eye = jnp.where(ci == ri, 1.0, 0.0).astype(F32)
    lvl = []
    for q in range(INV_LEVELS):
        in_block = jnp.where(jnp.right_shift(ri, q + 1) == jnp.right_shift(ci, q + 1), 1.0, 0.0).astype(F32)
        lvl.append(jnp.where(jnp.right_shift(ri, q) != jnp.right_shift(ci, q), in_block, 0.0))
    lane = lax.broadcasted_iota(jnp.int32, (CHUNK, PAIR), 1)
    left = lane < RW_HEAD_DIM

    def stack(x):
        return jnp.concatenate([jnp.where(left, x, 0.0), jnp.where(left, 0.0, x)], axis=0)

    groups = [(c, p) for c in range(n_chunks) for p in range(N_PAIRS)]
    eye_b = eye.astype(BF16)

    def tile_of(ref, c, p):
        return ref[c * CHUNK:(c + 1) * CHUNK, p * PAIR:(p + 1) * PAIR]

    rt = [stack(tile_of(rt_ref, c, p)) for c, p in groups]
    at_b = [stack(tile_of(at_ref, c, p)).astype(BF16) for c, p in groups]
    vs_b = [stack(tile_of(v_ref, c, p)).astype(BF16) for c, p in groups]
    o1_ = []
    for gi, (c, p) in enumerate(groups):
        bt = tile_of(bt_ref, c, p).astype(BF16)
        kt = tile_of(kt_ref, c, p).astype(BF16)
        lhs1 = jnp.concatenate([at_b[gi], rt[gi].astype(BF16)], axis=0)
        o1_.append(_dot_nt(lhs1, jnp.concatenate([bt, bt, kt, kt], axis=0)))
    n_ab = [o[0:PAIR, 0:PAIR] * m_strict for o in o1_]
    a_ak = [(o[0:PAIR, PAIR:] * m_strict).astype(BF16) for o in o1_]
    a_r = [jnp.concatenate([o[PAIR:, 0:PAIR] * m_incl, o[PAIR:, PAIR:] * m_incl], axis=1).astype(BF16)
           for o in o1_]

    t_inv = [eye + n * lvl[0] for n in n_ab]
    for q in range(1, INV_LEVELS):
        xq = [_dot((n * lvl[q]).astype(BF16), t.astype(BF16)) for n, t in zip(n_ab, t_inv)]
        t_inv = [t + _dot(t.astype(BF16), x.astype(BF16)) for t, x in zip(t_inv, xq)]

    av = [_dot(a, v_) for a, v_ in zip(a_ak, vs_b)]
    pu = [_dot(t.astype(BF16), jnp.concatenate([a, x.astype(BF16)], axis=1))
          for t, a, x in zip(t_inv, at_b, av)]
    zero_b = jnp.zeros((PAIR, PAIR), BF16)
    for gi, (c, p) in enumerate(groups):
        bh_t = stack(tile_of(bh_ref, c, p)).T
        kh_t = stack(tile_of(kh_ref, c, p)).T
        wmat = jnp.concatenate(
            [pu[gi].astype(BF16), jnp.concatenate([zero_b, vs_b[gi]], axis=1)], axis=0)
        lhs2 = jnp.concatenate([a_r[gi], jnp.concatenate([bh_t, kh_t], axis=1).astype(BF16)], axis=0)
        o2_ = _dot(lhs2, wmat)
        q_mat = rt[gi] + o2_[0:PAIR, 0:PAIR]
        qb_ref[c, p] = jnp.concatenate([q_mat, o2_[PAIR:, 0:PAIR]], axis=0).astype(BF16)
        yl_ref[c, p] = o2_[0:PAIR, PAIR:]
        ns_ref[c, p] = o2_[PAIR:, PAIR:]
        tot_c = tile_of(tot_ref, c, p)
        tot2 = jnp.concatenate([tot_c, tot_c], axis=0)
        hi = tot2.astype(BF16)
        r1 = tot2 - hi.astype(F32)
        mid = r1.astype(BF16)
        lo = (r1 - mid.astype(F32)).astype(BF16)
        wm_ref[c, p] = jnp.exp(_dot_nt(eye_b, hi) + _dot_nt(eye_b, mid) + _dot_nt(eye_b, lo))

    def chunk_step(j, carry):
        cc = jnp.where(d == 0, j, n_chunks - 1 - j)
        off = pl.multiple_of(cc * CHUNK, CHUNK)
        st = [s_ref[p] for p in range(N_PAIRS)]
        o3_ = [_dot(qb_ref[cc, p], st[p].astype(BF16)) for p in range(N_PAIRS)]
        for p in range(N_PAIRS):
            y_bs = o3_[p][0:PAIR] + yl_ref[cc, p]
            s_ref[p] = st[p] * wm_ref[cc, p] + o3_[p][PAIR:] + ns_ref[cc, p]
            y_ref[pl.ds(off, CHUNK), p * PAIR:(p + 1) * PAIR] = y_bs[0:CHUNK] + y_bs[CHUNK:]
        return carry

    lax.fori_loop(0, n_chunks, chunk_step, 0)


def _rwkv_scan(z, lp, bsz, seq):
    tile = min(256, seq)
    nt = seq // tile
    z3 = z.reshape(bsz, seq, Z_COLS)
    t8 = tile // 8

    def tidx(d, i):
        return jnp.where(d == 0, i, nt - 1 - i)

    cidx = np.arange(tile)
    same = (cidx[:, None] // CHUNK) == (cidx[None, :] // CHUNK)
    tri = np.stack([same & (cidx[None, :] <= cidx[:, None]), same & (cidx[None, :] >= cidx[:, None])])
    tri = jnp.asarray(tri, BF16)
    obd = jnp.asarray(same, BF16)

    def full(shape):
        return pl.BlockSpec(shape, lambda d, b, i: (0,) * len(shape))

    def per_dir(shape):
        return pl.BlockSpec((None,) + shape, lambda d, b, i: (d,) + (0,) * len(shape))

    out_sds = jax.ShapeDtypeStruct((2, bsz, seq, RW_WIDTH), F32)
    out_bf = jax.ShapeDtypeStruct((2, bsz, seq, RW_WIDTH), BF16)
    out_spec = pl.BlockSpec((None, None, tile, RW_WIDTH), lambda d, b, i: (d, b, tidx(d, i), 0))
    nat = pltpu.VMEM((tile, RW_WIDTH), F32)
    return pl.pallas_call(
        functools.partial(_rwkv_kernel, tile=tile),
        grid=(2, bsz, nt),
        in_specs=[
            pl.BlockSpec((None, tile, RW_COLS_PAD), lambda d, b, i: (b, tidx(d, i), 0)),
            pl.BlockSpec((None, 8, RW_COLS_PAD),
                         lambda d, b, i: (b, jnp.maximum(tidx(d, i) * t8 - 1, 0), 0)),
            pl.BlockSpec((None, 8, RW_COLS_PAD),
                         lambda d, b, i: (b, jnp.minimum((tidx(d, i) + 1) * t8, seq // 8 - 1), 0)),
            full((3, RW_COLS_PAD)),
            per_dir((1, RW_WIDTH)), per_dir((2 * DECAY_LORA, RW_WIDTH)),
            per_dir((1, RW_WIDTH)), per_dir((2 * AAA_LORA, RW_WIDTH)),
            full((GATE_LORA_PAD, RW_WIDTH)),
            full((1, RW_WIDTH)), full((1, RW_WIDTH)), full((1, RW_WIDTH)),
            full((RW_WIDTH, RW_WIDTH)),
            per_dir((tile, tile)), full((tile, tile)),
        ],
        out_specs=[out_spec, out_spec, out_spec],
        out_shape=[out_sds, out_bf, out_bf],
        scratch_shapes=[pltpu.VMEM((N_PAIRS, PAIR, PAIR), F32)] + [nat] * 8 + [
            pltpu.VMEM((tile // CHUNK, N_PAIRS, 2 * PAIR, PAIR), BF16),
            pltpu.VMEM((tile // CHUNK, N_PAIRS, PAIR, PAIR), F32),
            pltpu.VMEM((tile // CHUNK, N_PAIRS, PAIR, PAIR), F32),
            pltpu.VMEM((tile // CHUNK, N_PAIRS, PAIR, PAIR), F32)],
        compiler_params=_cparams(("arbitrary", "arbitrary", "arbitrary")),
        name="rwkv_scan",
    )(z3, z3, z3, lp["rw_conv"], lp["rw_decay0"], lp["rw_decay2"], lp["rw_a0"], lp["rw_a2"],
      lp["rw_g2"], lp["rw_k_k"], lp["rw_k_a"], lp["rw_r_k"], lp["bones"], tri, obd)


def _memkv_kernel(m_ref, g_ref, w_ref, o_ref):
    o_ref[...] = _dot(_rms(m_ref[...], g_ref[...]).astype(BF16), w_ref[...]).astype(o_ref.dtype)


def _mem_kv(mem, gain, w):
    bsz = mem.shape[0]
    return pl.pallas_call(
        _memkv_kernel,
        grid=(bsz,),
        in_specs=[
            pl.BlockSpec((None, MEM_TOKENS, D_MODEL), lambda b: (b, 0, 0)),
            pl.BlockSpec((1, D_MODEL), lambda b: (0, 0)),
            pl.BlockSpec((D_MODEL, 2 * MEM_WIDTH), lambda b: (0, 0)),
        ],
        out_specs=pl.BlockSpec((None, MEM_TOKENS, 2 * MEM_WIDTH), lambda b: (b, 0, 0)),
        out_shape=jax.ShapeDtypeStruct((bsz, MEM_TOKENS, 2 * MEM_WIDTH), BF16),
        compiler_params=_cparams(("parallel",)),
        name="mem_kv",
    )(mem, gain, w)


def _memattn_kernel(q_ref, kv_ref, o_ref):
    scale = MEM_HEAD_DIM ** -0.5
    outs = []
    for h in range(MEM_HEADS):
        sl = slice(h * MEM_HEAD_DIM, (h + 1) * MEM_HEAD_DIM)
        vsl = slice(MEM_WIDTH + h * MEM_HEAD_DIM, MEM_WIDTH + (h + 1) * MEM_HEAD_DIM)
        s = _dot_nt(q_ref[:, sl], kv_ref[:, sl]) * scale
        m = jnp.max(s, axis=-1, keepdims=True)
        p = jnp.exp(s - m)
        l = jnp.sum(p, axis=-1, keepdims=True)
        outs.append(_dot(p.astype(BF16), kv_ref[:, vsl]) / l)
    o_ref[...] = jnp.concatenate(outs, axis=-1).astype(o_ref.dtype)


def _mem_attention(z, kv, bsz, seq):
    tm = min(1024, seq)
    z3 = z.reshape(bsz, seq, Z_COLS)
    out = pl.pallas_call(
        _memattn_kernel,
        grid=(bsz, seq // tm),
        in_specs=[
            pl.BlockSpec((None, tm, MEM_WIDTH), lambda b, i: (b, i, Z_MEM // MEM_WIDTH)),
            pl.BlockSpec((None, MEM_TOKENS, 2 * MEM_WIDTH), lambda b, i: (b, 0, 0)),
        ],
        out_specs=pl.BlockSpec((None, tm, MEM_WIDTH), lambda b, i: (b, i, 0)),
        out_shape=jax.ShapeDtypeStruct((bsz, seq, MEM_WIDTH), BF16),
        compiler_params=_cparams(("parallel", "parallel")),
        name="mem_attn",
    )(z3, kv)
    return out.reshape(bsz * seq, MEM_WIDTH)


def _merge_kernel(x_ref, ona_ref, y0_ref, y1_ref, b0_ref, b1_ref, gt_ref, omem_ref,
                  g0_ref, g1_ref, g2_ref, wb_ref, wo_ref, lw_ref, lb_ref, bones_ref, o_ref):
    bones = bones_ref[...]
    inv_n = 1.0 / RW_HEAD_DIM
    y = y0_ref[...] + y1_ref[...]
    mu = _sel_right(y, bones) * inv_n
    dl = y - mu
    var = _sel_right(dl * dl, bones) * inv_n
    yn = dl * lax.rsqrt(var + GN_EPS)
    bonus = b0_ref[...].astype(F32) + b1_ref[...].astype(F32)
    o_rw = (yn * lw_ref[...] + lb_ref[...] + bonus) * gt_ref[...].astype(F32)

    merged = jax.nn.sigmoid(g0_ref[...].astype(F32)) * _dot(ona_ref[...], wb_ref[0])
    merged = merged + jax.nn.sigmoid(g1_ref[...].astype(F32)) * _dot(o_rw.astype(BF16), wb_ref[1])
    merged = merged + jax.nn.sigmoid(g2_ref[...].astype(F32)) * _dot(omem_ref[...], wb_ref[2])
    o_ref[...] = x_ref[...] + _dot(merged.astype(BF16), wo_ref[...])


def _merge(x2d, z, o_na, y, bonus, gate, o_mem, lp):
    t = x2d.shape[0]
    tm = min(512, t)
    y2 = y.reshape(2, t, RW_WIDTH)
    b2 = bonus.reshape(2, t, RW_WIDTH)
    g2 = gate.reshape(2, t, RW_WIDTH)
    gblk = Z_GATE // D_MODEL

    def tok(width):
        return pl.BlockSpec((tm, width), lambda i: (i, 0))

    def dir_tok(dd):
        return pl.BlockSpec((None, tm, RW_WIDTH), lambda i: (dd, i, 0))

    def const(shape):
        return pl.BlockSpec(shape, lambda i: (0,) * len(shape))

    return pl.pallas_call(
        _merge_kernel,
        grid=(t // tm,),
        in_specs=[
            tok(D_MODEL), tok(NA_WIDTH), dir_tok(0), dir_tok(1), dir_tok(0), dir_tok(1), dir_tok(0),
            tok(MEM_WIDTH),
            pl.BlockSpec((tm, D_MODEL), lambda i: (i, gblk)),
            pl.BlockSpec((tm, D_MODEL), lambda i: (i, gblk + 1)),
            pl.BlockSpec((tm, D_MODEL), lambda i: (i, gblk + 2)),
            const((N_BRANCH, NA_WIDTH, D_MODEL)), const((D_MODEL, D_MODEL)),
            const((1, RW_WIDTH)), const((1, RW_WIDTH)), const((RW_WIDTH, RW_WIDTH)),
        ],
        out_specs=tok(D_MODEL),
        out_shape=jax.ShapeDtypeStruct((t, D_MODEL), F32),
        compiler_params=_cparams(("parallel",)),
        name="merge",
    )(x2d, o_na, y2, y2, b2, b2, g2, o_mem, z, z, z, lp["w_branch"], lp["w_out"],
      lp["rw_lnx_w"], lp["rw_lnx_b"], lp["bones"])


def _ffn_kernel(x_ref, xp_ref, xn_ref, g_ref, wv_ref, wg_ref, cv_ref, cg_ref, bv_ref, bg_ref, wd_ref,
                fg_ref, o_ref, h_ref, acc_ref, *, tm, tiles_per_seq, final_norm):
    i = pl.program_id(0)
    j = pl.program_id(1)
    ext = tm + 16

    @pl.when(j == 0)
    def _():
        gain = g_ref[...]
        h_ref[0:8, :] = _rms(xp_ref[...], gain).astype(BF16)
        h_ref[8:8 + tm, :] = _rms(x_ref[...], gain).astype(BF16)
        h_ref[8 + tm:ext, :] = _rms(xn_ref[...], gain).astype(BF16)
        acc_ref[...] = jnp.zeros_like(acc_ref)

    seq_pos = i % tiles_per_seq
    rows = lax.broadcasted_iota(jnp.int32, (tm, 1), 0)
    kill_prev = (rows == 0) & (seq_pos == 0)
    kill_next = (rows == tm - 1) & (seq_pos == tiles_per_seq - 1)
    hx = h_ref[...]

    def conv(u, cw, bias):
        up = jnp.where(kill_prev, 0.0, pltpu.roll(u, 1, axis=0)[8:8 + tm])
        un = jnp.where(kill_next, 0.0, pltpu.roll(u, ext - 1, axis=0)[8:8 + tm])
        return up * cw[0:1, :] + u[8:8 + tm] * cw[1:2, :] + un * cw[2:3, :] + bias

    u_val = conv(_dot(hx, wv_ref[...]), cv_ref[...], bv_ref[...])
    u_gate = conv(_dot(hx, wg_ref[...]), cg_ref[...], bg_ref[...])
    act = (u_gate * jax.nn.sigmoid(u_gate) * u_val).astype(BF16)
    acc_ref[...] += _dot(act, wd_ref[...])

    @pl.when(j == pl.num_programs(1) - 1)
    def _():
        out = x_ref[...] + acc_ref[...]
        if final_norm:
            out = _rms(out, fg_ref[...])
        o_ref[...] = out


def _ffn(x2d, lp, final_gain, seq, final_norm):
    t = x2d.shape[0]
    tm = min(1024, seq)
    tn = 256
    nj = D_FF // tn
    t8 = tm // 8
    return pl.pallas_call(
        functools.partial(_ffn_kernel, tm=tm, tiles_per_seq=seq // tm, final_norm=final_norm),
        grid=(t // tm, nj),
        in_specs=[
            pl.BlockSpec((tm, D_MODEL), lambda i, j: (i, 0)),
            pl.BlockSpec((8, D_MODEL), lambda i, j: (jnp.maximum(i * t8 - 1, 0), 0)),
            pl.BlockSpec((8, D_MODEL), lambda i, j: (jnp.minimum((i + 1) * t8, t // 8 - 1), 0)),
            pl.BlockSpec((1, D_MODEL), lambda i, j: (0, 0)),
            pl.BlockSpec((D_MODEL, tn), lambda i, j: (0, j)),
            pl.BlockSpec((D_MODEL, tn), lambda i, j: (0, nj + j)),
            pl.BlockSpec((3, tn), lambda i, j: (0, j)),
            pl.BlockSpec((3, tn), lambda i, j: (0, nj + j)),
            pl.BlockSpec((1, tn), lambda i, j: (0, j)),
            pl.BlockSpec((1, tn), lambda i, j: (0, nj + j)),
            pl.BlockSpec((tn, D_MODEL), lambda i, j: (j, 0)),
            pl.BlockSpec((1, D_MODEL), lambda i, j: (0, 0)),
        ],
        out_specs=pl.BlockSpec((tm, D_MODEL), lambda i, j: (i, 0)),
        out_shape=jax.ShapeDtypeStruct((t, D_MODEL), F32),
        scratch_shapes=[pltpu.VMEM((tm + 16, D_MODEL), BF16), pltpu.VMEM((tm, D_MODEL), F32)],
        compiler_params=_cparams(("parallel", "arbitrary")),
        name="ffn",
    )(x2d, x2d, x2d, lp["ffn_norm"], lp["w_up"], lp["w_up"], lp["ffn_conv"], lp["ffn_conv"],
      lp["ffn_conv_b"], lp["ffn_conv_b"], lp["w_down"], final_gain)


def _prep_layer(l, attn_norm, w_in, na_rpb, rw_conv, rw_decay0, rw_decay2, rw_a0, rw_a2, rw_g2, rw_k_k,
                rw_k_a, rw_r_k, rw_lnx_w, rw_lnx_b, mem_norm, w_mem_kv, w_branch, w_out, ffn_norm, w_up,
                ffn_conv, ffn_conv_b, w_down):
    c1 = 3 * NA_WIDTH
    c2 = c1 + RW_COLS
    w = w_in[l]
    pad_cols = RW_COLS_PAD - RW_COLS
    w_new = jnp.concatenate(
        [w[:, c1:c2], jnp.zeros((D_MODEL, pad_cols), w.dtype), w[:, :c1], w[:, c2:]], axis=1)

    def lora_rows(m):
        zero = jnp.zeros_like(m[0])
        return jnp.stack([jnp.concatenate([m[0], zero], 0), jnp.concatenate([zero, m[1]], 0)]).astype(BF16)

    head = np.arange(RW_WIDTH) // RW_HEAD_DIM
    return dict(
        attn_norm=attn_norm[l][None, :],
        w_in=w_new.astype(BF16),
        na_bias=_na_bias_table(na_rpb[l]),
        rw_conv=jnp.pad(rw_conv[l], ((0, 0), (0, pad_cols))),
        rw_decay0=rw_decay0[l][:, None, :],
        rw_decay2=lora_rows(rw_decay2[l]),
        rw_a0=rw_a0[l][:, None, :],
        rw_a2=lora_rows(rw_a2[l]),
        rw_g2=jnp.pad(rw_g2[l], ((0, GATE_LORA_PAD - GATE_LORA), (0, 0))).astype(BF16),
        rw_k_k=rw_k_k[l][None, :],
        rw_k_a=rw_k_a[l][None, :],
        rw_r_k=rw_r_k[l].reshape(1, RW_WIDTH),
        rw_lnx_w=rw_lnx_w[l][None, :],
        rw_lnx_b=rw_lnx_b[l][None, :],
        bones=jnp.asarray(head[:, None] == head[None, :], BF16),
        mem_norm=mem_norm[l][None, :],
        w_mem_kv=w_mem_kv[l].astype(BF16),
        w_branch=w_branch[l].astype(BF16),
        w_out=w_out[l].astype(BF16),
        ffn_norm=ffn_norm[l][None, :],
        w_up=w_up[l].astype(BF16),
        ffn_conv=ffn_conv[l],
        ffn_conv_b=ffn_conv_b[l][None, :],
        w_down=w_down[l].astype(BF16),
    )


def _layer(x2d, mem, lp, bsz, seq, final_gain, final_norm):
    z = _in_proj(x2d, lp["attn_norm"], lp["w_in"])
    o_na = _na_attention(z, lp["na_bias"], bsz, seq)
    y, bonus, gate = _rwkv_scan(z, lp, bsz, seq)
    o_mem = _mem_attention(z, _mem_kv(mem, lp["mem_norm"], lp["w_mem_kv"]), bsz, seq)
    x2d = _merge(x2d, z, o_na, y, bonus, gate, o_mem, lp)
    return _ffn(x2d, lp, final_gain, seq, final_norm)


def _trunk(x, mem, layers, final_gain):
    bsz, seq, _ = x.shape
    x2d = x.reshape(bsz * seq, D_MODEL)
    for l, lp in enumerate(layers):
        x2d = _layer(x2d, mem, lp, bsz, seq, final_gain, l == len(layers) - 1)
    return x2d.reshape(bsz, seq, D_MODEL)


def kernel(x_prompt, x_sample, mem_prompt, mem_sample, attn_norm, w_in, na_rpb, rw_conv, rw_decay0, rw_decay2, rw_a0, rw_a2, rw_g2, rw_k_k, rw_k_a, rw_r_k, rw_lnx_w, rw_lnx_b, mem_norm, w_mem_kv, w_branch, w_out, ffn_norm, w_up, ffn_conv, ffn_conv_b, w_down, final_norm):
    params = (attn_norm, w_in, na_rpb, rw_conv, rw_decay0, rw_decay2, rw_a0, rw_a2, rw_g2, rw_k_k, rw_k_a,
              rw_r_k, rw_lnx_w, rw_lnx_b, mem_norm, w_mem_kv, w_branch, w_out, ffn_norm, w_up, ffn_conv,
              ffn_conv_b, w_down)
    layers = [_prep_layer(l, *params) for l in range(attn_norm.shape[0])]
    final_gain = final_norm[None, :]
    y_prompt = _trunk(x_prompt, mem_prompt, layers, final_gain)
    y_sample = _trunk(x_sample, mem_sample, layers, final_gain)
    return (y_prompt, y_sample)
```

```python
import functools

import jax
import jax.numpy as jnp
import numpy as np
from jax import lax
from jax.experimental import pallas as pl
from jax.experimental.pallas import tpu as pltpu

F32 = jnp.float32
BF16 = jnp.bfloat16

D_MODEL = 1024
GRID_W = 64
NA_HEADS = 8
NA_HEAD_DIM = 64
NA_WIDTH = 512
NA_WIN_ROWS = 8
NA_WIN_COLS = 16
RW_HEADS = 8
RW_HEAD_DIM = 64
RW_WIDTH = 512
DECAY_LORA = 64
AAA_LORA = 64
GATE_LORA = 160
GATE_LORA_PAD = 256
MEM_TOKENS = 256
MEM_HEADS = 4
MEM_HEAD_DIM = 128
MEM_WIDTH = 512
N_BRANCH = 3
D_FF = 2816
RMS_EPS = 1e-6
GN_EPS = 64e-5

RW_COLS = 3 * RW_WIDTH + 2 * DECAY_LORA + 2 * AAA_LORA + GATE_LORA
RW_COLS_PAD = 3 * RW_WIDTH + 2 * DECAY_LORA + 2 * AAA_LORA + GATE_LORA_PAD
Z_RW = 0
Z_NA = RW_COLS_PAD
Z_MEM = Z_NA + 3 * NA_WIDTH
Z_GATE = Z_MEM + MEM_WIDTH
Z_COLS = Z_GATE + N_BRANCH * D_MODEL

CHUNK = 64
PAIR = 2 * RW_HEAD_DIM
N_PAIRS = RW_HEADS // 2
INV_LEVELS = 6
NEG_BIG = -1e30
VMEM_LIMIT = 56 * 1024 * 1024


def _cparams(sem):
    return pltpu.CompilerParams(dimension_semantics=sem, vmem_limit_bytes=VMEM_LIMIT)


def _rms(x, gain):
    ms = jnp.mean(x * x, axis=-1, keepdims=True)
    return x * lax.rsqrt(ms + RMS_EPS) * gain


def _dot(a, b):
    return jnp.dot(a, b, preferred_element_type=F32)


def _dot_nt(a, b):
    return lax.dot_general(a, b, (((1,), (1,)), ((), ())), preferred_element_type=F32)


def _split_hi_lo(x):
    hi = x.astype(BF16)
    lo = (x - hi.astype(F32)).astype(BF16)
    return hi, lo


def _sel_left(sel, x):
    hi, lo = _split_hi_lo(x)
    return _dot(sel, hi) + _dot(sel, lo)


def _sel_right(x, sel):
    hi, lo = _split_hi_lo(x)
    return _dot(hi, sel) + _dot(lo, sel)


def _inproj_kernel(x_ref, g_ref, w_ref, o_ref, h_ref):
    @pl.when(pl.program_id(1) == 0)
    def _():
        h_ref[...] = _rms(x_ref[...], g_ref[...]).astype(BF16)

    o_ref[...] = _dot(h_ref[...], w_ref[...]).astype(o_ref.dtype)


def _in_proj(x2d, gain, w):
    t, n = x2d.shape[0], w.shape[1]
    tm = min(1024, t)
    tn = 1024
    return pl.pallas_call(
        _inproj_kernel,
        grid=(t // tm, n // tn),
        in_specs=[
            pl.BlockSpec((tm, D_MODEL), lambda i, j: (i, 0)),
            pl.BlockSpec((1, D_MODEL), lambda i, j: (0, 0)),
            pl.BlockSpec((D_MODEL, tn), lambda i, j: (0, j)),
        ],
        out_specs=pl.BlockSpec((tm, tn), lambda i, j: (i, j)),
        out_shape=jax.ShapeDtypeStruct((t, n), BF16),
        scratch_shapes=[pltpu.VMEM((tm, D_MODEL), BF16)],
        compiler_params=_cparams(("parallel", "arbitrary")),
        name="in_proj",
    )(x2d, gain, w)


NA_ROW_BLOCK = 8
NA_BAND_ROWS = NA_ROW_BLOCK + NA_WIN_ROWS - 1


def _na_kernel(q_ref, k_ref, v_ref, t2_ref, o_ref, *, rows):
    half = NA_WIN_ROWS // 2
    band = NA_WIN_ROWS * GRID_W
    i0 = pl.program_id(1) * NA_ROW_BLOCK
    block_start = jnp.clip(i0 - half, 0, rows - NA_BAND_ROWS)
    lane = lax.broadcasted_iota(jnp.int32, (1, PAIR), 1)
    left = lane < NA_HEAD_DIM
    scale = NA_HEAD_DIM ** -0.5

    def row_step(m, carry):
        i = i0 + m
        r0 = jnp.clip(i - half, 0, rows - NA_WIN_ROWS)
        off = r0 - block_start
        dr0 = (NA_WIN_ROWS - 1) - (i - r0)
        q = q_ref[m] * scale
        kb = k_ref[pl.ds(off, NA_WIN_ROWS)].reshape(band, NA_WIDTH)
        vb = v_ref[pl.ds(off, NA_WIN_ROWS)].reshape(band, NA_WIDTH)
        s = []
        for h in range(NA_HEADS):
            ps = slice((h // 2) * PAIR, (h // 2 + 1) * PAIR)
            qm = jnp.where(left if h % 2 == 0 else ~left, q[:, ps], 0.0).astype(BF16)
            bias = jnp.concatenate([t2_ref[h, dr0 + 2 * jj] for jj in range(NA_WIN_ROWS // 2)], axis=1)
            s.append(_dot_nt(qm, kb[:, ps]) + bias)
        mx = [jnp.max(x, axis=-1, keepdims=True) for x in s]
        p = [jnp.exp(x - m_) for x, m_ in zip(s, mx)]
        inv = [1.0 / jnp.sum(x, axis=-1, keepdims=True) for x in p]
        outs = []
        for pp in range(NA_HEADS // 2):
            vp = vb[:, pp * PAIR:(pp + 1) * PAIR]
            o_l = _dot(p[2 * pp].astype(BF16), jnp.where(left, vp, 0.0).astype(BF16))
            o_r = _dot(p[2 * pp + 1].astype(BF16), jnp.where(left, 0.0, vp).astype(BF16))
            outs.append(o_l * inv[2 * pp] + o_r * inv[2 * pp + 1])
        o_ref[m] = jnp.concatenate(outs, axis=-1).astype(o_ref.dtype)
        return carry

    lax.fori_loop(0, NA_ROW_BLOCK, row_step, 0)


def _na_bias_table(rpb):
    x = np.arange(GRID_W)[:, None]
    c = np.arange(GRID_W)[None, :]
    c0 = np.clip(x - NA_WIN_COLS // 2, 0, GRID_W - NA_WIN_COLS)
    valid = ((c >= c0) & (c < c0 + NA_WIN_COLS)).reshape(-1)
    dc = np.clip(c - x + (NA_WIN_COLS - 1), 0, 2 * NA_WIN_COLS - 2).reshape(-1)
    onehot = (dc[None, :] == np.arange(2 * NA_WIN_COLS - 1)[:, None]) & valid[None, :]
    tab = jnp.einsum('hrd,dn->hrn', rpb, jnp.asarray(onehot, F32), precision=lax.Precision.HIGHEST)
    tab = tab + jnp.asarray(np.where(valid, 0.0, NEG_BIG), F32)
    tab = tab.reshape(NA_HEADS, 2 * NA_WIN_ROWS - 1, GRID_W, GRID_W)
    return jnp.concatenate([tab[:, :-1], tab[:, 1:]], axis=-1)


def _na_attention(z, bias_tab, bsz, seq):
    rows = seq // GRID_W
    assert rows >= NA_BAND_ROWS and rows % NA_ROW_BLOCK == 0
    z4 = z.reshape(bsz, rows, GRID_W, Z_COLS)
    half = NA_WIN_ROWS // 2

    def kv_spec(col):
        return pl.BlockSpec(
            (pl.Squeezed(), pl.Element(NA_BAND_ROWS), pl.Element(GRID_W), pl.Element(NA_WIDTH)),
            lambda b, i: (b, jnp.clip(i * NA_ROW_BLOCK - half, 0, rows - NA_BAND_ROWS), 0, col))

    qblk = Z_NA // NA_WIDTH
    out = pl.pallas_call(
        functools.partial(_na_kernel, rows=rows),
        grid=(bsz, rows // NA_ROW_BLOCK),
        in_specs=[
            pl.BlockSpec((None, NA_ROW_BLOCK, GRID_W, NA_WIDTH), lambda b, i: (b, i, 0, qblk)),
            kv_spec(Z_NA + NA_WIDTH),
            kv_spec(Z_NA + 2 * NA_WIDTH),
            pl.BlockSpec(bias_tab.shape, lambda b, i: (0, 0, 0, 0)),
        ],
        out_specs=pl.BlockSpec((None, NA_ROW_BLOCK, GRID_W, NA_WIDTH), lambda b, i: (b, i, 0, 0)),
        out_shape=jax.ShapeDtypeStruct((bsz, rows, GRID_W, NA_WIDTH), BF16),
        compiler_params=_cparams(("parallel", "arbitrary")),
        name="na_attn",
    )(z4, z4, z4, bias_tab)
    return out.reshape(bsz * seq, NA_WIDTH)


def _softplus(u):
    return jnp.maximum(u, 0.0) + jnp.log1p(jnp.exp(-jnp.abs(u)))


def _rwkv_kernel(zc_ref, zp_ref, zn_ref, cw_ref, d0_ref, d2_ref, a0_ref, a2_ref, g2_ref,
                 kk_ref, ka_ref, rk_ref, bones_ref, tri_ref, aft_ref,
                 y_ref, bonus_ref, gate_ref,
                 s_ref, rt_ref, at_ref, bt_ref, kt_ref, bh_ref, kh_ref, v_ref, tot_ref,
                 *, tile, reverse):
    i = pl.program_id(1)
    nt = pl.num_programs(1)
    ti = nt - 1 - i if reverse else i
    n_chunks = tile // CHUNK

    @pl.when(i == 0)
    def _():
        s_ref[...] = jnp.zeros_like(s_ref)

    zc = zc_ref[...].astype(F32)
    prev_row = jnp.where(ti == 0, 0.0, zp_ref[...].astype(F32)[7:8, :])
    next_row = jnp.where(ti == nt - 1, 0.0, zn_ref[...].astype(F32)[0:1, :])
    rows = lax.broadcasted_iota(jnp.int32, (tile, 1), 0)
    z_prev = jnp.where(rows == 0, prev_row, pltpu.roll(zc, 1, axis=0))
    z_next = jnp.where(rows == tile - 1, next_row, pltpu.roll(zc, tile - 1, axis=0))
    zf = z_prev * cw_ref[0:1, :] + zc * cw_ref[1:2, :] + z_next * cw_ref[2:3, :]

    o1, o2, o3 = RW_WIDTH, 2 * RW_WIDTH, 3 * RW_WIDTH
    o4 = o3 + 2 * DECAY_LORA
    o5 = o4 + 2 * AAA_LORA
    r = zf[:, 0:o1]
    k = zf[:, o1:o2]
    v = zf[:, o2:o3]
    xw = zf[:, o3:o4]
    xa = zf[:, o4:o5]
    xg = zf[:, o5:RW_COLS_PAD]

    bones = bones_ref[...]
    kk = k * kk_ref[...]
    kk = kk * lax.rsqrt(jnp.maximum(_dot((kk * kk).astype(BF16), bones), 1e-24))
    gate_ref[...] = _dot(jax.nn.sigmoid(xg).astype(BF16), g2_ref[...]).astype(gate_ref.dtype)

    w_log = -_softplus(-(d0_ref[...] + _dot(jnp.tanh(xw).astype(BF16), d2_ref[...]))) - 0.5
    logw = -jnp.exp(w_log)
    a = jax.nn.sigmoid(a0_ref[...] + _dot(xa.astype(BF16), a2_ref[...]))
    k_d = k * (1.0 + (a - 1.0) * ka_ref[...])
    b = kk * a
    bonus_ref[...] = (_dot((r * k_d * rk_ref[...]).astype(BF16), bones) * v).astype(bonus_ref.dtype)

    g = _sel_left(tri_ref[...], logw)
    rest = _sel_left(aft_ref[...], logw)
    tot = g + rest
    eng = jnp.exp(-g)
    ehat = jnp.exp(rest)
    rt_ref[...] = r * jnp.exp(g)
    at_ref[...] = -kk * jnp.exp(g - logw)
    bt_ref[...] = b * eng
    kt_ref[...] = k_d * eng
    bh_ref[...] = b * ehat
    kh_ref[...] = k_d * ehat
    v_ref[...] = v
    tot_ref[...] = tot

    ri = lax.broadcasted_iota(jnp.int32, (PAIR, PAIR), 0)
    ci = lax.broadcasted_iota(jnp.int32, (PAIR, PAIR), 1)
    same_head = jnp.where(jnp.right_shift(ri, 6) == jnp.right_shift(ci, 6), 1.0, 0.0).astype(F32)
    order = ci - ri if reverse else ri - ci
    m_strict = jnp.where(order > 0, same_head, 0.0)
    m_incl = jnp.where(order >= 0, same_head, 0.0)
    eye = jnp.where(ci == ri, 1.0, 0.0).astype(F32)
    lvl = []
    for q in range(INV_LEVELS):
        in_block = jnp.where(jnp.right_shift(ri, q + 1) == jnp.right_shift(ci, q + 1), 1.0, 0.0).astype(F32)
        lvl.append(jnp.where(jnp.right_shift(ri, q) != jnp.right_shift(ci, q), in_block, 0.0))
    lane = lax.broadcasted_iota(jnp.int32, (CHUNK, PAIR), 1)
    left = lane < RW_HEAD_DIM

    def stack(x):
        return jnp.concatenate([jnp.where(left, x, 0.0), jnp.where(left, 0.0, x)], axis=0)

    groups = [(c, p) for c in range(n_chunks) for p in range(N_PAIRS)]
    eye_b = eye.astype(BF16)

    def tile_of(ref, c, p):
        return ref[c * CHUNK:(c + 1) * CHUNK, p * PAIR:(p + 1) * PAIR]

    rt = [stack(tile_of(rt_ref, c, p)) for c, p in groups]
    at_b = [stack(tile_of(at_ref, c, p)).astype(BF16) for c, p in groups]
    vs_b = [stack(tile_of(v_ref, c, p)).astype(BF16) for c, p in groups]
    o1_ = []
    for gi, (c, p) in enumerate(groups):
        bt = tile_of(bt_ref, c, p).astype(BF16)
        kt = tile_of(kt_ref, c, p).astype(BF16)
        lhs1 = jnp.concatenate([at_b[gi], rt[gi].astype(BF16)], axis=0)
        o1_.append(_dot_nt(lhs1, jnp.concatenate([bt, bt, kt, kt], axis=0)))
    n_ab = [o[0:PAIR, 0:PAIR] * m_strict for o in o1_]
    a_ak = [(o[0:PAIR, PAIR:] * m_strict).astype(BF16) for o in o1_]
    a_r = [jnp.concatenate([o[PAIR:, 0:PAIR] * m_incl, o[PAIR:, PAIR:] * m_incl], axis=1).astype(BF16)
           for o in o1_]

    t_inv = [eye + n * lvl[0] for n in n_ab]
    for q in range(1, INV_LEVELS):
        xq = [_dot((n * lvl[q]).astype(BF16), t.astype(BF16)) for n, t in zip(n_ab, t_inv)]
        t_inv = [t + _dot(t.astype(BF16), x.astype(BF16)) for t, x in zip(t_inv, xq)]

    av = [_dot(a, v_) for a, v_ in zip(a_ak, vs_b)]
    pu = [_dot(t.astype(BF16), jnp.concatenate([a, x.astype(BF16)], axis=1))
          for t, a, x in zip(t_inv, at_b, av)]
    zero_b = jnp.zeros((PAIR, PAIR), BF16)
    qb, y_loc, n_st = [], [], []
    for gi, (c, p) in enumerate(groups):
        bh_t = stack(tile_of(bh_ref, c, p)).T
        kh_t = stack(tile_of(kh_ref, c, p)).T
        wmat = jnp.concatenate(
            [pu[gi].astype(BF16), jnp.concatenate([zero_b, vs_b[gi]], axis=1)], axis=0)
        lhs2 = jnp.concatenate([a_r[gi], jnp.concatenate([bh_t, kh_t], axis=1).astype(BF16)], axis=0)
        o2_ = _dot(lhs2, wmat)
        q_mat = rt[gi] + o2_[0:PAIR, 0:PAIR]
        qb.append(jnp.concatenate([q_mat, o2_[PAIR:, 0:PAIR]], axis=0).astype(BF16))
        y_loc.append(o2_[0:PAIR, PAIR:])
        n_st.append(o2_[PAIR:, PAIR:])

    rep = PAIR // n_chunks
    w_tot = []
    for p in range(N_PAIRS):
        t_rows = jnp.concatenate(
            [tot_ref[c * CHUNK:c * CHUNK + rep, p * PAIR:(p + 1) * PAIR] for c in range(n_chunks)], axis=0)
        hi = t_rows.astype(BF16)
        r1 = t_rows - hi.astype(F32)
        mid = r1.astype(BF16)
        lo = (r1 - mid.astype(F32)).astype(BF16)
        cols = _dot_nt(eye_b, hi) + _dot_nt(eye_b, mid) + _dot_nt(eye_b, lo)
        w_tot.append([jnp.exp(cols[:, c * rep:c * rep + 1]) for c in range(n_chunks)])

    st = [s_ref[p] for p in range(N_PAIRS)]
    for c in (reversed(range(n_chunks)) if reverse else range(n_chunks)):
        o3_ = [_dot(qb[c * N_PAIRS + p], st[p].astype(BF16)) for p in range(N_PAIRS)]
        for p in range(N_PAIRS):
            gi = c * N_PAIRS + p
            y_bs = o3_[p][0:PAIR] + y_loc[gi]
            st[p] = st[p] * w_tot[p][c] + o3_[p][PAIR:] + n_st[gi]
            y_ref[c * CHUNK:(c + 1) * CHUNK, p * PAIR:(p + 1) * PAIR] = y_bs[0:CHUNK] + y_bs[CHUNK:]
    for p in range(N_PAIRS):
        s_ref[p] = st[p]


def _rwkv_scan(z, lp, bsz, seq, reverse):
    tile = min(256, seq)
    nt = seq // tile
    z3 = z.reshape(bsz, seq, Z_COLS)
    t8 = tile // 8
    d = int(reverse)

    def tidx(i):
        return nt - 1 - i if reverse else i

    cidx = np.arange(tile)
    same = (cidx[:, None] // CHUNK) == (cidx[None, :] // CHUNK)
    upto = cidx[None, :] >= cidx[:, None] if reverse else cidx[None, :] <= cidx[:, None]
    tri = jnp.asarray(same & upto, BF16)
    aft = jnp.asarray(same & ~upto, BF16)

    def full(shape):
        return pl.BlockSpec(shape, lambda b, i: (0,) * len(shape))

    def per_dir(shape):
        return pl.BlockSpec((None,) + shape, lambda b, i: (d,) + (0,) * len(shape))

    out_spec = pl.BlockSpec((None, tile, RW_WIDTH), lambda b, i: (b, tidx(i), 0))
    nat = pltpu.VMEM((tile, RW_WIDTH), F32)
    return pl.pallas_call(
        functools.partial(_rwkv_kernel, tile=tile, reverse=reverse),
        grid=(bsz, nt),
        in_specs=[
            pl.BlockSpec((None, tile, RW_COLS_PAD), lambda b, i: (b, tidx(i), 0)),
            pl.BlockSpec((None, 8, RW_COLS_PAD), lambda b, i: (b, jnp.maximum(tidx(i) * t8 - 1, 0), 0)),
            pl.BlockSpec((None, 8, RW_COLS_PAD),
                         lambda b, i: (b, jnp.minimum((tidx(i) + 1) * t8, seq // 8 - 1), 0)),
            full((3, RW_COLS_PAD)),
            per_dir((1, RW_WIDTH)), per_dir((2 * DECAY_LORA, RW_WIDTH)),
            per_dir((1, RW_WIDTH)), per_dir((2 * AAA_LORA, RW_WIDTH)),
            full((GATE_LORA_PAD, RW_WIDTH)),
            full((1, RW_WIDTH)), full((1, RW_WIDTH)), full((1, RW_WIDTH)),
            full((RW_WIDTH, RW_WIDTH)),
            full((tile, tile)), full((tile, tile)),
        ],
        out_specs=[out_spec, out_spec, out_spec],
        out_shape=[jax.ShapeDtypeStruct((bsz, seq, RW_WIDTH), F32),
                   jax.ShapeDtypeStruct((bsz, seq, RW_WIDTH), BF16),
                   jax.ShapeDtypeStruct((bsz, seq, RW_WIDTH), BF16)],
        scratch_shapes=[pltpu.VMEM((N_PAIRS, PAIR, PAIR), F32)] + [nat] * 8,
        compiler_params=_cparams(("arbitrary", "arbitrary")),
        name="rwkv_bwd" if reverse else "rwkv_fwd",
    )(z3, z3, z3, lp["rw_conv"], lp["rw_decay0"], lp["rw_decay2"], lp["rw_a0"], lp["rw_a2"],
      lp["rw_g2"], lp["rw_k_k"], lp["rw_k_a"], lp["rw_r_k"], lp["bones"], tri, aft)


def _memkv_kernel(m_ref, g_ref, w_ref, o_ref):
    o_ref[...] = _dot(_rms(m_ref[...], g_ref[...]).astype(BF16), w_ref[...]).astype(o_ref.dtype)


def _mem_kv(mem, gain, w):
    bsz = mem.shape[0]
    return pl.pallas_call(
        _memkv_kernel,
        grid=(bsz,),
        in_specs=[
            pl.BlockSpec((None, MEM_TOKENS, D_MODEL), lambda b: (b, 0, 0)),
            pl.BlockSpec((1, D_MODEL), lambda b: (0, 0)),
            pl.BlockSpec((D_MODEL, 2 * MEM_WIDTH), lambda b: (0, 0)),
        ],
        out_specs=pl.BlockSpec((None, MEM_TOKENS, 2 * MEM_WIDTH), lambda b: (b, 0, 0)),
        out_shape=jax.ShapeDtypeStruct((bsz, MEM_TOKENS, 2 * MEM_WIDTH), BF16),
        compiler_params=_cparams(("parallel",)),
        name="mem_kv",
    )(mem, gain, w)


def _memattn_kernel(q_ref, kv_ref, o_ref):
    scale = MEM_HEAD_DIM ** -0.5
    outs = []
    for h in range(MEM_HEADS):
        sl = slice(h * MEM_HEAD_DIM, (h + 1) * MEM_HEAD_DIM)
        vsl = slice(MEM_WIDTH + h * MEM_HEAD_DIM, MEM_WIDTH + (h + 1) * MEM_HEAD_DIM)
        s = _dot_nt(q_ref[:, sl], kv_ref[:, sl]) * scale
        m = jnp.max(s, axis=-1, keepdims=True)
        p = jnp.exp(s - m)
        l = jnp.sum(p, axis=-1, keepdims=True)
        outs.append(_dot(p.astype(BF16), kv_ref[:, vsl]) / l)
    o_ref[...] = jnp.concatenate(outs, axis=-1).astype(o_ref.dtype)


def _mem_attention(z, kv, bsz, seq):
    tm = min(1024, seq)
    z3 = z.reshape(bsz, seq, Z_COLS)
    out = pl.pallas_call(
        _memattn_kernel,
        grid=(bsz, seq // tm),
        in_specs=[
            pl.BlockSpec((None, tm, MEM_WIDTH), lambda b, i: (b, i, Z_MEM // MEM_WIDTH)),
            pl.BlockSpec((None, MEM_TOKENS, 2 * MEM_WIDTH), lambda b, i: (b, 0, 0)),
        ],
        out_specs=pl.BlockSpec((None, tm, MEM_WIDTH), lambda b, i: (b, i, 0)),
        out_shape=jax.ShapeDtypeStruct((bsz, seq, MEM_WIDTH), BF16),
        compiler_params=_cparams(("parallel", "parallel")),
        name="mem_attn",
    )(z3, kv)
    return out.reshape(bsz * seq, MEM_WIDTH)


def _merge_kernel(x_ref, ona_ref, y0_ref, y1_ref, b0_ref, b1_ref, gt_ref, omem_ref,
                  g0_ref, g1_ref, g2_ref, wb_ref, wo_ref, lw_ref, lb_ref, bones_ref, o_ref):
    bones = bones_ref[...]
    inv_n = 1.0 / RW_HEAD_DIM
    y = y0_ref[...] + y1_ref[...]
    mu = _sel_right(y, bones) * inv_n
    dl = y - mu
    var = _sel_right(dl * dl, bones) * inv_n
    yn = dl * lax.rsqrt(var + GN_EPS)
    bonus = b0_ref[...].astype(F32) + b1_ref[...].astype(F32)
    o_rw = (yn * lw_ref[...] + lb_ref[...] + bonus) * gt_ref[...].astype(F32)

    merged = jax.nn.sigmoid(g0_ref[...].astype(F32)) * _dot(ona_ref[...], wb_ref[0])
    merged = merged + jax.nn.sigmoid(g1_ref[...].astype(F32)) * _dot(o_rw.astype(BF16), wb_ref[1])
    merged = merged + jax.nn.sigmoid(g2_ref[...].astype(F32)) * _dot(omem_ref[...], wb_ref[2])
    o_ref[...] = x_ref[...] + _dot(merged.astype(BF16), wo_ref[...])


def _merge(x2d, z, o_na, rw_fwd, rw_bwd, o_mem, lp):
    t = x2d.shape[0]
    tm = min(512, t)
    y0, b0, gate = (a.reshape(t, RW_WIDTH) for a in rw_fwd)
    y1, b1, _ = (a.reshape(t, RW_WIDTH) for a in rw_bwd)
    gblk = Z_GATE // D_MODEL

    def tok(width):
        return pl.BlockSpec((tm, width), lambda i: (i, 0))

    def const(shape):
        return pl.BlockSpec(shape, lambda i: (0,) * len(shape))

    return pl.pallas_call(
        _merge_kernel,
        grid=(t // tm,),
        in_specs=[
            tok(D_MODEL), tok(NA_WIDTH), tok(RW_WIDTH), tok(RW_WIDTH), tok(RW_WIDTH), tok(RW_WIDTH), tok(RW_WIDTH),
            tok(MEM_WIDTH),
            pl.BlockSpec((tm, D_MODEL), lambda i: (i, gblk)),
            pl.BlockSpec((tm, D_MODEL), lambda i: (i, gblk + 1)),
            pl.BlockSpec((tm, D_MODEL), lambda i: (i, gblk + 2)),
            const((N_BRANCH, NA_WIDTH, D_MODEL)), const((D_MODEL, D_MODEL)),
            const((1, RW_WIDTH)), const((1, RW_WIDTH)), const((RW_WIDTH, RW_WIDTH)),
        ],
        out_specs=tok(D_MODEL),
        out_shape=jax.ShapeDtypeStruct((t, D_MODEL), F32),
        compiler_params=_cparams(("parallel",)),
        name="merge",
    )(x2d, o_na, y0, y1, b0, b1, gate, o_mem, z, z, z, lp["w_branch"], lp["w_out"],
      lp["rw_lnx_w"], lp["rw_lnx_b"], lp["bones"])


def _ffn_kernel(x_ref, xp_ref, xn_ref, g_ref, wv_ref, wg_ref, cv_ref, cg_ref, bv_ref, bg_ref, wd_ref,
                fg_ref, o_ref, h_ref, acc_ref, *, tm, tiles_per_seq, final_norm):
    i = pl.program_id(0)
    j = pl.program_id(1)
    ext = tm + 16

    @pl.when(j == 0)
    def _():
        gain = g_ref[...]
        h_ref[0:8, :] = _rms(xp_ref[...], gain).astype(BF16)
        h_ref[8:8 + tm, :] = _rms(x_ref[...], gain).astype(BF16)
        h_ref[8 + tm:ext, :] = _rms(xn_ref[...], gain).astype(BF16)
        acc_ref[...] = jnp.zeros_like(acc_ref)

    seq_pos = i % tiles_per_seq
    rows = lax.broadcasted_iota(jnp.int32, (tm, 1), 0)
    kill_prev = (rows == 0) & (seq_pos == 0)
    kill_next = (rows == tm - 1) & (seq_pos == tiles_per_seq - 1)
    hx = h_ref[...]

    def conv(u, cw, bias):
        up = jnp.where(kill_prev, 0.0, pltpu.roll(u, 1, axis=0)[8:8 + tm])
        un = jnp.where(kill_next, 0.0, pltpu.roll(u, ext - 1, axis=0)[8:8 + tm])
        return up * cw[0:1, :] + u[8:8 + tm] * cw[1:2, :] + un * cw[2:3, :] + bias

    u_val = conv(_dot(hx, wv_ref[...]), cv_ref[...], bv_ref[...])
    u_gate = conv(_dot(hx, wg_ref[...]), cg_ref[...], bg_ref[...])
    act = (u_gate * jax.nn.sigmoid(u_gate) * u_val).astype(BF16)
    acc_ref[...] += _dot(act, wd_ref[...])

    @pl.when(j == pl.num_programs(1) - 1)
    def _():
        out = x_ref[...] + acc_ref[...]
        if final_norm:
            out = _rms(out, fg_ref[...])
        o_ref[...] = out


def _ffn(x2d, lp, final_gain, seq, final_norm):
    t = x2d.shape[0]
    tm = min(1024, seq)
    tn = 256
    nj = D_FF // tn
    t8 = tm // 8
    return pl.pallas_call(
        functools.partial(_ffn_kernel, tm=tm, tiles_per_seq=seq // tm, final_norm=final_norm),
        grid=(t // tm, nj),
        in_specs=[
            pl.BlockSpec((tm, D_MODEL), lambda i, j: (i, 0)),
            pl.BlockSpec((8, D_MODEL), lambda i, j: (jnp.maximum(i * t8 - 1, 0), 0)),
            pl.BlockSpec((8, D_MODEL), lambda i, j: (jnp.minimum((i + 1) * t8, t // 8 - 1), 0)),
            pl.BlockSpec((1, D_MODEL), lambda i, j: (0, 0)),
            pl.BlockSpec((D_MODEL, tn), lambda i, j: (0, j)),
            pl.BlockSpec((D_MODEL, tn), lambda i, j: (0, nj + j)),
            pl.BlockSpec((3, tn), lambda i, j: (0, j)),
            pl.BlockSpec((3, tn), lambda i, j: (0, nj + j)),
            pl.BlockSpec((1, tn), lambda i, j: (0, j)),
            pl.BlockSpec((1, tn), lambda i, j: (0, nj + j)),
            pl.BlockSpec((tn, D_MODEL), lambda i, j: (j, 0)),
            pl.BlockSpec((1, D_MODEL), lambda i, j: (0, 0)),
        ],
        out_specs=pl.BlockSpec((tm, D_MODEL), lambda i, j: (i, 0)),
        out_shape=jax.ShapeDtypeStruct((t, D_MODEL), F32),
        scratch_shapes=[pltpu.VMEM((tm + 16, D_MODEL), BF16), pltpu.VMEM((tm, D_MODEL), F32)],
        compiler_params=_cparams(("parallel", "arbitrary")),
        name="ffn",
    )(x2d, x2d, x2d, lp["ffn_norm"], lp["w_up"], lp["w_up"], lp["ffn_conv"], lp["ffn_conv"],
      lp["ffn_conv_b"], lp["ffn_conv_b"], lp["w_down"], final_gain)


def _prep_layer(l, attn_norm, w_in, na_rpb, rw_conv, rw_decay0, rw_decay2, rw_a0, rw_a2, rw_g2, rw_k_k,
                rw_k_a, rw_r_k, rw_lnx_w, rw_lnx_b, mem_norm, w_mem_kv, w_branch, w_out, ffn_norm, w_up,
                ffn_conv, ffn_conv_b, w_down):
    c1 = 3 * NA_WIDTH
    c2 = c1 + RW_COLS
    w = w_in[l]
    pad_cols = RW_COLS_PAD - RW_COLS
    w_new = jnp.concatenate(
        [w[:, c1:c2], jnp.zeros((D_MODEL, pad_cols), w.dtype), w[:, :c1], w[:, c2:]], axis=1)

    def lora_rows(m):
        zero = jnp.zeros_like(m[0])
        return jnp.stack([jnp.concatenate([m[0], zero], 0), jnp.concatenate([zero, m[1]], 0)]).astype(BF16)

    head = np.arange(RW_WIDTH) // RW_HEAD_DIM
    return dict(
        attn_norm=attn_norm[l][None, :],
        w_in=w_new.astype(BF16),
        na_bias=_na_bias_table(na_rpb[l]),
        rw_conv=jnp.pad(rw_conv[l], ((0, 0), (0, pad_cols))),
        rw_decay0=rw_decay0[l][:, None, :],
        rw_decay2=lora_rows(rw_decay2[l]),
        rw_a0=rw_a0[l][:, None, :],
        rw_a2=lora_rows(rw_a2[l]),
        rw_g2=jnp.pad(rw_g2[l], ((0, GATE_LORA_PAD - GATE_LORA), (0, 0))).astype(BF16),
        rw_k_k=rw_k_k[l][None, :],
        rw_k_a=rw_k_a[l][None, :],
        rw_r_k=rw_r_k[l].reshape(1, RW_WIDTH),
        rw_lnx_w=rw_lnx_w[l][None, :],
        rw_lnx_b=rw_lnx_b[l][None, :],
        bones=jnp.asarray(head[:, None] == head[None, :], BF16),
        mem_norm=mem_norm[l][None, :],
        w_mem_kv=w_mem_kv[l].astype(BF16),
        w_branch=w_branch[l].astype(BF16),
        w_out=w_out[l].astype(BF16),
        ffn_norm=ffn_norm[l][None, :],
        w_up=w_up[l].astype(BF16),
        ffn_conv=ffn_conv[l],
        ffn_conv_b=ffn_conv_b[l][None, :],
        w_down=w_down[l].astype(BF16),
    )


def _layer(x2d, mem, lp, bsz, seq, final_gain, final_norm):
    z = _in_proj(x2d, lp["attn_norm"], lp["w_in"])
    o_na = _na_attention(z, lp["na_bias"], bsz, seq)
    rw_fwd = _rwkv_scan(z, lp, bsz, seq, reverse=False)
    rw_bwd = _rwkv_scan(z, lp, bsz, seq, reverse=True)
    o_mem = _mem_attention(z, _mem_kv(mem, lp["mem_norm"], lp["w_mem_kv"]), bsz, seq)
    x2d = _merge(x2d, z, o_na, rw_fwd, rw_bwd, o_mem, lp)
    return _ffn(x2d, lp, final_gain, seq, final_norm)


def _trunk(x, mem, layers, final_gain):
    bsz, seq, _ = x.shape
    x2d = x.reshape(bsz * seq, D_MODEL)
    for l, lp in enumerate(layers):
        x2d = _layer(x2d, mem, lp, bsz, seq, final_gain, l == len(layers) - 1)
    return x2d.reshape(bsz, seq, D_MODEL)


def kernel(x_prompt, x_sample, mem_prompt, mem_sample, attn_norm, w_in, na_rpb, rw_conv, rw_decay0, rw_decay2, rw_a0, rw_a2, rw_g2, rw_k_k, rw_k_a, rw_r_k, rw_lnx_w, rw_lnx_b, mem_norm, w_mem_kv, w_branch, w_out, ffn_norm, w_up, ffn_conv, ffn_conv_b, w_down, final_norm):
    params = (attn_norm, w_in, na_rpb, rw_conv, rw_decay0, rw_decay2, rw_a0, rw_a2, rw_g2, rw_k_k, rw_k_a,
              rw_r_k, rw_lnx_w, rw_lnx_b, mem_norm, w_mem_kv, w_branch, w_out, ffn_norm, w_up, ffn_conv,
              ffn_conv_b, w_down)
    layers = [_prep_layer(l, *params) for l in range(attn_norm.shape[0])]
    final_gain = final_norm[None, :]
    y_prompt = _trunk(x_prompt, mem_prompt, layers, final_gain)
    y_sample = _trunk(x_sample, mem_sample, layers, final_gain)
    return (y_prompt, y_sample)
```

```python
import functools

import jax
import jax.numpy as jnp
import numpy as np
from jax import lax
from jax.experimental import pallas as pl
from jax.experimental.pallas import tpu as pltpu

F32 = jnp.float32
BF16 = jnp.bfloat16

D_MODEL = 1024
GRID_W = 64
NA_HEADS = 8
NA_HEAD_DIM = 64
NA_WIDTH = 512
NA_WIN_ROWS = 8
NA_WIN_COLS = 16
RW_HEADS = 8
RW_HEAD_DIM = 64
RW_WIDTH = 512
DECAY_LORA = 64
AAA_LORA = 64
GATE_LORA = 160
GATE_LORA_PAD = 256
MEM_TOKENS = 256
MEM_HEADS = 4
MEM_HEAD_DIM = 128
MEM_WIDTH = 512
N_BRANCH = 3
D_FF = 2816
RMS_EPS = 1e-6
GN_EPS = 64e-5

RW_COLS = 3 * RW_WIDTH + 2 * DECAY_LORA + 2 * AAA_LORA + GATE_LORA
RW_COLS_PAD = 3 * RW_WIDTH + 2 * DECAY_LORA + 2 * AAA_LORA + GATE_LORA_PAD
Z_RW = 0
Z_NA = RW_COLS_PAD
Z_MEM = Z_NA + 3 * NA_WIDTH
Z_GATE = Z_MEM + MEM_WIDTH
Z_COLS = Z_GATE + N_BRANCH * D_MODEL

CHUNK = 64
PAIR = 2 * RW_HEAD_DIM
N_PAIRS = RW_HEADS // 2
INV_LEVELS = 6
NEG_BIG = -1e30
VMEM_LIMIT = 56 * 1024 * 1024


def _cparams(sem):
    return pltpu.CompilerParams(dimension_semantics=sem, vmem_limit_bytes=VMEM_LIMIT)


def _rms(x, gain):
    ms = jnp.mean(x * x, axis=-1, keepdims=True)
    return x * lax.rsqrt(ms + RMS_EPS) * gain


def _dot(a, b):
    return jnp.dot(a, b, preferred_element_type=F32)


def _dot_nt(a, b):
    return lax.dot_general(a, b, (((1,), (1,)), ((), ())), preferred_element_type=F32)


def _split_hi_lo(x):
    hi = x.astype(BF16)
    lo = (x - hi.astype(F32)).astype(BF16)
    return hi, lo


def _sel_left(sel, x):
    hi, lo = _split_hi_lo(x)
    return _dot(sel, hi) + _dot(sel, lo)


def _sel_right(x, sel):
    hi, lo = _split_hi_lo(x)
    return _dot(hi, sel) + _dot(lo, sel)


def _inproj_kernel(x_ref, g_ref, w_ref, o_ref, h_ref):
    @pl.when(pl.program_id(1) == 0)
    def _():
        h_ref[...] = _rms(x_ref[...], g_ref[...]).astype(BF16)

    o_ref[...] = _dot(h_ref[...], w_ref[...]).astype(o_ref.dtype)


def _in_proj(x2d, gain, w):
    t, n = x2d.shape[0], w.shape[1]
    tm = min(1024, t)
    tn = 1024
    return pl.pallas_call(
        _inproj_kernel,
        grid=(t // tm, n // tn),
        in_specs=[
            pl.BlockSpec((tm, D_MODEL), lambda i, j: (i, 0)),
            pl.BlockSpec((1, D_MODEL), lambda i, j: (0, 0)),
            pl.BlockSpec((D_MODEL, tn), lambda i, j: (0, j)),
        ],
        out_specs=pl.BlockSpec((tm, tn), lambda i, j: (i, j)),
        out_shape=jax.ShapeDtypeStruct((t, n), BF16),
        scratch_shapes=[pltpu.VMEM((tm, D_MODEL), BF16)],
        compiler_params=_cparams(("parallel", "arbitrary")),
        name="in_proj",
    )(x2d, gain, w)


NA_ROW_BLOCK = 8
NA_BAND_ROWS = NA_ROW_BLOCK + NA_WIN_ROWS - 1


def _na_kernel(q_ref, k_ref, v_ref, t2_ref, o_ref, *, rows):
    half = NA_WIN_ROWS // 2
    band = NA_WIN_ROWS * GRID_W
    i0 = pl.program_id(1) * NA_ROW_BLOCK
    block_start = jnp.clip(i0 - half, 0, rows - NA_BAND_ROWS)
    lane = lax.broadcasted_iota(jnp.int32, (1, PAIR), 1)
    left = lane < NA_HEAD_DIM
    scale = NA_HEAD_DIM ** -0.5

    def row_step(m, carry):
        i = i0 + m
        r0 = jnp.clip(i - half, 0, rows - NA_WIN_ROWS)
        off = r0 - block_start
        dr0 = (NA_WIN_ROWS - 1) - (i - r0)
        q = q_ref[m] * scale
        kb = k_ref[pl.ds(off, NA_WIN_ROWS)].reshape(band, NA_WIDTH)
        vb = v_ref[pl.ds(off, NA_WIN_ROWS)].reshape(band, NA_WIDTH)
        s = []
        for h in range(NA_HEADS):
            ps = slice((h // 2) * PAIR, (h // 2 + 1) * PAIR)
            qm = jnp.where(left if h % 2 == 0 else ~left, q[:, ps], 0.0).astype(BF16)
            bias = jnp.concatenate([t2_ref[h, dr0 + 2 * jj] for jj in range(NA_WIN_ROWS // 2)], axis=1)
            s.append(_dot_nt(qm, kb[:, ps]) + bias)
        mx = [jnp.max(x, axis=-1, keepdims=True) for x in s]
        p = [jnp.exp(x - m_) for x, m_ in zip(s, mx)]
        inv = [1.0 / jnp.sum(x, axis=-1, keepdims=True) for x in p]
        outs = []
        for pp in range(NA_HEADS // 2):
            vp = vb[:, pp * PAIR:(pp + 1) * PAIR]
            o_l = _dot(p[2 * pp].astype(BF16), jnp.where(left, vp, 0.0).astype(BF16))
            o_r = _dot(p[2 * pp + 1].astype(BF16), jnp.where(left, 0.0, vp).astype(BF16))
            outs.append(o_l * inv[2 * pp] + o_r * inv[2 * pp + 1])
        o_ref[m] = jnp.concatenate(outs, axis=-1).astype(o_ref.dtype)
        return carry

    lax.fori_loop(0, NA_ROW_BLOCK, row_step, 0)


def _na_bias_table(rpb):
    x = np.arange(GRID_W)[:, None]
    c = np.arange(GRID_W)[None, :]
    c0 = np.clip(x - NA_WIN_COLS // 2, 0, GRID_W - NA_WIN_COLS)
    valid = ((c >= c0) & (c < c0 + NA_WIN_COLS)).reshape(-1)
    dc = np.clip(c - x + (NA_WIN_COLS - 1), 0, 2 * NA_WIN_COLS - 2).reshape(-1)
    onehot = (dc[None, :] == np.arange(2 * NA_WIN_COLS - 1)[:, None]) & valid[None, :]
    tab = jnp.einsum('hrd,dn->hrn', rpb, jnp.asarray(onehot, F32), precision=lax.Precision.HIGHEST)
    tab = tab + jnp.asarray(np.where(valid, 0.0, NEG_BIG), F32)
    tab = tab.reshape(NA_HEADS, 2 * NA_WIN_ROWS - 1, GRID_W, GRID_W)
    return jnp.concatenate([tab[:, :-1], tab[:, 1:]], axis=-1)


def _na_attention(z, bias_tab, bsz, seq):
    rows = seq // GRID_W
    assert rows >= NA_BAND_ROWS and rows % NA_ROW_BLOCK == 0
    z4 = z.reshape(bsz, rows, GRID_W, Z_COLS)
    half = NA_WIN_ROWS // 2

    def kv_spec(col):
        return pl.BlockSpec(
            (pl.Squeezed(), pl.Element(NA_BAND_ROWS), pl.Element(GRID_W), pl.Element(NA_WIDTH)),
            lambda b, i: (b, jnp.clip(i * NA_ROW_BLOCK - half, 0, rows - NA_BAND_ROWS), 0, col))

    qblk = Z_NA // NA_WIDTH
    out = pl.pallas_call(
        functools.partial(_na_kernel, rows=rows),
        grid=(bsz, rows // NA_ROW_BLOCK),
        in_specs=[
            pl.BlockSpec((None, NA_ROW_BLOCK, GRID_W, NA_WIDTH), lambda b, i: (b, i, 0, qblk)),
            kv_spec(Z_NA + NA_WIDTH),
            kv_spec(Z_NA + 2 * NA_WIDTH),
            pl.BlockSpec(bias_tab.shape, lambda b, i: (0, 0, 0, 0)),
        ],
        out_specs=pl.BlockSpec((None, NA_ROW_BLOCK, GRID_W, NA_WIDTH), lambda b, i: (b, i, 0, 0)),
        out_shape=jax.ShapeDtypeStruct((bsz, rows, GRID_W, NA_WIDTH), BF16),
        compiler_params=_cparams(("parallel", "arbitrary")),
        name="na_attn",
    )(z4, z4, z4, bias_tab)
    return out.reshape(bsz * seq, NA_WIDTH)


def _softplus(u):
    return jnp.maximum(u, 0.0) + jnp.log1p(jnp.exp(-jnp.abs(u)))


def _rwkv_kernel(zc_ref, zp_ref, zn_ref, cw_ref, d0_ref, d2_ref, a0_ref, a2_ref, g2_ref,
                 kk_ref, ka_ref, rk_ref, bones_ref, tri_ref, aft_ref,
                 y_ref, bonus_ref, gate_ref,
                 s_ref, rt_ref, at_ref, bt_ref, kt_ref, bh_ref, kh_ref, v_ref, tot_ref,
                 *, tile, reverse):
    i = pl.program_id(1)
    nt = pl.num_programs(1)
    ti = nt - 1 - i if reverse else i
    n_chunks = tile // CHUNK

    @pl.when(i == 0)
    def _():
        s_ref[...] = jnp.zeros_like(s_ref)

    zc = zc_ref[...].astype(F32)
    prev_row = jnp.where(ti == 0, 0.0, zp_ref[...].astype(F32)[7:8, :])
    next_row = jnp.where(ti == nt - 1, 0.0, zn_ref[...].astype(F32)[0:1, :])
    rows = lax.broadcasted_iota(jnp.int32, (tile, 1), 0)
    z_prev = jnp.where(rows == 0, prev_row, pltpu.roll(zc, 1, axis=0))
    z_next = jnp.where(rows == tile - 1, next_row, pltpu.roll(zc, tile - 1, axis=0))
    zf = z_prev * cw_ref[0:1, :] + zc * cw_ref[1:2, :] + z_next * cw_ref[2:3, :]

    o1, o2, o3 = RW_WIDTH, 2 * RW_WIDTH, 3 * RW_WIDTH
    o4 = o3 + 2 * DECAY_LORA
    o5 = o4 + 2 * AAA_LORA
    r = zf[:, 0:o1]
    k = zf[:, o1:o2]
    v = zf[:, o2:o3]
    xw = zf[:, o3:o4]
    xa = zf[:, o4:o5]
    xg = zf[:, o5:RW_COLS_PAD]

    bones = bones_ref[...]
    kk = k * kk_ref[...]
    kk = kk * lax.rsqrt(jnp.maximum(_dot((kk * kk).astype(BF16), bones), 1e-24))
    gate_ref[...] = _dot(jax.nn.sigmoid(xg).astype(BF16), g2_ref[...]).astype(gate_ref.dtype)

    w_log = -_softplus(-(d0_ref[...] + _dot(jnp.tanh(xw).astype(BF16), d2_ref[...]))) - 0.5
    logw = -jnp.exp(w_log)
    a = jax.nn.sigmoid(a0_ref[...] + _dot(xa.astype(BF16), a2_ref[...]))
    k_d = k * (1.0 + (a - 1.0) * ka_ref[...])
    b = kk * a
    bonus_ref[...] = (_dot((r * k_d * rk_ref[...]).astype(BF16), bones) * v).astype(bonus_ref.dtype)

    g = _sel_left(tri_ref[...], logw)
    rest = _sel_left(aft_ref[...], logw)
    tot = g + rest
    eng = jnp.exp(-g)
    ehat = jnp.exp(rest)
    rt_ref[...] = r * jnp.exp(g)
    at_ref[...] = -kk * jnp.exp(g - logw)
    bt_ref[...] = b * eng
    kt_ref[...] = k_d * eng
    bh_ref[...] = b * ehat
    kh_ref[...] = k_d * ehat
    v_ref[...] = v
    tot_ref[...] = tot

    ri = lax.broadcasted_iota(jnp.int32, (PAIR, PAIR), 0)
    ci = lax.broadcasted_iota(jnp.int32, (PAIR, PAIR), 1)
    same_head = jnp.where(jnp.right_shift(ri, 6) == jnp.right_shift(ci, 6), 1.0, 0.0).astype(F32)
    order = ci - ri if reverse else ri - ci
    m_strict = jnp.where(order > 0, same_head, 0.0)
    m_incl = jnp.where(order >= 0, same_head, 0.0)
    eye = jnp.where(ci == ri, 1.0, 0.0).astype(F32)
    lvl = []
    for q in range(INV_LEVELS):
        in_block = jnp.where(jnp.right_shift(ri, q + 1) == jnp.right_shift(ci, q + 1), 1.0, 0.0).astype(F32)
        lvl.append(jnp.where(jnp.right_shift(ri, q) != jnp.right_shift(ci, q), in_block, 0.0))
    lane = lax.broadcasted_iota(jnp.int32, (CHUNK, PAIR), 1)
    left = lane < RW_HEAD_DIM

    def stack(x):
        return jnp.concatenate([jnp.where(left, x, 0.0), jnp.where(left, 0.0, x)], axis=0)

    groups = [(c, p) for c in range(n_chunks) for p in range(N_PAIRS)]
    eye_b = eye.astype(BF16)

    def tile_of(ref, c, p):
        return ref[c * CHUNK:(c + 1) * CHUNK, p * PAIR:(p + 1) * PAIR]

    rt = [stack(tile_of(rt_ref, c, p)) for c, p in groups]
    at_b = [stack(tile_of(at_ref, c, p)).astype(BF16) for c, p in groups]
    vs_b = [stack(tile_of(v_ref, c, p)).astype(BF16) for c, p in groups]
    o1_ = []
    for gi, (c, p) in enumerate(groups):
        bt = tile_of(bt_ref, c, p).astype(BF16)
        kt = tile_of(kt_ref, c, p).astype(BF16)
        lhs1 = jnp.concatenate([at_b[gi], rt[gi].astype(BF16)], axis=0)
        o1_.append(_dot_nt(lhs1, jnp.concatenate([bt, bt, kt, kt], axis=0)))
    n_ab = [o[0:PAIR, 0:PAIR] * m_strict for o in o1_]
    a_ak = [(o[0:PAIR, PAIR:] * m_strict).astype(BF16) for o in o1_]
    a_r = [jnp.concatenate([o[PAIR:, 0:PAIR] * m_incl, o[PAIR:, PAIR:] * m_incl], axis=1).astype(BF16)
           for o in o1_]

    t_inv = [eye + n * lvl[0] for n in n_ab]
    for q in range(1, INV_LEVELS):
        xq = [_dot((n * lvl[q]).astype(BF16), t.astype(BF16)) for n, t in zip(n_ab, t_inv)]
        t_inv = [t + _dot(t.astype(BF16), x.astype(BF16)) for t, x in zip(t_inv, xq)]

    av = [_dot(a, v_) for a, v_ in zip(a_ak, vs_b)]
    pu = [_dot(t.astype(BF16), jnp.concatenate([a, x.astype(BF16)], axis=1))
          for t, a, x in zip(t_inv, at_b, av)]
    zero_b = jnp.zeros((PAIR, PAIR), BF16)
    qb, y_loc, n_st = [], [], []
    for gi, (c, p) in enumerate(groups):
        bh_t = stack(tile_of(bh_ref, c, p)).T
        kh_t = stack(tile_of(kh_ref, c, p)).T
        wmat = jnp.concatenate(
            [pu[gi].astype(BF16), jnp.concatenate([zero_b, vs_b[gi]], axis=1)], axis=0)
        lhs2 = jnp.concatenate([a_r[gi], jnp.concatenate([bh_t, kh_t], axis=1).astype(BF16)], axis=0)
        o2_ = _dot(lhs2, wmat)
        q_mat = rt[gi] + o2_[0:PAIR, 0:PAIR]
        qb.append(jnp.concatenate([q_mat, o2_[PAIR:, 0:PAIR]], axis=0).astype(BF16))
        y_loc.append(o2_[0:PAIR, PAIR:])
        n_st.append(o2_[PAIR:, PAIR:])

    rep = PAIR // n_chunks
    w_tot = []
    for p in range(N_PAIRS):
        t_rows = jnp.concatenate(
            [tot_ref[c * CHUNK:c * CHUNK + rep, p * PAIR:(p + 1) * PAIR] for c in range(n_chunks)], axis=0)
        hi = t_rows.astype(BF16)
        r1 = t_rows - hi.astype(F32)
        mid = r1.astype(BF16)
        lo = (r1 - mid.astype(F32)).astype(BF16)
        cols = _dot_nt(eye_b, hi) + _dot_nt(eye_b, mid) + _dot_nt(eye_b, lo)
        w_tot.append([jnp.exp(cols[:, c * rep:c * rep + 1]) for c in range(n_chunks)])

    st = [s_ref[p] for p in range(N_PAIRS)]
    for c in (reversed(range(n_chunks)) if reverse else range(n_chunks)):
        o3_ = [_dot(qb[c * N_PAIRS + p], st[p].astype(BF16)) for p in range(N_PAIRS)]
        for p in range(N_PAIRS):
            gi = c * N_PAIRS + p
            y_bs = o3_[p][0:PAIR] + y_loc[gi]
            st[p] = st[p] * w_tot[p][c] + o3_[p][PAIR:] + n_st[gi]
            y_ref[c * CHUNK:(c + 1) * CHUNK, p * PAIR:(p + 1) * PAIR] = y_bs[0:CHUNK] + y_bs[CHUNK:]
    for p in range(N_PAIRS):
        s_ref[p] = st[p]


def _rwkv_scan(z, lp, bsz, seq, reverse):
    tile = min(256, seq)
    nt = seq // tile
    z3 = z.reshape(bsz, seq, Z_COLS)
    t8 = tile // 8
    d = int(reverse)

    def tidx(i):
        return nt - 1 - i if reverse else i

    cidx = np.arange(tile)
    same = (cidx[:, None] // CHUNK) == (cidx[None, :] // CHUNK)
    upto = cidx[None, :] >= cidx[:, None] if reverse else cidx[None, :] <= cidx[:, None]
    tri = jnp.asarray(same & upto, BF16)
    aft = jnp.asarray(same & ~upto, BF16)

    def full(shape):
        return pl.BlockSpec(shape, lambda b, i: (0,) * len(shape))

    def per_dir(shape):
        return pl.BlockSpec((None,) + shape, lambda b, i: (d,) + (0,) * len(shape))

    out_spec = pl.BlockSpec((None, tile, RW_WIDTH), lambda b, i: (b, tidx(i), 0))
    nat = pltpu.VMEM((tile, RW_WIDTH), F32)
    return pl.pallas_call(
        functools.partial(_rwkv_kernel, tile=tile, reverse=reverse),
        grid=(bsz, nt),
        in_specs=[
            pl.BlockSpec((None, tile, RW_COLS_PAD), lambda b, i: (b, tidx(i), 0)),
            pl.BlockSpec((None, 8, RW_COLS_PAD), lambda b, i: (b, jnp.maximum(tidx(i) * t8 - 1, 0), 0)),
            pl.BlockSpec((None, 8, RW_COLS_PAD),
                         lambda b, i: (b, jnp.minimum((tidx(i) + 1) * t8, seq // 8 - 1), 0)),
            full((3, RW_COLS_PAD)),
            per_dir((1, RW_WIDTH)), per_dir((2 * DECAY_LORA, RW_WIDTH)),
            per_dir((1, RW_WIDTH)), per_dir((2 * AAA_LORA, RW_WIDTH)),
            full((GATE_LORA_PAD, RW_WIDTH)),
            full((1, RW_WIDTH)), full((1, RW_WIDTH)), full((1, RW_WIDTH)),
            full((RW_WIDTH, RW_WIDTH)),
            full((tile, tile)), full((tile, tile)),
        ],
        out_specs=[out_spec, out_spec, out_spec],
        out_shape=[jax.ShapeDtypeStruct((bsz, seq, RW_WIDTH), F32),
                   jax.ShapeDtypeStruct((bsz, seq, RW_WIDTH), BF16),
                   jax.ShapeDtypeStruct((bsz, seq, RW_WIDTH), BF16)],
        scratch_shapes=[pltpu.VMEM((N_PAIRS, PAIR, PAIR), F32)] + [nat] * 8,
        compiler_params=_cparams(("arbitrary", "arbitrary")),
        name="rwkv_bwd" if reverse else "rwkv_fwd",
    )(z3, z3, z3, lp["rw_conv"], lp["rw_decay0"], lp["rw_decay2"], lp["rw_a0"], lp["rw_a2"],
      lp["rw_g2"], lp["rw_k_k"], lp["rw_k_a"], lp["rw_r_k"], lp["bones"], tri, aft)


def _memkv_kernel(m_ref, g_ref, w_ref, o_ref):
    o_ref[...] = _dot(_rms(m_ref[...], g_ref[...]).astype(BF16), w_ref[...]).astype(o_ref.dtype)


def _mem_kv(mem, gain, w):
    bsz = mem.shape[0]
    return pl.pallas_call(
        _memkv_kernel,
        grid=(bsz,),
        in_specs=[
            pl.BlockSpec((None, MEM_TOKENS, D_MODEL), lambda b: (b, 0, 0)),
            pl.BlockSpec((1, D_MODEL), lambda b: (0, 0)),
            pl.BlockSpec((D_MODEL, 2 * MEM_WIDTH), lambda b: (0, 0)),
        ],
        out_specs=pl.BlockSpec((None, MEM_TOKENS, 2 * MEM_WIDTH), lambda b: (b, 0, 0)),
        out_shape=jax.ShapeDtypeStruct((bsz, MEM_TOKENS, 2 * MEM_WIDTH), BF16),
        compiler_params=_cparams(("parallel",)),
        name="mem_kv",
    )(mem, gain, w)


def _memattn_kernel(q_ref, kv_ref, o_ref):
    scale = MEM_HEAD_DIM ** -0.5
    outs = []
    for h in range(MEM_HEADS):
        sl = slice(h * MEM_HEAD_DIM, (h + 1) * MEM_HEAD_DIM)
        vsl = slice(MEM_WIDTH + h * MEM_HEAD_DIM, MEM_WIDTH + (h + 1) * MEM_HEAD_DIM)
        s = _dot_nt(q_ref[:, sl], kv_ref[:, sl]) * scale
        m = jnp.max(s, axis=-1, keepdims=True)
        p = jnp.exp(s - m)
        l = jnp.sum(p, axis=-1, keepdims=True)
        outs.append(_dot(p.astype(BF16), kv_ref[:, vsl]) / l)
    o_ref[...] = jnp.concatenate(outs, axis=-1).astype(o_ref.dtype)


def _mem_attention(z, kv, bsz, seq):
    tm = min(1024, seq)
    z3 = z.reshape(bsz, seq, Z_COLS)
    out = pl.pallas_call(
        _memattn_kernel,
        grid=(bsz, seq // tm),
        in_specs=[
            pl.BlockSpec((None, tm, MEM_WIDTH), lambda b, i: (b, i, Z_MEM // MEM_WIDTH)),
            pl.BlockSpec((None, MEM_TOKENS, 2 * MEM_WIDTH), lambda b, i: (b, 0, 0)),
        ],
        out_specs=pl.BlockSpec((None, tm, MEM_WIDTH), lambda b, i: (b, i, 0)),
        out_shape=jax.ShapeDtypeStruct((bsz, seq, MEM_WIDTH), BF16),
        compiler_params=_cparams(("parallel", "parallel")),
        name="mem_attn",
    )(z3, kv)
    return out.reshape(bsz * seq, MEM_WIDTH)


def _merge_kernel(x_ref, ona_ref, y0_ref, y1_ref, b0_ref, b1_ref, gt_ref, omem_ref,
                  g0_ref, g1_ref, g2_ref, wb_ref, wo_ref, lw_ref, lb_ref, bones_ref, o_ref):
    bones = bones_ref[...]
    inv_n = 1.0 / RW_HEAD_DIM
    y = y0_ref[...] + y1_ref[...]
    mu = _sel_right(y, bones) * inv_n
    dl = y - mu
    var = _sel_right(dl * dl, bones) * inv_n
    yn = dl * lax.rsqrt(var + GN_EPS)
    bonus = b0_ref[...].astype(F32) + b1_ref[...].astype(F32)
    o_rw = (yn * lw_ref[...] + lb_ref[...] + bonus) * gt_ref[...].astype(F32)

    merged = jax.nn.sigmoid(g0_ref[...].astype(F32)) * _dot(ona_ref[...], wb_ref[0])
    merged = merged + jax.nn.sigmoid(g1_ref[...].astype(F32)) * _dot(o_rw.astype(BF16), wb_ref[1])
    merged = merged + jax.nn.sigmoid(g2_ref[...].astype(F32)) * _dot(omem_ref[...], wb_ref[2])
    o_ref[...] = x_ref[...] + _dot(merged.astype(BF16), wo_ref[...])


def _merge(x2d, z, o_na, rw_fwd, rw_bwd, o_mem, lp):
    t = x2d.shape[0]
    tm = min(512, t)
    y0, b0, gate = (a.reshape(t, RW_WIDTH) for a in rw_fwd)
    y1, b1, _ = (a.reshape(t, RW_WIDTH) for a in rw_bwd)
    gblk = Z_GATE // D_MODEL

    def tok(width):
        return pl.BlockSpec((tm, width), lambda i: (i, 0))

    def const(shape):
        return pl.BlockSpec(shape, lambda i: (0,) * len(shape))

    return pl.pallas_call(
        _merge_kernel,
        grid=(t // tm,),
        in_specs=[
            tok(D_MODEL), tok(NA_WIDTH), tok(RW_WIDTH), tok(RW_WIDTH), tok(RW_WIDTH), tok(RW_WIDTH), tok(RW_WIDTH),
            tok(MEM_WIDTH),
            pl.BlockSpec((tm, D_MODEL), lambda i: (i, gblk)),
            pl.BlockSpec((tm, D_MODEL), lambda i: (i, gblk + 1)),
            pl.BlockSpec((tm, D_MODEL), lambda i: (i, gblk + 2)),
            const((N_BRANCH, NA_WIDTH, D_MODEL)), const((D_MODEL, D_MODEL)),
            const((1, RW_WIDTH)), const((1, RW_WIDTH)), const((RW_WIDTH, RW_WIDTH)),
        ],
        out_specs=tok(D_MODEL),
        out_shape=jax.ShapeDtypeStruct((t, D_MODEL), F32),
        compiler_params=_cparams(("parallel",)),
        name="merge",
    )(x2d, o_na, y0, y1, b0, b1, gate, o_mem, z, z, z, lp["w_branch"], lp["w_out"],
      lp["rw_lnx_w"], lp["rw_lnx_b"], lp["bones"])


def _ffn_kernel(x_ref, xp_ref, xn_ref, g_ref, wv_ref, wg_ref, cv_ref, cg_ref, bv_ref, bg_ref, wd_ref,
                fg_ref, o_ref, h_ref, acc_ref, u0_ref, u1_ref, act0_ref, act1_ref,
                *, tm, tiles_per_seq, n_tiles, final_norm):
    i = pl.program_id(0)
    j = pl.program_id(1)
    ext = tm + 16

    @pl.when(j == 0)
    def _():
        gain = g_ref[...]
        seq_pos = i % tiles_per_seq
        h_ref[0:8, :] = jnp.where(seq_pos == 0, 0.0, _rms(xp_ref[...], gain)).astype(BF16)
        h_ref[8:8 + tm, :] = _rms(x_ref[...], gain).astype(BF16)
        h_ref[8 + tm:ext, :] = jnp.where(seq_pos == tiles_per_seq - 1, 0.0, _rms(xn_ref[...], gain)).astype(BF16)
        acc_ref[...] = jnp.zeros_like(acc_ref)
        u1_ref[...] = jnp.zeros_like(u1_ref)
        act0_ref[...] = jnp.zeros_like(act0_ref)
        act1_ref[...] = jnp.zeros_like(act1_ref)

    rc = 128

    def conv(u_ref, r0, cw_ref, bias_ref):
        u = u_ref[r0:r0 + rc + 16, :]
        return (pltpu.roll(u, 1, axis=0)[8:8 + rc] * cw_ref[0:1, :] + u[8:8 + rc] * cw_ref[1:2, :]
                + pltpu.roll(u, rc + 15, axis=0)[8:8 + rc] * cw_ref[2:3, :] + bias_ref[...])

    def step(parity, with_up, with_conv):
        u_cur, u_prev = (u0_ref, u1_ref) if parity == 0 else (u1_ref, u0_ref)
        act_old, act_new = (act0_ref, act1_ref) if parity == 0 else (act1_ref, act0_ref)
        if with_conv:
            for r0 in range(0, tm, rc):
                u_val = conv(u_prev.at[0], r0, cv_ref, bv_ref)
                u_gate = conv(u_prev.at[1], r0, cg_ref, bg_ref)
                act_new[r0:r0 + rc, :] = jnp.where(
                    j >= 1, u_gate * jax.nn.sigmoid(u_gate) * u_val, 0.0).astype(BF16)
        if with_up:
            hx = h_ref[...]
            u_cur[0] = _dot(hx, wv_ref[...])
            u_cur[1] = _dot(hx, wg_ref[...])
        acc_ref[...] += _dot(act_old[...], wd_ref[...])

    for parity in (0, 1):
        pl.when((j < n_tiles) & (j % 2 == parity))(functools.partial(step, parity, True, True))
    pl.when(j == n_tiles)(functools.partial(step, n_tiles % 2, False, True))
    pl.when(j == n_tiles + 1)(functools.partial(step, (n_tiles + 1) % 2, False, False))

    @pl.when(j == pl.num_programs(1) - 1)
    def _():
        out = x_ref[...] + acc_ref[...]
        if final_norm:
            out = _rms(out, fg_ref[...])
        o_ref[...] = out


def _ffn(x2d, lp, final_gain, seq, final_norm):
    t = x2d.shape[0]
    tm = min(1024, seq)
    tn = 256
    nj = D_FF // tn
    t8 = tm // 8

    def stage(j, lag):
        return jnp.clip(j - lag, 0, nj - 1)

    u_buf = pltpu.VMEM((2, tm + 16, tn), F32)
    act_buf = pltpu.VMEM((tm, tn), BF16)
    return pl.pallas_call(
        functools.partial(_ffn_kernel, tm=tm, tiles_per_seq=seq // tm, n_tiles=nj, final_norm=final_norm),
        grid=(t // tm, nj + 2),
        in_specs=[
            pl.BlockSpec((tm, D_MODEL), lambda i, j: (i, 0)),
            pl.BlockSpec((8, D_MODEL), lambda i, j: (jnp.maximum(i * t8 - 1, 0), 0)),
            pl.BlockSpec((8, D_MODEL), lambda i, j: (jnp.minimum((i + 1) * t8, t // 8 - 1), 0)),
            pl.BlockSpec((1, D_MODEL), lambda i, j: (0, 0)),
            pl.BlockSpec((D_MODEL, tn), lambda i, j: (0, stage(j, 0))),
            pl.BlockSpec((D_MODEL, tn), lambda i, j: (0, nj + stage(j, 0))),
            pl.BlockSpec((3, tn), lambda i, j: (0, stage(j, 1))),
            pl.BlockSpec((3, tn), lambda i, j: (0, nj + stage(j, 1))),
            pl.BlockSpec((1, tn), lambda i, j: (0, stage(j, 1))),
            pl.BlockSpec((1, tn), lambda i, j: (0, nj + stage(j, 1))),
            pl.BlockSpec((tn, D_MODEL), lambda i, j: (stage(j, 2), 0)),
            pl.BlockSpec((1, D_MODEL), lambda i, j: (0, 0)),
        ],
        out_specs=pl.BlockSpec((tm, D_MODEL), lambda i, j: (i, 0)),
        out_shape=jax.ShapeDtypeStruct((t, D_MODEL), F32),
        scratch_shapes=[pltpu.VMEM((tm + 16, D_MODEL), BF16), pltpu.VMEM((tm, D_MODEL), F32),
                        u_buf, u_buf, act_buf, act_buf],
        compiler_params=_cparams(("parallel", "arbitrary")),
        name="ffn",
    )(x2d, x2d, x2d, lp["ffn_norm"], lp["w_up"], lp["w_up"], lp["ffn_conv"], lp["ffn_conv"],
      lp["ffn_conv_b"], lp["ffn_conv_b"], lp["w_down"], final_gain)


def _prep_layer(l, attn_norm, w_in, na_rpb, rw_conv, rw_decay0, rw_decay2, rw_a0, rw_a2, rw_g2, rw_k_k,
                rw_k_a, rw_r_k, rw_lnx_w, rw_lnx_b, mem_norm, w_mem_kv, w_branch, w_out, ffn_norm, w_up,
                ffn_conv, ffn_conv_b, w_down):
    c1 = 3 * NA_WIDTH
    c2 = c1 + RW_COLS
    w = w_in[l]
    pad_cols = RW_COLS_PAD - RW_COLS
    w_new = jnp.concatenate(
        [w[:, c1:c2], jnp.zeros((D_MODEL, pad_cols), w.dtype), w[:, :c1], w[:, c2:]], axis=1)

    def lora_rows(m):
        zero = jnp.zeros_like(m[0])
        return jnp.stack([jnp.concatenate([m[0], zero], 0), jnp.concatenate([zero, m[1]], 0)]).astype(BF16)

    head = np.arange(RW_WIDTH) // RW_HEAD_DIM
    return dict(
        attn_norm=attn_norm[l][None, :],
        w_in=w_new.astype(BF16),
        na_bias=_na_bias_table(na_rpb[l]),
        rw_conv=jnp.pad(rw_conv[l], ((0, 0), (0, pad_cols))),
        rw_decay0=rw_decay0[l][:, None, :],
        rw_decay2=lora_rows(rw_decay2[l]),
        rw_a0=rw_a0[l][:, None, :],
        rw_a2=lora_rows(rw_a2[l]),
        rw_g2=jnp.pad(rw_g2[l], ((0, GATE_LORA_PAD - GATE_LORA), (0, 0))).astype(BF16),
        rw_k_k=rw_k_k[l][None, :],
        rw_k_a=rw_k_a[l][None, :],
        rw_r_k=rw_r_k[l].reshape(1, RW_WIDTH),
        rw_lnx_w=rw_lnx_w[l][None, :],
        rw_lnx_b=rw_lnx_b[l][None, :],
        bones=jnp.asarray(head[:, None] == head[None, :], BF16),
        mem_norm=mem_norm[l][None, :],
        w_mem_kv=w_mem_kv[l].astype(BF16),
        w_branch=w_branch[l].astype(BF16),
        w_out=w_out[l].astype(BF16),
        ffn_norm=ffn_norm[l][None, :],
        w_up=w_up[l].astype(BF16),
        ffn_conv=ffn_conv[l],
        ffn_conv_b=ffn_conv_b[l][None, :],
        w_down=w_down[l].astype(BF16),
    )


def _layer(x2d, mem, lp, bsz, seq, final_gain, final_norm):
    z = _in_proj(x2d, lp["attn_norm"], lp["w_in"])
    o_na = _na_attention(z, lp["na_bias"], bsz, seq)
    rw_fwd = _rwkv_scan(z, lp, bsz, seq, reverse=False)
    rw_bwd = _rwkv_scan(z, lp, bsz, seq, reverse=True)
    o_mem = _mem_attention(z, _mem_kv(mem, lp["mem_norm"], lp["w_mem_kv"]), bsz, seq)
    x2d = _merge(x2d, z, o_na, rw_fwd, rw_bwd, o_mem, lp)
    return _ffn(x2d, lp, final_gain, seq, final_norm)


def _trunk(x, mem, layers, final_gain):
    bsz, seq, _ = x.shape
    x2d = x.reshape(bsz * seq, D_MODEL)
    for l, lp in enumerate(layers):
        x2d = _layer(x2d, mem, lp, bsz, seq, final_gain, l == len(layers) - 1)
    return x2d.reshape(bsz, seq, D_MODEL)


def kernel(x_prompt, x_sample, mem_prompt, mem_sample, attn_norm, w_in, na_rpb, rw_conv, rw_decay0, rw_decay2, rw_a0, rw_a2, rw_g2, rw_k_k, rw_k_a, rw_r_k, rw_lnx_w, rw_lnx_b, mem_norm, w_mem_kv, w_branch, w_out, ffn_norm, w_up, ffn_conv, ffn_conv_b, w_down, final_norm):
    params = (attn_norm, w_in, na_rpb, rw_conv, rw_decay0, rw_decay2, rw_a0, rw_a2, rw_g2, rw_k_k, rw_k_a,
              rw_r_k, rw_lnx_w, rw_lnx_b, mem_norm, w_mem_kv, w_branch, w_out, ffn_norm, w_up, ffn_conv,
              ffn_conv_b, w_down)
    layers = [_prep_layer(l, *params) for l in range(attn_norm.shape[0])]
    final_gain = final_norm[None, :]
    y_prompt = _trunk(x_prompt, mem_prompt, layers, final_gain)
    y_sample = _trunk(x_sample, mem_sample, layers, final_gain)
    return (y_prompt, y_sample)
```

```python
import functools

import jax
import jax.numpy as jnp
import numpy as np
from jax import lax
from jax.experimental import pallas as pl
from jax.experimental.pallas import tpu as pltpu

F32 = jnp.float32
BF16 = jnp.bfloat16

D_MODEL = 1024
GRID_W = 64
NA_HEADS = 8
NA_HEAD_DIM = 64
NA_WIDTH = 512
NA_WIN_ROWS = 8
NA_WIN_COLS = 16
RW_HEADS = 8
RW_HEAD_DIM = 64
RW_WIDTH = 512
DECAY_LORA = 64
AAA_LORA = 64
GATE_LORA = 160
GATE_LORA_PAD = 256
MEM_TOKENS = 256
MEM_HEADS = 4
MEM_HEAD_DIM = 128
MEM_WIDTH = 512
N_BRANCH = 3
D_FF = 2816
RMS_EPS = 1e-6
GN_EPS = 64e-5

RW_COLS = 3 * RW_WIDTH + 2 * DECAY_LORA + 2 * AAA_LORA + GATE_LORA
RW_COLS_PAD = 3 * RW_WIDTH + 2 * DECAY_LORA + 2 * AAA_LORA + GATE_LORA_PAD
Z_RW = 0
Z_NA = RW_COLS_PAD
Z_MEM = Z_NA + 3 * NA_WIDTH
Z_GATE = Z_MEM + MEM_WIDTH
Z_COLS = Z_GATE + N_BRANCH * D_MODEL

CHUNK = 64
PAIR = 2 * RW_HEAD_DIM
N_PAIRS = RW_HEADS // 2
INV_LEVELS = 6
NEG_BIG = -1e30
VMEM_LIMIT = 56 * 1024 * 1024


def _cparams(sem):
    return pltpu.CompilerParams(dimension_semantics=sem, vmem_limit_bytes=VMEM_LIMIT)


def _rms(x, gain):
    ms = jnp.mean(x * x, axis=-1, keepdims=True)
    return x * lax.rsqrt(ms + RMS_EPS) * gain


def _dot(a, b):
    return jnp.dot(a, b, preferred_element_type=F32)


def _dot_nt(a, b):
    return lax.dot_general(a, b, (((1,), (1,)), ((), ())), preferred_element_type=F32)


def _split_hi_lo(x):
    hi = x.astype(BF16)
    lo = (x - hi.astype(F32)).astype(BF16)
    return hi, lo


def _sel_left(sel, x):
    hi, lo = _split_hi_lo(x)
    return _dot(sel, hi) + _dot(sel, lo)


def _sel_right(x, sel):
    hi, lo = _split_hi_lo(x)
    return _dot(hi, sel) + _dot(lo, sel)


def _inproj_kernel(x_ref, g_ref, w_ref, o_ref, h_ref):
    @pl.when(pl.program_id(1) == 0)
    def _():
        h_ref[...] = _rms(x_ref[...], g_ref[...]).astype(BF16)

    o_ref[...] = _dot(h_ref[...], w_ref[...]).astype(o_ref.dtype)


def _in_proj(x2d, gain, w):
    t, n = x2d.shape[0], w.shape[1]
    tm = min(1024, t)
    tn = 1024
    return pl.pallas_call(
        _inproj_kernel,
        grid=(t // tm, n // tn),
        in_specs=[
            pl.BlockSpec((tm, D_MODEL), lambda i, j: (i, 0)),
            pl.BlockSpec((1, D_MODEL), lambda i, j: (0, 0)),
            pl.BlockSpec((D_MODEL, tn), lambda i, j: (0, j)),
        ],
        out_specs=pl.BlockSpec((tm, tn), lambda i, j: (i, j)),
        out_shape=jax.ShapeDtypeStruct((t, n), BF16),
        scratch_shapes=[pltpu.VMEM((tm, D_MODEL), BF16)],
        compiler_params=_cparams(("parallel", "arbitrary")),
        name="in_proj",
    )(x2d, gain, w)


NA_ROW_BLOCK = 8
NA_BAND_ROWS = NA_ROW_BLOCK + NA_WIN_ROWS - 1


def _na_kernel(q_ref, k_ref, v_ref, t2_ref, o_ref, *, rows):
    half = NA_WIN_ROWS // 2
    band = NA_WIN_ROWS * GRID_W
    i0 = pl.program_id(1) * NA_ROW_BLOCK
    block_start = jnp.clip(i0 - half, 0, rows - NA_BAND_ROWS)
    lane = lax.broadcasted_iota(jnp.int32, (1, PAIR), 1)
    left = lane < NA_HEAD_DIM
    scale = NA_HEAD_DIM ** -0.5

    def row_step(m, carry):
        i = i0 + m
        r0 = jnp.clip(i - half, 0, rows - NA_WIN_ROWS)
        off = r0 - block_start
        dr0 = (NA_WIN_ROWS - 1) - (i - r0)
        q = q_ref[m] * scale
        kb = k_ref[pl.ds(off, NA_WIN_ROWS)].reshape(band, NA_WIDTH)
        vb = v_ref[pl.ds(off, NA_WIN_ROWS)].reshape(band, NA_WIDTH)
        s = []
        for h in range(NA_HEADS):
            ps = slice((h // 2) * PAIR, (h // 2 + 1) * PAIR)
            qm = jnp.where(left if h % 2 == 0 else ~left, q[:, ps], 0.0).astype(BF16)
            bias = jnp.concatenate([t2_ref[h, dr0 + 2 * jj] for jj in range(NA_WIN_ROWS // 2)], axis=1)
            s.append(_dot_nt(qm, kb[:, ps]) + bias)
        mx = [jnp.max(x, axis=-1, keepdims=True) for x in s]
        p = [jnp.exp(x - m_) for x, m_ in zip(s, mx)]
        inv = [1.0 / jnp.sum(x, axis=-1, keepdims=True) for x in p]
        outs = []
        for pp in range(NA_HEADS // 2):
            vp = vb[:, pp * PAIR:(pp + 1) * PAIR]
            o_l = _dot(p[2 * pp].astype(BF16), jnp.where(left, vp, 0.0).astype(BF16))
            o_r = _dot(p[2 * pp + 1].astype(BF16), jnp.where(left, 0.0, vp).astype(BF16))
            outs.append(o_l * inv[2 * pp] + o_r * inv[2 * pp + 1])
        o_ref[m] = jnp.concatenate(outs, axis=-1).astype(o_ref.dtype)
        return carry

    lax.fori_loop(0, NA_ROW_BLOCK, row_step, 0)


def _na_bias_table(rpb):
    x = np.arange(GRID_W)[:, None]
    c = np.arange(GRID_W)[None, :]
    c0 = np.clip(x - NA_WIN_COLS // 2, 0, GRID_W - NA_WIN_COLS)
    valid = ((c >= c0) & (c < c0 + NA_WIN_COLS)).reshape(-1)
    dc = np.clip(c - x + (NA_WIN_COLS - 1), 0, 2 * NA_WIN_COLS - 2).reshape(-1)
    onehot = (dc[None, :] == np.arange(2 * NA_WIN_COLS - 1)[:, None]) & valid[None, :]
    tab = jnp.einsum('hrd,dn->hrn', rpb, jnp.asarray(onehot, F32), precision=lax.Precision.HIGHEST)
    tab = tab + jnp.asarray(np.where(valid, 0.0, NEG_BIG), F32)
    tab = tab.reshape(NA_HEADS, 2 * NA_WIN_ROWS - 1, GRID_W, GRID_W)
    return jnp.concatenate([tab[:, :-1], tab[:, 1:]], axis=-1)


def _na_attention(z, bias_tab, bsz, seq):
    rows = seq // GRID_W
    assert rows >= NA_BAND_ROWS and rows % NA_ROW_BLOCK == 0
    z4 = z.reshape(bsz, rows, GRID_W, Z_COLS)
    half = NA_WIN_ROWS // 2

    def kv_spec(col):
        return pl.BlockSpec(
            (pl.Squeezed(), pl.Element(NA_BAND_ROWS), pl.Element(GRID_W), pl.Element(NA_WIDTH)),
            lambda b, i: (b, jnp.clip(i * NA_ROW_BLOCK - half, 0, rows - NA_BAND_ROWS), 0, col))

    qblk = Z_NA // NA_WIDTH
    out = pl.pallas_call(
        functools.partial(_na_kernel, rows=rows),
        grid=(bsz, rows // NA_ROW_BLOCK),
        in_specs=[
            pl.BlockSpec((None, NA_ROW_BLOCK, GRID_W, NA_WIDTH), lambda b, i: (b, i, 0, qblk)),
            kv_spec(Z_NA + NA_WIDTH),
            kv_spec(Z_NA + 2 * NA_WIDTH),
            pl.BlockSpec(bias_tab.shape, lambda b, i: (0, 0, 0, 0)),
        ],
        out_specs=pl.BlockSpec((None, NA_ROW_BLOCK, GRID_W, NA_WIDTH), lambda b, i: (b, i, 0, 0)),
        out_shape=jax.ShapeDtypeStruct((bsz, rows, GRID_W, NA_WIDTH), BF16),
        compiler_params=_cparams(("parallel", "arbitrary")),
        name="na_attn",
    )(z4, z4, z4, bias_tab)
    return out.reshape(bsz * seq, NA_WIDTH)


def _softplus(u):
    return jnp.maximum(u, 0.0) + jnp.log1p(jnp.exp(-jnp.abs(u)))


def _rwkv_kernel(zc_ref, zp_ref, zn_ref, cw_ref, d0_ref, d2_ref, a0_ref, a2_ref, g2_ref,
                 kk_ref, ka_ref, rk_ref, bones_ref, tri_ref,
                 y_ref, bonus_ref, gate_ref,
                 s_ref, rt_ref, at_ref, bt_ref, kt_ref, bh_ref, kh_ref, v_ref, tot_ref,
                 *, tile, reverse):
    i = pl.program_id(1)
    nt = pl.num_programs(1)
    ti = nt - 1 - i if reverse else i
    n_chunks = tile // CHUNK

    @pl.when(i == 0)
    def _():
        s_ref[...] = jnp.zeros_like(s_ref)

    zc = zc_ref[...].astype(F32)
    prev_row = jnp.where(ti == 0, 0.0, zp_ref[...].astype(F32)[7:8, :])
    next_row = jnp.where(ti == nt - 1, 0.0, zn_ref[...].astype(F32)[0:1, :])
    rows = lax.broadcasted_iota(jnp.int32, (tile, 1), 0)
    z_prev = jnp.where(rows == 0, prev_row, pltpu.roll(zc, 1, axis=0))
    z_next = jnp.where(rows == tile - 1, next_row, pltpu.roll(zc, tile - 1, axis=0))
    zf = z_prev * cw_ref[0:1, :] + zc * cw_ref[1:2, :] + z_next * cw_ref[2:3, :]

    o1, o2, o3 = RW_WIDTH, 2 * RW_WIDTH, 3 * RW_WIDTH
    o4 = o3 + 2 * DECAY_LORA
    o5 = o4 + 2 * AAA_LORA
    r = zf[:, 0:o1]
    k = zf[:, o1:o2]
    v = zf[:, o2:o3]
    xw = zf[:, o3:o4]
    xa = zf[:, o4:o5]
    xg = zf[:, o5:RW_COLS_PAD]

    bones = bones_ref[...]
    kk = k * kk_ref[...]
    kk = kk * lax.rsqrt(jnp.maximum(_dot((kk * kk).astype(BF16), bones), 1e-24))
    gate_ref[...] = _dot(jax.nn.sigmoid(xg).astype(BF16), g2_ref[...]).astype(gate_ref.dtype)

    w_log = -_softplus(-(d0_ref[...] + _dot(jnp.tanh(xw).astype(BF16), d2_ref[...]))) - 0.5
    logw = -jnp.exp(w_log)
    a = jax.nn.sigmoid(a0_ref[...] + _dot(xa.astype(BF16), a2_ref[...]))
    k_d = k * (1.0 + (a - 1.0) * ka_ref[...])
    b = kk * a
    bonus_ref[...] = (_dot((r * k_d * rk_ref[...]).astype(BF16), bones) * v).astype(bonus_ref.dtype)

    span = tri_ref.shape[0]
    g = jnp.concatenate([_sel_left(tri_ref[...], logw[r:r + span]) for r in range(0, tile, span)], axis=0)
    last = 0 if reverse else CHUNK - 1
    tot = jnp.concatenate(
        [jnp.broadcast_to(g[c * CHUNK + last:c * CHUNK + last + 1, :], (CHUNK, RW_WIDTH))
         for c in range(n_chunks)], axis=0)
    eng = jnp.exp(-g)
    ehat = jnp.exp(tot - g)
    rt_ref[...] = r * jnp.exp(g)
    at_ref[...] = -kk * jnp.exp(g - logw)
    bt_ref[...] = b * eng
    kt_ref[...] = k_d * eng
    bh_ref[...] = b * ehat
    kh_ref[...] = k_d * ehat
    v_ref[...] = v
    tot_ref[...] = tot

    ri = lax.broadcasted_iota(jnp.int32, (PAIR, PAIR), 0)
    ci = lax.broadcasted_iota(jnp.int32, (PAIR, PAIR), 1)
    same_head = jnp.where(jnp.right_shift(ri, 6) == jnp.right_shift(ci, 6), 1.0, 0.0).astype(F32)
    order = ci - ri if reverse else ri - ci
    m_strict = jnp.where(order > 0, same_head, 0.0)
    m_incl = jnp.where(order >= 0, same_head, 0.0)
    eye = jnp.where(ci == ri, 1.0, 0.0).astype(F32)
    lvl = []
    for q in range(INV_LEVELS):
        in_block = jnp.where(jnp.right_shift(ri, q + 1) == jnp.right_shift(ci, q + 1), 1.0, 0.0).astype(F32)
        lvl.append(jnp.where(jnp.right_shift(ri, q) != jnp.right_shift(ci, q), in_block, 0.0))
    lane = lax.broadcasted_iota(jnp.int32, (CHUNK, PAIR), 1)
    left = lane < RW_HEAD_DIM

    def stack(x):
        return jnp.concatenate([jnp.where(left, x, 0.0), jnp.where(left, 0.0, x)], axis=0)

    groups = [(c, p) for c in range(n_chunks) for p in range(N_PAIRS)]
    eye_b = eye.astype(BF16)

    def tile_of(ref, c, p):
        return ref[c * CHUNK:(c + 1) * CHUNK, p * PAIR:(p + 1) * PAIR]

    rt = [stack(tile_of(rt_ref, c, p)) for c, p in groups]
    at_b = [stack(tile_of(at_ref, c, p)).astype(BF16) for c, p in groups]
    vs_b = [stack(tile_of(v_ref, c, p)).astype(BF16) for c, p in groups]
    o1_ = []
    for gi, (c, p) in enumerate(groups):
        bt = tile_of(bt_ref, c, p).astype(BF16)
        kt = tile_of(kt_ref, c, p).astype(BF16)
        lhs1 = jnp.concatenate([at_b[gi], rt[gi].astype(BF16)], axis=0)
        o1_.append(_dot_nt(lhs1, jnp.concatenate([bt, bt, kt, kt], axis=0)))
    ms_b = m_strict.astype(BF16)
    mi_b = m_incl.astype(BF16)
    lvl_b = [m.astype(BF16) for m in lvl]
    n_ab = [o[0:PAIR, 0:PAIR] * m_strict for o in o1_]
    n_b = [n.astype(BF16) for n in n_ab]
    a_ak = [o[0:PAIR, PAIR:].astype(BF16) * ms_b for o in o1_]
    a_r = [jnp.concatenate([o[PAIR:, 0:PAIR].astype(BF16) * mi_b, o[PAIR:, PAIR:].astype(BF16) * mi_b], axis=1)
           for o in o1_]

    t_inv = [eye + n * lvl[0] for n in n_ab]
    for q in range(1, INV_LEVELS):
        blk = 1 << q
        if blk < 8:
            xq = [_dot(n * lvl_b[q], t.astype(BF16)) for n, t in zip(n_b, t_inv)]
            t_inv = [t + _dot(t.astype(BF16), x.astype(BF16)) for t, x in zip(t_inv, xq)]
            continue
        n_blocks = PAIR // blk
        late = [k for k in range(n_blocks) if (k % 2 == 0) == reverse]

        def take(x):
            return jnp.concatenate([x[k * blk:(k + 1) * blk] for k in late], axis=0)

        def spread(x):
            zero = jnp.zeros((blk, PAIR), x.dtype)
            return jnp.concatenate(
                [x[late.index(k) * blk:(late.index(k) + 1) * blk] if k in late else zero
                 for k in range(n_blocks)], axis=0)

        if blk % 16 == 0:
            xq = [_dot(take(n) * take(lvl_b[q]), t.astype(BF16)) for n, t in zip(n_b, t_inv)]
            upd = [_dot(take(t).astype(BF16), spread(x.astype(BF16))) for t, x in zip(t_inv, xq)]
        else:
            xq = [_dot(take(n * lvl[q]).astype(BF16), t.astype(BF16)) for n, t in zip(n_ab, t_inv)]
            upd = [_dot(take(t).astype(BF16), spread(x).astype(BF16)) for t, x in zip(t_inv, xq)]
        t_inv = [t + spread(u) for t, u in zip(t_inv, upd)]

    av = [_dot(a, v_) for a, v_ in zip(a_ak, vs_b)]
    pu = [_dot(t.astype(BF16), jnp.concatenate([a, x.astype(BF16)], axis=1))
          for t, a, x in zip(t_inv, at_b, av)]
    zero_b = jnp.zeros((PAIR, PAIR), BF16)
    qb, y_loc, n_st = [], [], []
    for gi, (c, p) in enumerate(groups):
        bh_t = stack(tile_of(bh_ref, c, p)).T
        kh_t = stack(tile_of(kh_ref, c, p)).T
        wmat = jnp.concatenate(
            [pu[gi].astype(BF16), jnp.concatenate([zero_b, vs_b[gi]], axis=1)], axis=0)
        lhs2 = jnp.concatenate([a_r[gi], jnp.concatenate([bh_t, kh_t], axis=1).astype(BF16)], axis=0)
        o2_ = _dot(lhs2, wmat)
        q_mat = rt[gi] + o2_[0:PAIR, 0:PAIR]
        qb.append(jnp.concatenate([q_mat, o2_[PAIR:, 0:PAIR]], axis=0).astype(BF16))
        y_loc.append(o2_[0:PAIR, PAIR:])
        n_st.append(o2_[PAIR:, PAIR:])

    rep = PAIR // n_chunks
    w_tot = []
    for p in range(N_PAIRS):
        t_rows = jnp.concatenate(
            [tot_ref[c * CHUNK:c * CHUNK + rep, p * PAIR:(p + 1) * PAIR] for c in range(n_chunks)], axis=0)
        hi = t_rows.astype(BF16)
        r1 = t_rows - hi.astype(F32)
        mid = r1.astype(BF16)
        lo = (r1 - mid.astype(F32)).astype(BF16)
        cols = _dot_nt(eye_b, hi) + _dot_nt(eye_b, mid) + _dot_nt(eye_b, lo)
        w_tot.append([jnp.exp(cols[:, c * rep:c * rep + 1]) for c in range(n_chunks)])

    st = [s_ref[p] for p in range(N_PAIRS)]
    for c in (reversed(range(n_chunks)) if reverse else range(n_chunks)):
        o3_ = [_dot(qb[c * N_PAIRS + p], st[p].astype(BF16)) for p in range(N_PAIRS)]
        for p in range(N_PAIRS):
            gi = c * N_PAIRS + p
            y_bs = o3_[p][0:PAIR] + y_loc[gi]
            st[p] = st[p] * w_tot[p][c] + o3_[p][PAIR:] + n_st[gi]
            y_ref[c * CHUNK:(c + 1) * CHUNK, p * PAIR:(p + 1) * PAIR] = y_bs[0:CHUNK] + y_bs[CHUNK:]
    for p in range(N_PAIRS):
        s_ref[p] = st[p]


def _rwkv_scan(z, lp, bsz, seq, reverse):
    tile = min(512, seq)
    nt = seq // tile
    z3 = z.reshape(bsz, seq, Z_COLS)
    t8 = tile // 8
    d = int(reverse)

    def tidx(i):
        return nt - 1 - i if reverse else i

    span = min(256, tile)
    cidx = np.arange(span)
    same = (cidx[:, None] // CHUNK) == (cidx[None, :] // CHUNK)
    upto = cidx[None, :] >= cidx[:, None] if reverse else cidx[None, :] <= cidx[:, None]
    tri = jnp.asarray(same & upto, BF16)

    def full(shape):
        return pl.BlockSpec(shape, lambda b, i: (0,) * len(shape))

    def per_dir(shape):
        return pl.BlockSpec((None,) + shape, lambda b, i: (d,) + (0,) * len(shape))

    out_spec = pl.BlockSpec((None, tile, RW_WIDTH), lambda b, i: (b, tidx(i), 0))
    nat = pltpu.VMEM((tile, RW_WIDTH), F32)
    return pl.pallas_call(
        functools.partial(_rwkv_kernel, tile=tile, reverse=reverse),
        grid=(bsz, nt),
        in_specs=[
            pl.BlockSpec((None, tile, RW_COLS_PAD), lambda b, i: (b, tidx(i), 0)),
            pl.BlockSpec((None, 8, RW_COLS_PAD), lambda b, i: (b, jnp.maximum(tidx(i) * t8 - 1, 0), 0)),
            pl.BlockSpec((None, 8, RW_COLS_PAD),
                         lambda b, i: (b, jnp.minimum((tidx(i) + 1) * t8, seq // 8 - 1), 0)),
            full((3, RW_COLS_PAD)),
            per_dir((1, RW_WIDTH)), per_dir((2 * DECAY_LORA, RW_WIDTH)),
            per_dir((1, RW_WIDTH)), per_dir((2 * AAA_LORA, RW_WIDTH)),
            full((GATE_LORA_PAD, RW_WIDTH)),
            full((1, RW_WIDTH)), full((1, RW_WIDTH)), full((1, RW_WIDTH)),
            full((RW_WIDTH, RW_WIDTH)),
            full((span, span)),
        ],
        out_specs=[out_spec, out_spec, out_spec],
        out_shape=[jax.ShapeDtypeStruct((bsz, seq, RW_WIDTH), F32),
                   jax.ShapeDtypeStruct((bsz, seq, RW_WIDTH), BF16),
                   jax.ShapeDtypeStruct((bsz, seq, RW_WIDTH), BF16)],
        scratch_shapes=[pltpu.VMEM((N_PAIRS, PAIR, PAIR), F32)] + [nat] * 8,
        compiler_params=_cparams(("arbitrary", "arbitrary")),
        name="rwkv_bwd" if reverse else "rwkv_fwd",
    )(z3, z3, z3, lp["rw_conv"], lp["rw_decay0"], lp["rw_decay2"], lp["rw_a0"], lp["rw_a2"],
      lp["rw_g2"], lp["rw_k_k"], lp["rw_k_a"], lp["rw_r_k"], lp["bones"], tri)


def _memkv_kernel(m_ref, g_ref, w_ref, o_ref):
    o_ref[...] = _dot(_rms(m_ref[...], g_ref[...]).astype(BF16), w_ref[...]).astype(o_ref.dtype)


def _mem_kv(mem, gain, w):
    bsz = mem.shape[0]
    return pl.pallas_call(
        _memkv_kernel,
        grid=(bsz,),
        in_specs=[
            pl.BlockSpec((None, MEM_TOKENS, D_MODEL), lambda b: (b, 0, 0)),
            pl.BlockSpec((1, D_MODEL), lambda b: (0, 0)),
            pl.BlockSpec((D_MODEL, 2 * MEM_WIDTH), lambda b: (0, 0)),
        ],
        out_specs=pl.BlockSpec((None, MEM_TOKENS, 2 * MEM_WIDTH), lambda b: (b, 0, 0)),
        out_shape=jax.ShapeDtypeStruct((bsz, MEM_TOKENS, 2 * MEM_WIDTH), BF16),
        compiler_params=_cparams(("parallel",)),
        name="mem_kv",
    )(mem, gain, w)


def _memattn_kernel(q_ref, kv_ref, o_ref):
    scale = MEM_HEAD_DIM ** -0.5
    outs = []
    for h in range(MEM_HEADS):
        sl = slice(h * MEM_HEAD_DIM, (h + 1) * MEM_HEAD_DIM)
        vsl = slice(MEM_WIDTH + h * MEM_HEAD_DIM, MEM_WIDTH + (h + 1) * MEM_HEAD_DIM)
        s = _dot_nt(q_ref[:, sl], kv_ref[:, sl]) * scale
        m = jnp.max(s, axis=-1, keepdims=True)
        p = jnp.exp(s - m)
        l = jnp.sum(p, axis=-1, keepdims=True)
        outs.append(_dot(p.astype(BF16), kv_ref[:, vsl]) / l)
    o_ref[...] = jnp.concatenate(outs, axis=-1).astype(o_ref.dtype)


def _mem_attention(z, kv, bsz, seq):
    tm = min(1024, seq)
    z3 = z.reshape(bsz, seq, Z_COLS)
    out = pl.pallas_call(
        _memattn_kernel,
        grid=(bsz, seq // tm),
        in_specs=[
            pl.BlockSpec((None, tm, MEM_WIDTH), lambda b, i: (b, i, Z_MEM // MEM_WIDTH)),
            pl.BlockSpec((None, MEM_TOKENS, 2 * MEM_WIDTH), lambda b, i: (b, 0, 0)),
        ],
        out_specs=pl.BlockSpec((None, tm, MEM_WIDTH), lambda b, i: (b, i, 0)),
        out_shape=jax.ShapeDtypeStruct((bsz, seq, MEM_WIDTH), BF16),
        compiler_params=_cparams(("parallel", "parallel")),
        name="mem_attn",
    )(z3, kv)
    return out.reshape(bsz * seq, MEM_WIDTH)


def _merge_kernel(x_ref, ona_ref, y0_ref, y1_ref, b0_ref, b1_ref, gt_ref, omem_ref,
                  g0_ref, g1_ref, g2_ref, wb_ref, wo_ref, lw_ref, lb_ref, bones_ref, o_ref):
    bones = bones_ref[...]
    inv_n = 1.0 / RW_HEAD_DIM
    y = y0_ref[...] + y1_ref[...]
    mu = _dot(y.astype(BF16), bones) * inv_n
    dl = y - mu
    var = _dot((dl * dl).astype(BF16), bones) * inv_n
    yn = dl * lax.rsqrt(var + GN_EPS)
    bonus = b0_ref[...].astype(F32) + b1_ref[...].astype(F32)
    o_rw = (yn * lw_ref[...] + lb_ref[...] + bonus) * gt_ref[...].astype(F32)

    merged = jax.nn.sigmoid(g0_ref[...].astype(F32)) * _dot(ona_ref[...], wb_ref[0])
    merged = merged + jax.nn.sigmoid(g1_ref[...].astype(F32)) * _dot(o_rw.astype(BF16), wb_ref[1])
    merged = merged + jax.nn.sigmoid(g2_ref[...].astype(F32)) * _dot(omem_ref[...], wb_ref[2])
    o_ref[...] = x_ref[...] + _dot(merged.astype(BF16), wo_ref[...])


def _merge(x2d, z, o_na, rw_fwd, rw_bwd, o_mem, lp):
    t = x2d.shape[0]
    tm = min(512, t)
    y0, b0, gate = (a.reshape(t, RW_WIDTH) for a in rw_fwd)
    y1, b1, _ = (a.reshape(t, RW_WIDTH) for a in rw_bwd)
    gblk = Z_GATE // D_MODEL

    def tok(width):
        return pl.BlockSpec((tm, width), lambda i: (i, 0))

    def const(shape):
        return pl.BlockSpec(shape, lambda i: (0,) * len(shape))

    return pl.pallas_call(
        _merge_kernel,
        grid=(t // tm,),
        in_specs=[
            tok(D_MODEL), tok(NA_WIDTH), tok(RW_WIDTH), tok(RW_WIDTH), tok(RW_WIDTH), tok(RW_WIDTH), tok(RW_WIDTH),
            tok(MEM_WIDTH),
            pl.BlockSpec((tm, D_MODEL), lambda i: (i, gblk)),
            pl.BlockSpec((tm, D_MODEL), lambda i: (i, gblk + 1)),
            pl.BlockSpec((tm, D_MODEL), lambda i: (i, gblk + 2)),
            const((N_BRANCH, NA_WIDTH, D_MODEL)), const((D_MODEL, D_MODEL)),
            const((1, RW_WIDTH)), const((1, RW_WIDTH)), const((RW_WIDTH, RW_WIDTH)),
        ],
        out_specs=tok(D_MODEL),
        out_shape=jax.ShapeDtypeStruct((t, D_MODEL), F32),
        compiler_params=_cparams(("parallel",)),
        name="merge",
    )(x2d, o_na, y0, y1, b0, b1, gate, o_mem, z, z, z, lp["w_branch"], lp["w_out"],
      lp["rw_lnx_w"], lp["rw_lnx_b"], lp["bones"])


def _ffn_kernel(x_ref, xp_ref, xn_ref, g_ref, wv_ref, wg_ref, cv_ref, cg_ref, bv_ref, bg_ref, wd_ref,
                fg_ref, o_ref, h_ref, acc_ref, *, tm, tiles_per_seq, final_norm):
    i = pl.program_id(0)
    j = pl.program_id(1)
    ext = tm + 16

    @pl.when(j == 0)
    def _():
        gain = g_ref[...]
        seq_pos = i % tiles_per_seq
        h_ref[0:8, :] = jnp.where(seq_pos == 0, 0.0, _rms(xp_ref[...], gain)).astype(BF16)
        h_ref[8:8 + tm, :] = _rms(x_ref[...], gain).astype(BF16)
        h_ref[8 + tm:ext, :] = jnp.where(seq_pos == tiles_per_seq - 1, 0.0, _rms(xn_ref[...], gain)).astype(BF16)
        acc_ref[...] = jnp.zeros_like(acc_ref)

    hx = h_ref[...]

    def conv(u, cw, bias):
        return (pltpu.roll(u, 1, axis=0)[8:8 + tm] * cw[0:1, :] + u[8:8 + tm] * cw[1:2, :]
                + pltpu.roll(u, ext - 1, axis=0)[8:8 + tm] * cw[2:3, :] + bias)

    u_val = conv(_dot(hx, wv_ref[...]), cv_ref[...], bv_ref[...])
    u_gate = conv(_dot(hx, wg_ref[...]), cg_ref[...], bg_ref[...])
    act = (u_gate * jax.nn.sigmoid(u_gate) * u_val).astype(BF16)
    acc_ref[...] += _dot(act, wd_ref[...])

    @pl.when(j == pl.num_programs(1) - 1)
    def _():
        out = x_ref[...] + acc_ref[...]
        if final_norm:
            out = _rms(out, fg_ref[...])
        o_ref[...] = out


def _ffn(x2d, lp, final_gain, seq, final_norm):
    t = x2d.shape[0]
    tm = min(1024, seq)
    tn = 256
    nj = D_FF // tn
    t8 = tm // 8
    return pl.pallas_call(
        functools.partial(_ffn_kernel, tm=tm, tiles_per_seq=seq // tm, final_norm=final_norm),
        grid=(t // tm, nj),
        in_specs=[
            pl.BlockSpec((tm, D_MODEL), lambda i, j: (i, 0)),
            pl.BlockSpec((8, D_MODEL), lambda i, j: (jnp.maximum(i * t8 - 1, 0), 0)),
            pl.BlockSpec((8, D_MODEL), lambda i, j: (jnp.minimum((i + 1) * t8, t // 8 - 1), 0)),
            pl.BlockSpec((1, D_MODEL), lambda i, j: (0, 0)),
            pl.BlockSpec((D_MODEL, tn), lambda i, j: (0, j)),
            pl.BlockSpec((D_MODEL, tn), lambda i, j: (0, nj + j)),
            pl.BlockSpec((3, tn), lambda i, j: (0, j)),
            pl.BlockSpec((3, tn), lambda i, j: (0, nj + j)),
            pl.BlockSpec((1, tn), lambda i, j: (0, j)),
            pl.BlockSpec((1, tn), lambda i, j: (0, nj + j)),
            pl.BlockSpec((tn, D_MODEL), lambda i, j: (j, 0)),
            pl.BlockSpec((1, D_MODEL), lambda i, j: (0, 0)),
        ],
        out_specs=pl.BlockSpec((tm, D_MODEL), lambda i, j: (i, 0)),
        out_shape=jax.ShapeDtypeStruct((t, D_MODEL), F32),
        scratch_shapes=[pltpu.VMEM((tm + 16, D_MODEL), BF16), pltpu.VMEM((tm, D_MODEL), F32)],
        compiler_params=_cparams(("parallel", "arbitrary")),
        name="ffn",
    )(x2d, x2d, x2d, lp["ffn_norm"], lp["w_up"], lp["w_up"], lp["ffn_conv"], lp["ffn_conv"],
      lp["ffn_conv_b"], lp["ffn_conv_b"], lp["w_down"], final_gain)


def _prep_layer(l, attn_norm, w_in, na_rpb, rw_conv, rw_decay0, rw_decay2, rw_a0, rw_a2, rw_g2, rw_k_k,
                rw_k_a, rw_r_k, rw_lnx_w, rw_lnx_b, mem_norm, w_mem_kv, w_branch, w_out, ffn_norm, w_up,
                ffn_conv, ffn_conv_b, w_down):
    c1 = 3 * NA_WIDTH
    c2 = c1 + RW_COLS
    w = w_in[l]
    pad_cols = RW_COLS_PAD - RW_COLS
    w_new = jnp.concatenate(
        [w[:, c1:c2], jnp.zeros((D_MODEL, pad_cols), w.dtype), w[:, :c1], w[:, c2:]], axis=1)

    def lora_rows(m):
        zero = jnp.zeros_like(m[0])
        return jnp.stack([jnp.concatenate([m[0], zero], 0), jnp.concatenate([zero, m[1]], 0)]).astype(BF16)

    head = np.arange(RW_WIDTH) // RW_HEAD_DIM
    return dict(
        attn_norm=attn_norm[l][None, :],
        w_in=w_new.astype(BF16),
        na_bias=_na_bias_table(na_rpb[l]),
        rw_conv=jnp.pad(rw_conv[l], ((0, 0), (0, pad_cols))),
        rw_decay0=rw_decay0[l][:, None, :],
        rw_decay2=lora_rows(rw_decay2[l]),
        rw_a0=rw_a0[l][:, None, :],
        rw_a2=lora_rows(rw_a2[l]),
        rw_g2=jnp.pad(rw_g2[l], ((0, GATE_LORA_PAD - GATE_LORA), (0, 0))).astype(BF16),
        rw_k_k=rw_k_k[l][None, :],
        rw_k_a=rw_k_a[l][None, :],
        rw_r_k=rw_r_k[l].reshape(1, RW_WIDTH),
        rw_lnx_w=rw_lnx_w[l][None, :],
        rw_lnx_b=rw_lnx_b[l][None, :],
        bones=jnp.asarray(head[:, None] == head[None, :], BF16),
        mem_norm=mem_norm[l][None, :],
        w_mem_kv=w_mem_kv[l].astype(BF16),
        w_branch=w_branch[l].astype(BF16),
        w_out=w_out[l].astype(BF16),
        ffn_norm=ffn_norm[l][None, :],
        w_up=w_up[l].astype(BF16),
        ffn_conv=ffn_conv[l],
        ffn_conv_b=ffn_conv_b[l][None, :],
        w_down=w_down[l].astype(BF16),
    )


def _layer(x2d, mem, lp, bsz, seq, final_gain, final_norm):
    z = _in_proj(x2d, lp["attn_norm"], lp["w_in"])
    o_na = _na_attention(z, lp["na_bias"], bsz, seq)
    rw_fwd = _rwkv_scan(z, lp, bsz, seq, reverse=False)
    rw_bwd = _rwkv_scan(z, lp, bsz, seq, reverse=True)
    o_mem = _mem_attention(z, _mem_kv(mem, lp["mem_norm"], lp["w_mem_kv"]), bsz, seq)
    x2d = _merge(x2d, z, o_na, rw_fwd, rw_bwd, o_mem, lp)
    return _ffn(x2d, lp, final_gain, seq, final_norm)


def _trunk(x, mem, layers, final_gain):
    bsz, seq, _ = x.shape
    x2d = x.reshape(bsz * seq, D_MODEL)
    for l, lp in enumerate(layers):
        x2d = _layer(x2d, mem, lp, bsz, seq, final_gain, l == len(layers) - 1)
    return x2d.reshape(bsz, seq, D_MODEL)


def kernel(x_prompt, x_sample, mem_prompt, mem_sample, attn_norm, w_in, na_rpb, rw_conv, rw_decay0, rw_decay2, rw_a0, rw_a2, rw_g2, rw_k_k, rw_k_a, rw_r_k, rw_lnx_w, rw_lnx_b, mem_norm, w_mem_kv, w_branch, w_out, ffn_norm, w_up, ffn_conv, ffn_conv_b, w_down, final_norm):
    params = (attn_norm, w_in, na_rpb, rw_conv, rw_decay0, rw_decay2, rw_a0, rw_a2, rw_g2, rw_k_k, rw_k_a,
              rw_r_k, rw_lnx_w, rw_lnx_b, mem_norm, w_mem_kv, w_branch, w_out, ffn_norm, w_up, ffn_conv,
              ffn_conv_b, w_down)
    layers = [_prep_layer(l, *params) for l in range(attn_norm.shape[0])]
    final_gain = final_norm[None, :]
    y_prompt = _trunk(x_prompt, mem_prompt, layers, final_gain)
    y_sample = _trunk(x_sample, mem_sample, layers, final_gain)
    return (y_prompt, y_sample)
```

```python
import functools

import jax
import jax.numpy as jnp
import numpy as np
from jax import lax
from jax.experimental import pallas as pl
from jax.experimental.pallas import tpu as pltpu

F32 = jnp.float32
BF16 = jnp.bfloat16

D_MODEL = 1024
GRID_W = 64
NA_HEADS = 8
NA_HEAD_DIM = 64
NA_WIDTH = 512
NA_WIN_ROWS = 8
NA_WIN_COLS = 16
RW_HEADS = 8
RW_HEAD_DIM = 64
RW_WIDTH = 512
DECAY_LORA = 64
AAA_LORA = 64
GATE_LORA = 160
GATE_LORA_PAD = 256
MEM_TOKENS = 256
MEM_HEADS = 4
MEM_HEAD_DIM = 128
MEM_WIDTH = 512
N_BRANCH = 3
D_FF = 2816
RMS_EPS = 1e-6
GN_EPS = 64e-5

RW_COLS = 3 * RW_WIDTH + 2 * DECAY_LORA + 2 * AAA_LORA + GATE_LORA
RW_COLS_PAD = 3 * RW_WIDTH + 2 * DECAY_LORA + 2 * AAA_LORA + GATE_LORA_PAD
Z_RW = 0
Z_NA = RW_COLS_PAD
Z_MEM = Z_NA + 3 * NA_WIDTH
Z_GATE = Z_MEM + MEM_WIDTH
Z_COLS = Z_GATE + N_BRANCH * D_MODEL

CHUNK = 64
PAIR = 2 * RW_HEAD_DIM
N_PAIRS = RW_HEADS // 2
INV_LEVELS = 6
NEG_BIG = -1e30
VMEM_LIMIT = 56 * 1024 * 1024


def _cparams(sem):
    return pltpu.CompilerParams(dimension_semantics=sem, vmem_limit_bytes=VMEM_LIMIT)


def _rms(x, gain):
    ms = jnp.mean(x * x, axis=-1, keepdims=True)
    return x * lax.rsqrt(ms + RMS_EPS) * gain


def _dot(a, b):
    return jnp.dot(a, b, preferred_element_type=F32)


def _dot_nt(a, b):
    return lax.dot_general(a, b, (((1,), (1,)), ((), ())), preferred_element_type=F32)


def _split_hi_lo(x):
    hi = x.astype(BF16)
    lo = (x - hi.astype(F32)).astype(BF16)
    return hi, lo


def _sel_left(sel, x):
    hi, lo = _split_hi_lo(x)
    return _dot(sel, hi) + _dot(sel, lo)


def _sel_right(x, sel):
    hi, lo = _split_hi_lo(x)
    return _dot(hi, sel) + _dot(lo, sel)


def _inproj_kernel(x_ref, g_ref, w_ref, o_ref, h_ref):
    @pl.when(pl.program_id(1) == 0)
    def _():
        h_ref[...] = _rms(x_ref[...], g_ref[...]).astype(BF16)

    o_ref[...] = _dot(h_ref[...], w_ref[...]).astype(o_ref.dtype)


def _in_proj(x2d, gain, w):
    t, n = x2d.shape[0], w.shape[1]
    tm = min(1024, t)
    tn = 1024
    return pl.pallas_call(
        _inproj_kernel,
        grid=(t // tm, n // tn),
        in_specs=[
            pl.BlockSpec((tm, D_MODEL), lambda i, j: (i, 0)),
            pl.BlockSpec((1, D_MODEL), lambda i, j: (0, 0)),
            pl.BlockSpec((D_MODEL, tn), lambda i, j: (0, j)),
        ],
        out_specs=pl.BlockSpec((tm, tn), lambda i, j: (i, j)),
        out_shape=jax.ShapeDtypeStruct((t, n), BF16),
        scratch_shapes=[pltpu.VMEM((tm, D_MODEL), BF16)],
        compiler_params=_cparams(("parallel", "arbitrary")),
        name="in_proj",
    )(x2d, gain, w)


NA_ROW_BLOCK = 8
NA_BAND_ROWS = NA_ROW_BLOCK + NA_WIN_ROWS - 1


def _na_kernel(q_ref, k_ref, v_ref, t2_ref, o_ref, vl_ref, vr_ref, *, rows):
    half = NA_WIN_ROWS // 2
    band = NA_WIN_ROWS * GRID_W
    i0 = pl.program_id(1) * NA_ROW_BLOCK
    block_start = jnp.clip(i0 - half, 0, rows - NA_BAND_ROWS)
    scale = NA_HEAD_DIM ** -0.5

    lane = lax.broadcasted_iota(jnp.int32, (GRID_W, NA_WIDTH), 1)
    first = jnp.where(lane % PAIR < NA_HEAD_DIM, 1.0, 0.0).astype(BF16)
    second = jnp.where(lane % PAIR < NA_HEAD_DIM, 0.0, 1.0).astype(BF16)
    for r in range(NA_BAND_ROWS):
        v_row = v_ref[r]
        vl_ref[r] = v_row * first
        vr_ref[r] = v_row * second

    def row_step(m, carry):
        i = i0 + m
        r0 = jnp.clip(i - half, 0, rows - NA_WIN_ROWS)
        off = r0 - block_start
        dr0 = (NA_WIN_ROWS - 1) - (i - r0)
        q = q_ref[m] * scale
        q_sel = (q * first, q * second)
        kb = k_ref[pl.ds(off, NA_WIN_ROWS)].reshape(band, NA_WIDTH)
        v_sel = (vl_ref[pl.ds(off, NA_WIN_ROWS)].reshape(band, NA_WIDTH),
                 vr_ref[pl.ds(off, NA_WIN_ROWS)].reshape(band, NA_WIDTH))
        s = []
        for h in range(NA_HEADS):
            ps = slice((h // 2) * PAIR, (h // 2 + 1) * PAIR)
            bias = jnp.concatenate([t2_ref[h, dr0 + 2 * jj] for jj in range(NA_WIN_ROWS // 2)], axis=1)
            s.append(_dot_nt(q_sel[h % 2][:, ps], kb[:, ps]) + bias)
        mx = [jnp.max(x, axis=-1, keepdims=True) for x in s]
        p = [jnp.exp(x - m_) for x, m_ in zip(s, mx)]
        inv = [1.0 / jnp.sum(x, axis=-1, keepdims=True) for x in p]
        outs = []
        for pp in range(NA_HEADS // 2):
            ps = slice(pp * PAIR, (pp + 1) * PAIR)
            o_l = _dot(p[2 * pp].astype(BF16), v_sel[0][:, ps])
            o_r = _dot(p[2 * pp + 1].astype(BF16), v_sel[1][:, ps])
            outs.append(o_l * inv[2 * pp] + o_r * inv[2 * pp + 1])
        o_ref[m] = jnp.concatenate(outs, axis=-1).astype(o_ref.dtype)
        return carry

    lax.fori_loop(0, NA_ROW_BLOCK, row_step, 0)


def _na_bias_table(rpb):
    x = np.arange(GRID_W)[:, None]
    c = np.arange(GRID_W)[None, :]
    c0 = np.clip(x - NA_WIN_COLS // 2, 0, GRID_W - NA_WIN_COLS)
    valid = ((c >= c0) & (c < c0 + NA_WIN_COLS)).reshape(-1)
    dc = np.clip(c - x + (NA_WIN_COLS - 1), 0, 2 * NA_WIN_COLS - 2).reshape(-1)
    onehot = (dc[None, :] == np.arange(2 * NA_WIN_COLS - 1)[:, None]) & valid[None, :]
    tab = jnp.einsum('hrd,dn->hrn', rpb, jnp.asarray(onehot, F32), precision=lax.Precision.HIGHEST)
    tab = tab + jnp.asarray(np.where(valid, 0.0, NEG_BIG), F32)
    tab = tab.reshape(NA_HEADS, 2 * NA_WIN_ROWS - 1, GRID_W, GRID_W)
    return jnp.concatenate([tab[:, :-1], tab[:, 1:]], axis=-1)


def _na_attention(z, bias_tab, bsz, seq):
    rows = seq // GRID_W
    assert rows >= NA_BAND_ROWS and rows % NA_ROW_BLOCK == 0
    z4 = z.reshape(bsz, rows, GRID_W, Z_COLS)
    half = NA_WIN_ROWS // 2

    def kv_spec(col):
        return pl.BlockSpec(
            (pl.Squeezed(), pl.Element(NA_BAND_ROWS), pl.Element(GRID_W), pl.Element(NA_WIDTH)),
            lambda b, i: (b, jnp.clip(i * NA_ROW_BLOCK - half, 0, rows - NA_BAND_ROWS), 0, col))

    qblk = Z_NA // NA_WIDTH
    out = pl.pallas_call(
        functools.partial(_na_kernel, rows=rows),
        grid=(bsz, rows // NA_ROW_BLOCK),
        in_specs=[
            pl.BlockSpec((None, NA_ROW_BLOCK, GRID_W, NA_WIDTH), lambda b, i: (b, i, 0, qblk)),
            kv_spec(Z_NA + NA_WIDTH),
            kv_spec(Z_NA + 2 * NA_WIDTH),
            pl.BlockSpec(bias_tab.shape, lambda b, i: (0, 0, 0, 0)),
        ],
        out_specs=pl.BlockSpec((None, NA_ROW_BLOCK, GRID_W, NA_WIDTH), lambda b, i: (b, i, 0, 0)),
        out_shape=jax.ShapeDtypeStruct((bsz, rows, GRID_W, NA_WIDTH), BF16),
        scratch_shapes=[pltpu.VMEM((NA_BAND_ROWS, GRID_W, NA_WIDTH), BF16)] * 2,
        compiler_params=_cparams(("parallel", "arbitrary")),
        name="na_attn",
    )(z4, z4, z4, bias_tab)
    return out.reshape(bsz * seq, NA_WIDTH)


def _softplus(u):
    return jnp.maximum(u, 0.0) + jnp.log1p(jnp.exp(-jnp.abs(u)))


def _rwkv_kernel(zc_ref, zp_ref, zn_ref, cw_ref, d0_ref, d2_ref, a0_ref, a2_ref, g2_ref,
                 kk_ref, ka_ref, rk_ref, bones_ref, tri_ref,
                 y_ref, bonus_ref, gate_ref, s_ref, *bufs, tile, nt, reverse):
    i = pl.program_id(1)
    ti = nt - 1 - i if reverse else i

    @pl.when(i == 0)
    def _():
        s_ref[...] = jnp.zeros_like(s_ref)

    _rwkv_prep_tile(zc_ref, zp_ref, zn_ref, cw_ref, d0_ref, d2_ref, a0_ref, a2_ref, g2_ref,
                    kk_ref, ka_ref, rk_ref, bones_ref, tri_ref, bonus_ref, gate_ref, *bufs,
                    tile=tile, nt=nt, ti=ti, reverse=reverse)
    _rwkv_scan_tile(y_ref, s_ref, *bufs, tile=tile, reverse=reverse)


def _rwkv_prep_tile(zc_ref, zp_ref, zn_ref, cw_ref, d0_ref, d2_ref, a0_ref, a2_ref, g2_ref,
                    kk_ref, ka_ref, rk_ref, bones_ref, tri_ref, bonus_ref, gate_ref,
                    rt_ref, at_ref, bt_ref, kt_ref, bh_ref, kh_ref, v_ref, tot_ref,
                    *, tile, nt, ti, reverse):
    n_chunks = tile // CHUNK

    zc = zc_ref[...].astype(F32)
    prev_row = jnp.where(ti == 0, 0.0, zp_ref[...].astype(F32)[7:8, :])
    next_row = jnp.where(ti == nt - 1, 0.0, zn_ref[...].astype(F32)[0:1, :])
    zf = (pltpu.roll(zc, 1, axis=0) * cw_ref[0:1, :] + zc * cw_ref[1:2, :]
          + pltpu.roll(zc, tile - 1, axis=0) * cw_ref[2:3, :])
    rows8 = lax.broadcasted_iota(jnp.int32, (8, 1), 0)
    fix_first = jnp.where(rows8 == 0, (prev_row - zc[tile - 1:tile, :]) * cw_ref[0:1, :], 0.0)
    fix_last = jnp.where(rows8 == 7, (next_row - zc[0:1, :]) * cw_ref[2:3, :], 0.0)
    zf = jnp.concatenate([zf[0:8] + fix_first, zf[8:tile - 8], zf[tile - 8:tile] + fix_last], axis=0)

    o1, o2, o3 = RW_WIDTH, 2 * RW_WIDTH, 3 * RW_WIDTH
    o4 = o3 + 2 * DECAY_LORA
    o5 = o4 + 2 * AAA_LORA
    r = zf[:, 0:o1]
    k = zf[:, o1:o2]
    v = zf[:, o2:o3]
    xw = zf[:, o3:o4]
    xa = zf[:, o4:o5]
    xg = zf[:, o5:RW_COLS_PAD]

    bones = bones_ref[...]
    kk = k * kk_ref[...]
    kk = kk * lax.rsqrt(jnp.maximum(_dot((kk * kk).astype(BF16), bones), 1e-24))
    gate_ref[...] = _dot(jax.nn.sigmoid(xg).astype(BF16), g2_ref[...]).astype(gate_ref.dtype)

    w_log = -_softplus(-(d0_ref[...] + _dot(jnp.tanh(xw).astype(BF16), d2_ref[...]))) - 0.5
    logw = -jnp.exp(w_log)
    a = jax.nn.sigmoid(a0_ref[...] + _dot(xa.astype(BF16), a2_ref[...]))
    k_d = k * (1.0 + (a - 1.0) * ka_ref[...])
    b = kk * a
    bonus_ref[...] = (_dot((r * k_d * rk_ref[...]).astype(BF16), bones) * v).astype(bonus_ref.dtype)

    span = tri_ref.shape[0]
    g = jnp.concatenate([_sel_left(tri_ref[...], logw[r:r + span]) for r in range(0, tile, span)], axis=0)
    last = 0 if reverse else CHUNK - 1
    tot = jnp.concatenate(
        [jnp.broadcast_to(g[c * CHUNK + last:c * CHUNK + last + 1, :], (CHUNK, RW_WIDTH))
         for c in range(n_chunks)], axis=0)
    eng = jnp.exp(-g)
    ehat = jnp.exp(tot - g)
    rt_ref[...] = r * jnp.exp(g)
    at_ref[...] = -kk * jnp.exp(g - logw)
    bt_ref[...] = b * eng
    kt_ref[...] = k_d * eng
    bh_ref[...] = b * ehat
    kh_ref[...] = k_d * ehat
    v_ref[...] = v
    tot_ref[...] = tot


def _rwkv_scan_tile(y_ref, s_ref, rt_ref, at_ref, bt_ref, kt_ref, bh_ref, kh_ref, v_ref, tot_ref,
                    *, tile, reverse):
    n_chunks = tile // CHUNK

    ri = lax.broadcasted_iota(jnp.int32, (PAIR, PAIR), 0)
    ci = lax.broadcasted_iota(jnp.int32, (PAIR, PAIR), 1)
    same_head = jnp.where(jnp.right_shift(ri, 6) == jnp.right_shift(ci, 6), 1.0, 0.0).astype(F32)
    order = ci - ri if reverse else ri - ci
    m_strict = jnp.where(order > 0, same_head, 0.0)
    m_incl = jnp.where(order >= 0, same_head, 0.0)
    eye = jnp.where(ci == ri, 1.0, 0.0).astype(F32)
    lvl = []
    for q in range(INV_LEVELS):
        in_block = jnp.where(jnp.right_shift(ri, q + 1) == jnp.right_shift(ci, q + 1), 1.0, 0.0).astype(F32)
        lvl.append(jnp.where(jnp.right_shift(ri, q) != jnp.right_shift(ci, q), in_block, 0.0))
    lane = lax.broadcasted_iota(jnp.int32, (CHUNK, PAIR), 1)
    left = lane < RW_HEAD_DIM

    def stack(x):
        return jnp.concatenate([jnp.where(left, x, 0.0), jnp.where(left, 0.0, x)], axis=0)

    groups = [(c, p) for c in range(n_chunks) for p in range(N_PAIRS)]
    eye_b = eye.astype(BF16)

    def tile_of(ref, c, p):
        return ref[c * CHUNK:(c + 1) * CHUNK, p * PAIR:(p + 1) * PAIR]

    at_b = [stack(tile_of(at_ref, c, p)).astype(BF16) for c, p in groups]
    vs_b = [stack(tile_of(v_ref, c, p)).astype(BF16) for c, p in groups]
    o1_ = []
    for gi, (c, p) in enumerate(groups):
        bt = tile_of(bt_ref, c, p).astype(BF16)
        kt = tile_of(kt_ref, c, p).astype(BF16)
        lhs1 = jnp.concatenate([at_b[gi], stack(tile_of(rt_ref, c, p)).astype(BF16)], axis=0)
        o1_.append(_dot_nt(lhs1, jnp.concatenate([bt, bt, kt, kt], axis=0)))
    ms_b = m_strict.astype(BF16)
    mi_b = m_incl.astype(BF16)
    lvl_b = [m.astype(BF16) for m in lvl]
    n_b = [o[0:PAIR, 0:PAIR].astype(BF16) * ms_b for o in o1_]
    a_ak = [o[0:PAIR, PAIR:].astype(BF16) * ms_b for o in o1_]
    a_r = [jnp.concatenate([o[PAIR:, 0:PAIR].astype(BF16) * mi_b, o[PAIR:, PAIR:].astype(BF16) * mi_b], axis=1)
           for o in o1_]

    t_inv = [eye.astype(BF16) + n * lvl_b[0] for n in n_b]
    for q in range(1, INV_LEVELS):
        blk = 1 << q
        if blk < 16:
            xq = [_dot(n * lvl_b[q], t) for n, t in zip(n_b, t_inv)]
            t_inv = [t + _dot(t, x.astype(BF16)).astype(BF16) for t, x in zip(t_inv, xq)]
            continue
        n_blocks = PAIR // blk
        late = [k for k in range(n_blocks) if (k % 2 == 0) == reverse]

        def take(x):
            return jnp.concatenate([x[k * blk:(k + 1) * blk] for k in late], axis=0)

        def spread(x):
            zero = jnp.zeros((blk, PAIR), x.dtype)
            return jnp.concatenate(
                [x[late.index(k) * blk:(late.index(k) + 1) * blk] if k in late else zero
                 for k in range(n_blocks)], axis=0)

        xq = [_dot(take(n) * take(lvl_b[q]), t) for n, t in zip(n_b, t_inv)]
        upd = [_dot(take(t), spread(x.astype(BF16))) for t, x in zip(t_inv, xq)]
        t_inv = [t + spread(u.astype(BF16)) for t, u in zip(t_inv, upd)]

    av = [_dot(a, v_) for a, v_ in zip(a_ak, vs_b)]
    pu = [_dot(t, jnp.concatenate([a, x.astype(BF16)], axis=1))
          for t, a, x in zip(t_inv, at_b, av)]
    zero_b = jnp.zeros((PAIR, PAIR), BF16)
    qb, y_loc, n_st = [], [], []
    for gi, (c, p) in enumerate(groups):
        bh_t = stack(tile_of(bh_ref, c, p)).T
        kh_t = stack(tile_of(kh_ref, c, p)).T
        wmat = jnp.concatenate(
            [pu[gi].astype(BF16), jnp.concatenate([zero_b, vs_b[gi]], axis=1)], axis=0)
        lhs2 = jnp.concatenate([a_r[gi], jnp.concatenate([bh_t, kh_t], axis=1).astype(BF16)], axis=0)
        o2_ = _dot(lhs2, wmat)
        q_mat = stack(tile_of(rt_ref, c, p)) + o2_[0:PAIR, 0:PAIR]
        qb.append(jnp.concatenate([q_mat, o2_[PAIR:, 0:PAIR]], axis=0).astype(BF16))
        y_loc.append(o2_[0:PAIR, PAIR:])
        n_st.append(o2_[PAIR:, PAIR:])

    rep = PAIR // n_chunks
    w_tot = []
    for p in range(N_PAIRS):
        t_rows = jnp.concatenate(
            [tot_ref[c * CHUNK:c * CHUNK + rep, p * PAIR:(p + 1) * PAIR] for c in range(n_chunks)], axis=0)
        hi = t_rows.astype(BF16)
        r1 = t_rows - hi.astype(F32)
        mid = r1.astype(BF16)
        lo = (r1 - mid.astype(F32)).astype(BF16)
        cols = _dot_nt(eye_b, hi) + _dot_nt(eye_b, mid) + _dot_nt(eye_b, lo)
        w_tot.append([jnp.exp(cols[:, c * rep:c * rep + 1]) for c in range(n_chunks)])

    st = [s_ref[p] for p in range(N_PAIRS)]
    for c in (reversed(range(n_chunks)) if reverse else range(n_chunks)):
        o3_ = [_dot(qb[c * N_PAIRS + p], st[p].astype(BF16)) for p in range(N_PAIRS)]
        for p in range(N_PAIRS):
            gi = c * N_PAIRS + p
            y_bs = o3_[p][0:PAIR] + y_loc[gi]
            st[p] = st[p] * w_tot[p][c] + o3_[p][PAIR:] + n_st[gi]
            y_ref[c * CHUNK:(c + 1) * CHUNK, p * PAIR:(p + 1) * PAIR] = y_bs[0:CHUNK] + y_bs[CHUNK:]
    for p in range(N_PAIRS):
        s_ref[p] = st[p]


def _rwkv_scan(z, lp, bsz, seq, reverse):
    tile = min(512, seq)
    nt = seq // tile
    z3 = z.reshape(bsz, seq, Z_COLS)
    t8 = tile // 8
    d = int(reverse)

    def tidx(i):
        return nt - 1 - i if reverse else i

    span = min(256, tile)
    cidx = np.arange(span)
    same = (cidx[:, None] // CHUNK) == (cidx[None, :] // CHUNK)
    upto = cidx[None, :] >= cidx[:, None] if reverse else cidx[None, :] <= cidx[:, None]
    tri = jnp.asarray(same & upto, BF16)

    def full(shape):
        return pl.BlockSpec(shape, lambda b, i: (0,) * len(shape))

    def per_dir(shape):
        return pl.BlockSpec((None,) + shape, lambda b, i: (d,) + (0,) * len(shape))

    out_spec = pl.BlockSpec((None, tile, RW_WIDTH), lambda b, i: (b, tidx(i), 0))
    nat = pltpu.VMEM((tile, RW_WIDTH), F32)
    return pl.pallas_call(
        functools.partial(_rwkv_kernel, tile=tile, nt=nt, reverse=reverse),
        grid=(bsz, nt),
        in_specs=[
            pl.BlockSpec((None, tile, RW_COLS_PAD), lambda b, i: (b, tidx(i), 0)),
            pl.BlockSpec((None, 8, RW_COLS_PAD), lambda b, i: (b, jnp.maximum(tidx(i) * t8 - 1, 0), 0)),
            pl.BlockSpec((None, 8, RW_COLS_PAD),
                         lambda b, i: (b, jnp.minimum((tidx(i) + 1) * t8, seq // 8 - 1), 0)),
            full((3, RW_COLS_PAD)),
            per_dir((1, RW_WIDTH)), per_dir((2 * DECAY_LORA, RW_WIDTH)),
            per_dir((1, RW_WIDTH)), per_dir((2 * AAA_LORA, RW_WIDTH)),
            full((GATE_LORA_PAD, RW_WIDTH)),
            full((1, RW_WIDTH)), full((1, RW_WIDTH)), full((1, RW_WIDTH)),
            full((RW_WIDTH, RW_WIDTH)),
            full((span, span)),
        ],
        out_specs=[out_spec, out_spec, out_spec],
        out_shape=[jax.ShapeDtypeStruct((bsz, seq, RW_WIDTH), F32),
                   jax.ShapeDtypeStruct((bsz, seq, RW_WIDTH), BF16),
                   jax.ShapeDtypeStruct((bsz, seq, RW_WIDTH), BF16)],
        scratch_shapes=[pltpu.VMEM((N_PAIRS, PAIR, PAIR), F32)] + [nat] * 8,
        compiler_params=_cparams(("arbitrary", "arbitrary")),
        name="rwkv_bwd" if reverse else "rwkv_fwd",
    )(z3, z3, z3, lp["rw_conv"], lp["rw_decay0"], lp["rw_decay2"], lp["rw_a0"], lp["rw_a2"],
      lp["rw_g2"], lp["rw_k_k"], lp["rw_k_a"], lp["rw_r_k"], lp["bones"], tri)


def _memkv_kernel(m_ref, g_ref, w_ref, o_ref):
    o_ref[...] = _dot(_rms(m_ref[...], g_ref[...]).astype(BF16), w_ref[...]).astype(o_ref.dtype)


def _mem_kv(mem, gain, w):
    bsz = mem.shape[0]
    return pl.pallas_call(
        _memkv_kernel,
        grid=(bsz,),
        in_specs=[
            pl.BlockSpec((None, MEM_TOKENS, D_MODEL), lambda b: (b, 0, 0)),
            pl.BlockSpec((1, D_MODEL), lambda b: (0, 0)),
            pl.BlockSpec((D_MODEL, 2 * MEM_WIDTH), lambda b: (0, 0)),
        ],
        out_specs=pl.BlockSpec((None, MEM_TOKENS, 2 * MEM_WIDTH), lambda b: (b, 0, 0)),
        out_shape=jax.ShapeDtypeStruct((bsz, MEM_TOKENS, 2 * MEM_WIDTH), BF16),
        compiler_params=_cparams(("parallel",)),
        name="mem_kv",
    )(mem, gain, w)


def _memattn_kernel(q_ref, kv_ref, o_ref):
    scale = MEM_HEAD_DIM ** -0.5
    outs = []
    for h in range(MEM_HEADS):
        sl = slice(h * MEM_HEAD_DIM, (h + 1) * MEM_HEAD_DIM)
        vsl = slice(MEM_WIDTH + h * MEM_HEAD_DIM, MEM_WIDTH + (h + 1) * MEM_HEAD_DIM)
        s = _dot_nt(q_ref[:, sl], kv_ref[:, sl]) * scale
        m = jnp.max(s, axis=-1, keepdims=True)
        p = jnp.exp(s - m)
        l = jnp.sum(p, axis=-1, keepdims=True)
        outs.append(_dot(p.astype(BF16), kv_ref[:, vsl]) / l)
    o_ref[...] = jnp.concatenate(outs, axis=-1).astype(o_ref.dtype)


def _mem_attention(z, kv, bsz, seq):
    tm = min(1024, seq)
    z3 = z.reshape(bsz, seq, Z_COLS)
    out = pl.pallas_call(
        _memattn_kernel,
        grid=(bsz, seq // tm),
        in_specs=[
            pl.BlockSpec((None, tm, MEM_WIDTH), lambda b, i: (b, i, Z_MEM // MEM_WIDTH)),
            pl.BlockSpec((None, MEM_TOKENS, 2 * MEM_WIDTH), lambda b, i: (b, 0, 0)),
        ],
        out_specs=pl.BlockSpec((None, tm, MEM_WIDTH), lambda b, i: (b, i, 0)),
        out_shape=jax.ShapeDtypeStruct((bsz, seq, MEM_WIDTH), BF16),
        compiler_params=_cparams(("parallel", "parallel")),
        name="mem_attn",
    )(z3, kv)
    return out.reshape(bsz * seq, MEM_WIDTH)


def _merge_kernel(x_ref, ona_ref, y0_ref, y1_ref, b0_ref, b1_ref, gt_ref, omem_ref,
                  g0_ref, g1_ref, g2_ref, wb_ref, wo_ref, lw_ref, lb_ref, bones_ref, o_ref):
    bones = bones_ref[...]
    inv_n = 1.0 / RW_HEAD_DIM
    y = y0_ref[...] + y1_ref[...]
    mu = _dot(y.astype(BF16), bones) * inv_n
    dl = y - mu
    var = _dot((dl * dl).astype(BF16), bones) * inv_n
    yn = dl * lax.rsqrt(var + GN_EPS)
    bonus = b0_ref[...].astype(F32) + b1_ref[...].astype(F32)
    o_rw = (yn * lw_ref[...] + lb_ref[...] + bonus) * gt_ref[...].astype(F32)

    merged = jax.nn.sigmoid(g0_ref[...].astype(F32)) * _dot(ona_ref[...], wb_ref[0])
    merged = merged + jax.nn.sigmoid(g1_ref[...].astype(F32)) * _dot(o_rw.astype(BF16), wb_ref[1])
    merged = merged + jax.nn.sigmoid(g2_ref[...].astype(F32)) * _dot(omem_ref[...], wb_ref[2])
    o_ref[...] = x_ref[...] + _dot(merged.astype(BF16), wo_ref[...])


def _merge(x2d, z, o_na, rw_fwd, rw_bwd, o_mem, lp):
    t = x2d.shape[0]
    tm = min(512, t)
    y0, b0, gate = (a.reshape(t, RW_WIDTH) for a in rw_fwd)
    y1, b1, _ = (a.reshape(t, RW_WIDTH) for a in rw_bwd)
    gblk = Z_GATE // D_MODEL

    def tok(width):
        return pl.BlockSpec((tm, width), lambda i: (i, 0))

    def const(shape):
        return pl.BlockSpec(shape, lambda i: (0,) * len(shape))

    return pl.pallas_call(
        _merge_kernel,
        grid=(t // tm,),
        in_specs=[
            tok(D_MODEL), tok(NA_WIDTH), tok(RW_WIDTH), tok(RW_WIDTH), tok(RW_WIDTH), tok(RW_WIDTH), tok(RW_WIDTH),
            tok(MEM_WIDTH),
            pl.BlockSpec((tm, D_MODEL), lambda i: (i, gblk)),
            pl.BlockSpec((tm, D_MODEL), lambda i: (i, gblk + 1)),
            pl.BlockSpec((tm, D_MODEL), lambda i: (i, gblk + 2)),
            const((N_BRANCH, NA_WIDTH, D_MODEL)), const((D_MODEL, D_MODEL)),
            const((1, RW_WIDTH)), const((1, RW_WIDTH)), const((RW_WIDTH, RW_WIDTH)),
        ],
        out_specs=tok(D_MODEL),
        out_shape=jax.ShapeDtypeStruct((t, D_MODEL), F32),
        compiler_params=_cparams(("parallel",)),
        name="merge",
    )(x2d, o_na, y0, y1, b0, b1, gate, o_mem, z, z, z, lp["w_branch"], lp["w_out"],
      lp["rw_lnx_w"], lp["rw_lnx_b"], lp["bones"])


def _ffn_kernel(x_ref, xp_ref, xn_ref, g_ref, wv_ref, wg_ref, cv_ref, cg_ref, bv_ref, bg_ref, wd_ref,
                fg_ref, o_ref, h_ref, acc_ref, *, tm, tiles_per_seq, final_norm):
    i = pl.program_id(0)
    j = pl.program_id(1)
    ext = tm + 16

    @pl.when(j == 0)
    def _():
        gain = g_ref[...]
        seq_pos = i % tiles_per_seq
        h_ref[0:8, :] = jnp.where(seq_pos == 0, 0.0, _rms(xp_ref[...], gain)).astype(BF16)
        h_ref[8:8 + tm, :] = _rms(x_ref[...], gain).astype(BF16)
        h_ref[8 + tm:ext, :] = jnp.where(seq_pos == tiles_per_seq - 1, 0.0, _rms(xn_ref[...], gain)).astype(BF16)
        acc_ref[...] = jnp.zeros_like(acc_ref)

    hx = h_ref[...]

    def conv(u, cw, bias):
        return (pltpu.roll(u, 1, axis=0)[8:8 + tm] * cw[0:1, :] + u[8:8 + tm] * cw[1:2, :]
                + pltpu.roll(u, ext - 1, axis=0)[8:8 + tm] * cw[2:3, :] + bias)

    u_val = conv(_dot(hx, wv_ref[...]), cv_ref[...], bv_ref[...])
    u_gate = conv(_dot(hx, wg_ref[...]), cg_ref[...], bg_ref[...])
    act = (u_gate * jax.nn.sigmoid(u_gate) * u_val).astype(BF16)
    acc_ref[...] += _dot(act, wd_ref[...])

    @pl.when(j == pl.num_programs(1) - 1)
    def _():
        out = x_ref[...] + acc_ref[...]
        if final_norm:
            out = _rms(out, fg_ref[...])
        o_ref[...] = out


def _ffn(x2d, lp, final_gain, seq, final_norm):
    t = x2d.shape[0]
    tm = min(1024, seq)
    tn = 256
    nj = D_FF // tn
    t8 = tm // 8
    return pl.pallas_call(
        functools.partial(_ffn_kernel, tm=tm, tiles_per_seq=seq // tm, final_norm=final_norm),
        grid=(t // tm, nj),
        in_specs=[
            pl.BlockSpec((tm, D_MODEL), lambda i, j: (i, 0)),
            pl.BlockSpec((8, D_MODEL), lambda i, j: (jnp.maximum(i * t8 - 1, 0), 0)),
            pl.BlockSpec((8, D_MODEL), lambda i, j: (jnp.minimum((i + 1) * t8, t // 8 - 1), 0)),
            pl.BlockSpec((1, D_MODEL), lambda i, j: (0, 0)),
            pl.BlockSpec((D_MODEL, tn), lambda i, j: (0, j)),
            pl.BlockSpec((D_MODEL, tn), lambda i, j: (0, nj + j)),
            pl.BlockSpec((3, tn), lambda i, j: (0, j)),
            pl.BlockSpec((3, tn), lambda i, j: (0, nj + j)),
            pl.BlockSpec((1, tn), lambda i, j: (0, j)),
            pl.BlockSpec((1, tn), lambda i, j: (0, nj + j)),
            pl.BlockSpec((tn, D_MODEL), lambda i, j: (j, 0)),
            pl.BlockSpec((1, D_MODEL), lambda i, j: (0, 0)),
        ],
        out_specs=pl.BlockSpec((tm, D_MODEL), lambda i, j: (i, 0)),
        out_shape=jax.ShapeDtypeStruct((t, D_MODEL), F32),
        scratch_shapes=[pltpu.VMEM((tm + 16, D_MODEL), BF16), pltpu.VMEM((tm, D_MODEL), F32)],
        compiler_params=_cparams(("parallel", "arbitrary")),
        name="ffn",
    )(x2d, x2d, x2d, lp["ffn_norm"], lp["w_up"], lp["w_up"], lp["ffn_conv"], lp["ffn_conv"],
      lp["ffn_conv_b"], lp["ffn_conv_b"], lp["w_down"], final_gain)


def _prep_layer(l, attn_norm, w_in, na_rpb, rw_conv, rw_decay0, rw_decay2, rw_a0, rw_a2, rw_g2, rw_k_k,
                rw_k_a, rw_r_k, rw_lnx_w, rw_lnx_b, mem_norm, w_mem_kv, w_branch, w_out, ffn_norm, w_up,
                ffn_conv, ffn_conv_b, w_down):
    c1 = 3 * NA_WIDTH
    c2 = c1 + RW_COLS
    w = w_in[l]
    pad_cols = RW_COLS_PAD - RW_COLS
    w_new = jnp.concatenate(
        [w[:, c1:c2], jnp.zeros((D_MODEL, pad_cols), w.dtype), w[:, :c1], w[:, c2:]], axis=1)

    def lora_rows(m):
        zero = jnp.zeros_like(m[0])
        return jnp.stack([jnp.concatenate([m[0], zero], 0), jnp.concatenate([zero, m[1]], 0)]).astype(BF16)

    head = np.arange(RW_WIDTH) // RW_HEAD_DIM
    return dict(
        attn_norm=attn_norm[l][None, :],
        w_in=w_new.astype(BF16),
        na_bias=_na_bias_table(na_rpb[l]),
        rw_conv=jnp.pad(rw_conv[l], ((0, 0), (0, pad_cols))),
        rw_decay0=rw_decay0[l][:, None, :],
        rw_decay2=lora_rows(rw_decay2[l]),
        rw_a0=rw_a0[l][:, None, :],
        rw_a2=lora_rows(rw_a2[l]),
        rw_g2=jnp.pad(rw_g2[l], ((0, GATE_LORA_PAD - GATE_LORA), (0, 0))).astype(BF16),
        rw_k_k=rw_k_k[l][None, :],
        rw_k_a=rw_k_a[l][None, :],
        rw_r_k=rw_r_k[l].reshape(1, RW_WIDTH),
        rw_lnx_w=rw_lnx_w[l][None, :],
        rw_lnx_b=rw_lnx_b[l][None, :],
        bones=jnp.asarray(head[:, None] == head[None, :], BF16),
        mem_norm=mem_norm[l][None, :],
        w_mem_kv=w_mem_kv[l].astype(BF16),
        w_branch=w_branch[l].astype(BF16),
        w_out=w_out[l].astype(BF16),
        ffn_norm=ffn_norm[l][None, :],
        w_up=w_up[l].astype(BF16),
        ffn_conv=ffn_conv[l],
        ffn_conv_b=ffn_conv_b[l][None, :],
        w_down=w_down[l].astype(BF16),
    )


def _layer(x2d, mem, lp, bsz, seq, final_gain, final_norm):
    z = _in_proj(x2d, lp["attn_norm"], lp["w_in"])
    o_na = _na_attention(z, lp["na_bias"], bsz, seq)
    rw_fwd = _rwkv_scan(z, lp, bsz, seq, reverse=False)
    rw_bwd = _rwkv_scan(z, lp, bsz, seq, reverse=True)
    o_mem = _mem_attention(z, _mem_kv(mem, lp["mem_norm"], lp["w_mem_kv"]), bsz, seq)
    x2d = _merge(x2d, z, o_na, rw_fwd, rw_bwd, o_mem, lp)
    return _ffn(x2d, lp, final_gain, seq, final_norm)


def _trunk(x, mem, layers, final_gain):
    bsz, seq, _ = x.shape
    x2d = x.reshape(bsz * seq, D_MODEL)
    for l, lp in enumerate(layers):
        x2d = _layer(x2d, mem, lp, bsz, seq, final_gain, l == len(layers) - 1)
    return x2d.reshape(bsz, seq, D_MODEL)


def kernel(x_prompt, x_sample, mem_prompt, mem_sample, attn_norm, w_in, na_rpb, rw_conv, rw_decay0, rw_decay2, rw_a0, rw_a2, rw_g2, rw_k_k, rw_k_a, rw_r_k, rw_lnx_w, rw_lnx_b, mem_norm, w_mem_kv, w_branch, w_out, ffn_norm, w_up, ffn_conv, ffn_conv_b, w_down, final_norm):
    params = (attn_norm, w_in, na_rpb, rw_conv, rw_decay0, rw_decay2, rw_a0, rw_a2, rw_g2, rw_k_k, rw_k_a,
              rw_r_k, rw_lnx_w, rw_lnx_b, mem_norm, w_mem_kv, w_branch, w_out, ffn_norm, w_up, ffn_conv,
              ffn_conv_b, w_down)
    layers = [_prep_layer(l, *params) for l in range(attn_norm.shape[0])]
    final_gain = final_norm[None, :]
    y_prompt = _trunk(x_prompt, mem_prompt, layers, final_gain)
    y_sample = _trunk(x_sample, mem_sample, layers, final_gain)
    return (y_prompt, y_sample)
```

```python
import functools

import jax
import jax.numpy as jnp
import numpy as np
from jax import lax
from jax.experimental import pallas as pl
from jax.experimental.pallas import tpu as pltpu

F32 = jnp.float32
BF16 = jnp.bfloat16

D_MODEL = 1024
GRID_W = 64
NA_HEADS = 8
NA_HEAD_DIM = 64
NA_WIDTH = 512
NA_WIN_ROWS = 8
NA_WIN_COLS = 16
RW_HEADS = 8
RW_HEAD_DIM = 64
RW_WIDTH = 512
DECAY_LORA = 64
AAA_LORA = 64
GATE_LORA = 160
GATE_LORA_PAD = 256
MEM_TOKENS = 256
MEM_HEADS = 4
MEM_HEAD_DIM = 128
MEM_WIDTH = 512
N_BRANCH = 3
D_FF = 2816
RMS_EPS = 1e-6
GN_EPS = 64e-5

RW_COLS = 3 * RW_WIDTH + 2 * DECAY_LORA + 2 * AAA_LORA + GATE_LORA
RW_COLS_PAD = 3 * RW_WIDTH + 2 * DECAY_LORA + 2 * AAA_LORA + GATE_LORA_PAD
Z_RW = 0
Z_NA = RW_COLS_PAD
Z_MEM = Z_NA + 3 * NA_WIDTH
Z_GATE = Z_MEM + MEM_WIDTH
Z_COLS = Z_GATE + N_BRANCH * D_MODEL

CHUNK = 64
PAIR = 2 * RW_HEAD_DIM
N_PAIRS = RW_HEADS // 2
INV_LEVELS = 6
NEG_BIG = -1e30
VMEM_LIMIT = 56 * 1024 * 1024


def _cparams(sem):
    return pltpu.CompilerParams(dimension_semantics=sem, vmem_limit_bytes=VMEM_LIMIT)


def _rms(x, gain):
    ms = jnp.mean(x * x, axis=-1, keepdims=True)
    return x * lax.rsqrt(ms + RMS_EPS) * gain


def _dot(a, b):
    return jnp.dot(a, b, preferred_element_type=F32)


def _dot_nt(a, b):
    return lax.dot_general(a, b, (((1,), (1,)), ((), ())), preferred_element_type=F32)


def _split_hi_lo(x):
    hi = x.astype(BF16)
    lo = (x - hi.astype(F32)).astype(BF16)
    return hi, lo


def _sel_left(sel, x):
    hi, lo = _split_hi_lo(x)
    return _dot(sel, hi) + _dot(sel, lo)


def _sel_right(x, sel):
    hi, lo = _split_hi_lo(x)
    return _dot(hi, sel) + _dot(lo, sel)


def _inproj_kernel(x_ref, g_ref, w_ref, o_ref, h_ref):
    @pl.when(pl.program_id(1) == 0)
    def _():
        h_ref[...] = _rms(x_ref[...], g_ref[...]).astype(BF16)

    o_ref[...] = _dot(h_ref[...], w_ref[...]).astype(o_ref.dtype)


def _in_proj(x2d, gain, w):
    t, n = x2d.shape[0], w.shape[1]
    tm = min(2048, t)
    tn = 1024
    return pl.pallas_call(
        _inproj_kernel,
        grid=(t // tm, n // tn),
        in_specs=[
            pl.BlockSpec((tm, D_MODEL), lambda i, j: (i, 0)),
            pl.BlockSpec((1, D_MODEL), lambda i, j: (0, 0)),
            pl.BlockSpec((D_MODEL, tn), lambda i, j: (0, j)),
        ],
        out_specs=pl.BlockSpec((tm, tn), lambda i, j: (i, j)),
        out_shape=jax.ShapeDtypeStruct((t, n), BF16),
        scratch_shapes=[pltpu.VMEM((tm, D_MODEL), BF16)],
        compiler_params=_cparams(("parallel", "arbitrary")),
        name="in_proj",
    )(x2d, gain, w)


NA_ROW_BLOCK = 8
NA_BAND_ROWS = NA_ROW_BLOCK + NA_WIN_ROWS - 1


def _na_kernel(q_ref, k_ref, v_ref, t2_ref, o_ref, vl_ref, vr_ref, *, rows):
    half = NA_WIN_ROWS // 2
    band = NA_WIN_ROWS * GRID_W
    i0 = pl.program_id(1) * NA_ROW_BLOCK
    block_start = jnp.clip(i0 - half, 0, rows - NA_BAND_ROWS)
    scale = NA_HEAD_DIM ** -0.5

    lane = lax.broadcasted_iota(jnp.int32, (GRID_W, NA_WIDTH), 1)
    first = jnp.where(lane % PAIR < NA_HEAD_DIM, 1.0, 0.0).astype(BF16)
    second = jnp.where(lane % PAIR < NA_HEAD_DIM, 0.0, 1.0).astype(BF16)
    for r in range(NA_BAND_ROWS):
        v_row = v_ref[r]
        vl_ref[r] = v_row * first
        vr_ref[r] = v_row * second

    rows_per_iter = 4

    def rows_step(it, carry):
        s, v_sel = [], []
        for u in range(rows_per_iter):
            m = it * rows_per_iter + u
            i = i0 + m
            r0 = jnp.clip(i - half, 0, rows - NA_WIN_ROWS)
            off = r0 - block_start
            dr0 = (NA_WIN_ROWS - 1) - (i - r0)
            q = q_ref[m] * scale
            q_sel = (q * first, q * second)
            kb = k_ref[pl.ds(off, NA_WIN_ROWS)].reshape(band, NA_WIDTH)
            v_sel.append((vl_ref[pl.ds(off, NA_WIN_ROWS)].reshape(band, NA_WIDTH),
                          vr_ref[pl.ds(off, NA_WIN_ROWS)].reshape(band, NA_WIDTH)))
            for h in range(NA_HEADS):
                ps = slice((h // 2) * PAIR, (h // 2 + 1) * PAIR)
                bias = jnp.concatenate([t2_ref[h, dr0 + 2 * jj] for jj in range(NA_WIN_ROWS // 2)], axis=1)
                s.append(_dot_nt(q_sel[h % 2][:, ps], kb[:, ps]) + bias)
        mx = [jnp.max(x, axis=-1, keepdims=True) for x in s]
        p = [jnp.exp(x - m_) for x, m_ in zip(s, mx)]
        inv = [1.0 / jnp.sum(x, axis=-1, keepdims=True) for x in p]
        for u in range(rows_per_iter):
            outs = []
            for pp in range(NA_HEADS // 2):
                ps = slice(pp * PAIR, (pp + 1) * PAIR)
                e = u * NA_HEADS + 2 * pp
                o_l = _dot(p[e].astype(BF16), v_sel[u][0][:, ps])
                o_r = _dot(p[e + 1].astype(BF16), v_sel[u][1][:, ps])
                outs.append(o_l * inv[e] + o_r * inv[e + 1])
            o_ref[it * rows_per_iter + u] = jnp.concatenate(outs, axis=-1).astype(o_ref.dtype)
        return carry

    lax.fori_loop(0, NA_ROW_BLOCK // rows_per_iter, rows_step, 0)


def _na_bias_table(rpb):
    x = np.arange(GRID_W)[:, None]
    c = np.arange(GRID_W)[None, :]
    c0 = np.clip(x - NA_WIN_COLS // 2, 0, GRID_W - NA_WIN_COLS)
    valid = ((c >= c0) & (c < c0 + NA_WIN_COLS)).reshape(-1)
    dc = np.clip(c - x + (NA_WIN_COLS - 1), 0, 2 * NA_WIN_COLS - 2).reshape(-1)
    onehot = (dc[None, :] == np.arange(2 * NA_WIN_COLS - 1)[:, None]) & valid[None, :]
    tab = jnp.einsum('hrd,dn->hrn', rpb, jnp.asarray(onehot, F32), precision=lax.Precision.HIGHEST)
    tab = tab + jnp.asarray(np.where(valid, 0.0, NEG_BIG), F32)
    tab = tab.reshape(NA_HEADS, 2 * NA_WIN_ROWS - 1, GRID_W, GRID_W)
    return jnp.concatenate([tab[:, :-1], tab[:, 1:]], axis=-1)


def _na_attention(z, bias_tab, bsz, seq):
    rows = seq // GRID_W
    assert rows >= NA_BAND_ROWS and rows % NA_ROW_BLOCK == 0
    z4 = z.reshape(bsz, rows, GRID_W, Z_COLS)
    half = NA_WIN_ROWS // 2

    def kv_spec(col):
        return pl.BlockSpec(
            (pl.Squeezed(), pl.Element(NA_BAND_ROWS), pl.Element(GRID_W), pl.Element(NA_WIDTH)),
            lambda b, i: (b, jnp.clip(i * NA_ROW_BLOCK - half, 0, rows - NA_BAND_ROWS), 0, col))

    qblk = Z_NA // NA_WIDTH
    out = pl.pallas_call(
        functools.partial(_na_kernel, rows=rows),
        grid=(bsz, rows // NA_ROW_BLOCK),
        in_specs=[
            pl.BlockSpec((None, NA_ROW_BLOCK, GRID_W, NA_WIDTH), lambda b, i: (b, i, 0, qblk)),
            kv_spec(Z_NA + NA_WIDTH),
            kv_spec(Z_NA + 2 * NA_WIDTH),
            pl.BlockSpec(bias_tab.shape, lambda b, i: (0, 0, 0, 0)),
        ],
        out_specs=pl.BlockSpec((None, NA_ROW_BLOCK, GRID_W, NA_WIDTH), lambda b, i: (b, i, 0, 0)),
        out_shape=jax.ShapeDtypeStruct((bsz, rows, GRID_W, NA_WIDTH), BF16),
        scratch_shapes=[pltpu.VMEM((NA_BAND_ROWS, GRID_W, NA_WIDTH), BF16)] * 2,
        compiler_params=_cparams(("parallel", "arbitrary")),
        name="na_attn",
    )(z4, z4, z4, bias_tab)
    return out.reshape(bsz * seq, NA_WIDTH)


def _softplus(u):
    return jnp.maximum(u, 0.0) + jnp.log1p(jnp.exp(-jnp.abs(u)))


def _rwkv_kernel(zc_ref, zp_ref, zn_ref, cw_ref, d0_ref, d2_ref, a0_ref, a2_ref, g2_ref,
                 kk_ref, ka_ref, rk_ref, bones_ref, tri_ref,
                 y_ref, bonus_ref, gate_ref, s_ref, *bufs, tile, nt, reverse):
    i = pl.program_id(1)
    ti = nt - 1 - i if reverse else i

    @pl.when(i == 0)
    def _():
        s_ref[...] = jnp.zeros_like(s_ref)

    _rwkv_prep_tile(zc_ref, zp_ref, zn_ref, cw_ref, d0_ref, d2_ref, a0_ref, a2_ref, g2_ref,
                    kk_ref, ka_ref, rk_ref, bones_ref, tri_ref, bonus_ref, gate_ref, *bufs,
                    tile=tile, nt=nt, ti=ti, reverse=reverse)
    _rwkv_scan_tile(y_ref, s_ref, *bufs, tile=tile, reverse=reverse)


def _rwkv_prep_tile(zc_ref, zp_ref, zn_ref, cw_ref, d0_ref, d2_ref, a0_ref, a2_ref, g2_ref,
                    kk_ref, ka_ref, rk_ref, bones_ref, tri_ref, bonus_ref, gate_ref,
                    rt_ref, at_ref, bt_ref, kt_ref, bh_ref, kh_ref, v_ref, tot_ref,
                    *, tile, nt, ti, reverse):
    n_chunks = tile // CHUNK

    zc = zc_ref[...].astype(F32)
    prev_row = jnp.where(ti == 0, 0.0, zp_ref[...].astype(F32)[7:8, :])
    next_row = jnp.where(ti == nt - 1, 0.0, zn_ref[...].astype(F32)[0:1, :])
    zf = (pltpu.roll(zc, 1, axis=0) * cw_ref[0:1, :] + zc * cw_ref[1:2, :]
          + pltpu.roll(zc, tile - 1, axis=0) * cw_ref[2:3, :])
    rows8 = lax.broadcasted_iota(jnp.int32, (8, 1), 0)
    fix_first = jnp.where(rows8 == 0, (prev_row - zc[tile - 1:tile, :]) * cw_ref[0:1, :], 0.0)
    fix_last = jnp.where(rows8 == 7, (next_row - zc[0:1, :]) * cw_ref[2:3, :], 0.0)
    zf = jnp.concatenate([zf[0:8] + fix_first, zf[8:tile - 8], zf[tile - 8:tile] + fix_last], axis=0)

    o1, o2, o3 = RW_WIDTH, 2 * RW_WIDTH, 3 * RW_WIDTH
    o4 = o3 + 2 * DECAY_LORA
    o5 = o4 + 2 * AAA_LORA
    r = zf[:, 0:o1]
    k = zf[:, o1:o2]
    v = zf[:, o2:o3]
    xw = zf[:, o3:o4]
    xa = zf[:, o4:o5]
    xg = zf[:, o5:RW_COLS_PAD]

    bones = bones_ref[...]
    kk = k * kk_ref[...]
    kk = kk * lax.rsqrt(jnp.maximum(_dot((kk * kk).astype(BF16), bones), 1e-24))
    gate_ref[...] = _dot(jax.nn.sigmoid(xg).astype(BF16), g2_ref[...]).astype(gate_ref.dtype)

    w_log = -_softplus(-(d0_ref[...] + _dot(jnp.tanh(xw).astype(BF16), d2_ref[...]))) - 0.5
    logw = -jnp.exp(w_log)
    a = jax.nn.sigmoid(a0_ref[...] + _dot(xa.astype(BF16), a2_ref[...]))
    k_d = k * (1.0 + (a - 1.0) * ka_ref[...])
    b = kk * a
    bonus_ref[...] = (_dot((r * k_d * rk_ref[...]).astype(BF16), bones) * v).astype(bonus_ref.dtype)

    span = tri_ref.shape[0]
    g = jnp.concatenate([_sel_left(tri_ref[...], logw[r:r + span]) for r in range(0, tile, span)], axis=0)
    last = 0 if reverse else CHUNK - 1
    tot = jnp.concatenate(
        [jnp.broadcast_to(g[c * CHUNK + last:c * CHUNK + last + 1, :], (CHUNK, RW_WIDTH))
         for c in range(n_chunks)], axis=0)
    eng = jnp.exp(-g)
    ehat = jnp.exp(tot - g)
    rt_ref[...] = r * jnp.exp(g)
    at_ref[...] = -kk * jnp.exp(g - logw)
    bt_ref[...] = b * eng
    kt_ref[...] = k_d * eng
    bh_ref[...] = b * ehat
    kh_ref[...] = k_d * ehat
    v_ref[...] = v
    tot_ref[...] = tot


def _rwkv_scan_tile(y_ref, s_ref, rt_ref, at_ref, bt_ref, kt_ref, bh_ref, kh_ref, v_ref, tot_ref,
                    *, tile, reverse):
    n_chunks = tile // CHUNK

    ri = lax.broadcasted_iota(jnp.int32, (PAIR, PAIR), 0)
    ci = lax.broadcasted_iota(jnp.int32, (PAIR, PAIR), 1)
    same_head = jnp.where(jnp.right_shift(ri, 6) == jnp.right_shift(ci, 6), 1.0, 0.0).astype(F32)
    order = ci - ri if reverse else ri - ci
    m_strict = jnp.where(order > 0, same_head, 0.0)
    m_incl = jnp.where(order >= 0, same_head, 0.0)
    eye = jnp.where(ci == ri, 1.0, 0.0).astype(F32)
    lvl = []
    for q in range(INV_LEVELS):
        in_block = jnp.where(jnp.right_shift(ri, q + 1) == jnp.right_shift(ci, q + 1), 1.0, 0.0).astype(F32)
        lvl.append(jnp.where(jnp.right_shift(ri, q) != jnp.right_shift(ci, q), in_block, 0.0))
    lane = lax.broadcasted_iota(jnp.int32, (CHUNK, PAIR), 1)
    left = lane < RW_HEAD_DIM

    def stack(x):
        return jnp.concatenate([jnp.where(left, x, 0.0), jnp.where(left, 0.0, x)], axis=0)

    groups = [(c, p) for c in range(n_chunks) for p in range(N_PAIRS)]
    eye_b = eye.astype(BF16)

    def tile_of(ref, c, p):
        return ref[c * CHUNK:(c + 1) * CHUNK, p * PAIR:(p + 1) * PAIR]

    at_b = [stack(tile_of(at_ref, c, p)).astype(BF16) for c, p in groups]
    vs_b = [stack(tile_of(v_ref, c, p)).astype(BF16) for c, p in groups]
    o1_ = []
    for gi, (c, p) in enumerate(groups):
        bt = tile_of(bt_ref, c, p).astype(BF16)
        kt = tile_of(kt_ref, c, p).astype(BF16)
        lhs1 = jnp.concatenate([at_b[gi], stack(tile_of(rt_ref, c, p)).astype(BF16)], axis=0)
        o1_.append(_dot_nt(lhs1, jnp.concatenate([bt, bt, kt, kt], axis=0)))
    ms_b = m_strict.astype(BF16)
    mi_b = m_incl.astype(BF16)
    lvl_b = [m.astype(BF16) for m in lvl]
    n_b = [o[0:PAIR, 0:PAIR].astype(BF16) * ms_b for o in o1_]
    a_ak = [o[0:PAIR, PAIR:].astype(BF16) * ms_b for o in o1_]
    a_r = [jnp.concatenate([o[PAIR:, 0:PAIR].astype(BF16) * mi_b, o[PAIR:, PAIR:].astype(BF16) * mi_b], axis=1)
           for o in o1_]

    t_inv = [eye.astype(BF16) + n * lvl_b[0] for n in n_b]
    for q in range(1, INV_LEVELS):
        blk = 1 << q
        if blk < 16:
            xq = [_dot(n * lvl_b[q], t) for n, t in zip(n_b, t_inv)]
            t_inv = [t + _dot(t, x.astype(BF16)).astype(BF16) for t, x in zip(t_inv, xq)]
            continue
        n_blocks = PAIR // blk
        late = [k for k in range(n_blocks) if (k % 2 == 0) == reverse]

        def take(x):
            return jnp.concatenate([x[k * blk:(k + 1) * blk] for k in late], axis=0)

        def spread(x):
            zero = jnp.zeros((blk, PAIR), x.dtype)
            return jnp.concatenate(
                [x[late.index(k) * blk:(late.index(k) + 1) * blk] if k in late else zero
                 for k in range(n_blocks)], axis=0)

        xq = [_dot(take(n) * take(lvl_b[q]), t) for n, t in zip(n_b, t_inv)]
        upd = [_dot(take(t), spread(x.astype(BF16))) for t, x in zip(t_inv, xq)]
        t_inv = [t + spread(u.astype(BF16)) for t, u in zip(t_inv, upd)]

    av = [_dot(a, v_) for a, v_ in zip(a_ak, vs_b)]
    pu = [_dot(t, jnp.concatenate([a, x.astype(BF16)], axis=1))
          for t, a, x in zip(t_inv, at_b, av)]
    zero_b = jnp.zeros((PAIR, PAIR), BF16)
    qb, y_loc, n_st = [], [], []
    for gi, (c, p) in enumerate(groups):
        bh_t = stack(tile_of(bh_ref, c, p)).T
        kh_t = stack(tile_of(kh_ref, c, p)).T
        wmat = jnp.concatenate(
            [pu[gi].astype(BF16), jnp.concatenate([zero_b, vs_b[gi]], axis=1)], axis=0)
        lhs2 = jnp.concatenate([a_r[gi], jnp.concatenate([bh_t, kh_t], axis=1).astype(BF16)], axis=0)
        o2_ = _dot(lhs2, wmat)
        q_mat = stack(tile_of(rt_ref, c, p)) + o2_[0:PAIR, 0:PAIR]
        qb.append(jnp.concatenate([q_mat, o2_[PAIR:, 0:PAIR]], axis=0).astype(BF16))
        y_loc.append(o2_[0:PAIR, PAIR:])
        n_st.append(o2_[PAIR:, PAIR:])

    rep = PAIR // n_chunks
    w_tot = []
    for p in range(N_PAIRS):
        t_rows = jnp.concatenate(
            [tot_ref[c * CHUNK:c * CHUNK + rep, p * PAIR:(p + 1) * PAIR] for c in range(n_chunks)], axis=0)
        hi = t_rows.astype(BF16)
        r1 = t_rows - hi.astype(F32)
        mid = r1.astype(BF16)
        lo = (r1 - mid.astype(F32)).astype(BF16)
        cols = _dot_nt(eye_b, hi) + _dot_nt(eye_b, mid) + _dot_nt(eye_b, lo)
        w_tot.append([jnp.exp(cols[:, c * rep:c * rep + 1]) for c in range(n_chunks)])

    st = [s_ref[p] for p in range(N_PAIRS)]
    for c in (reversed(range(n_chunks)) if reverse else range(n_chunks)):
        o3_ = [_dot(qb[c * N_PAIRS + p], st[p].astype(BF16)) for p in range(N_PAIRS)]
        for p in range(N_PAIRS):
            gi = c * N_PAIRS + p
            y_bs = o3_[p][0:PAIR] + y_loc[gi]
            st[p] = st[p] * w_tot[p][c] + o3_[p][PAIR:] + n_st[gi]
            y_ref[c * CHUNK:(c + 1) * CHUNK, p * PAIR:(p + 1) * PAIR] = y_bs[0:CHUNK] + y_bs[CHUNK:]
    for p in range(N_PAIRS):
        s_ref[p] = st[p]


def _rwkv_scan(z, lp, bsz, seq, reverse):
    tile = min(512, seq)
    nt = seq // tile
    z3 = z.reshape(bsz, seq, Z_COLS)
    t8 = tile // 8
    d = int(reverse)

    def tidx(i):
        return nt - 1 - i if reverse else i

    span = min(256, tile)
    cidx = np.arange(span)
    same = (cidx[:, None] // CHUNK) == (cidx[None, :] // CHUNK)
    upto = cidx[None, :] >= cidx[:, None] if reverse else cidx[None, :] <= cidx[:, None]
    tri = jnp.asarray(same & upto, BF16)

    def full(shape):
        return pl.BlockSpec(shape, lambda b, i: (0,) * len(shape))

    def per_dir(shape):
        return pl.BlockSpec((None,) + shape, lambda b, i: (d,) + (0,) * len(shape))

    out_spec = pl.BlockSpec((None, tile, RW_WIDTH), lambda b, i: (b, tidx(i), 0))
    nat = pltpu.VMEM((tile, RW_WIDTH), F32)
    return pl.pallas_call(
        functools.partial(_rwkv_kernel, tile=tile, nt=nt, reverse=reverse),
        grid=(bsz, nt),
        in_specs=[
            pl.BlockSpec((None, tile, RW_COLS_PAD), lambda b, i: (b, tidx(i), 0)),
            pl.BlockSpec((None, 8, RW_COLS_PAD), lambda b, i: (b, jnp.maximum(tidx(i) * t8 - 1, 0), 0)),
            pl.BlockSpec((None, 8, RW_COLS_PAD),
                         lambda b, i: (b, jnp.minimum((tidx(i) + 1) * t8, seq // 8 - 1), 0)),
            full((3, RW_COLS_PAD)),
            per_dir((1, RW_WIDTH)), per_dir((2 * DECAY_LORA, RW_WIDTH)),
            per_dir((1, RW_WIDTH)), per_dir((2 * AAA_LORA, RW_WIDTH)),
            full((GATE_LORA_PAD, RW_WIDTH)),
            full((1, RW_WIDTH)), full((1, RW_WIDTH)), full((1, RW_WIDTH)),
            full((RW_WIDTH, RW_WIDTH)),
            full((span, span)),
        ],
        out_specs=[out_spec, out_spec, out_spec],
        out_shape=[jax.ShapeDtypeStruct((bsz, seq, RW_WIDTH), F32),
                   jax.ShapeDtypeStruct((bsz, seq, RW_WIDTH), BF16),
                   jax.ShapeDtypeStruct((bsz, seq, RW_WIDTH), BF16)],
        scratch_shapes=[pltpu.VMEM((N_PAIRS, PAIR, PAIR), F32)] + [nat] * 8,
        compiler_params=_cparams(("arbitrary", "arbitrary")),
        name="rwkv_bwd" if reverse else "rwkv_fwd",
    )(z3, z3, z3, lp["rw_conv"], lp["rw_decay0"], lp["rw_decay2"], lp["rw_a0"], lp["rw_a2"],
      lp["rw_g2"], lp["rw_k_k"], lp["rw_k_a"], lp["rw_r_k"], lp["bones"], tri)


def _memkv_kernel(m_ref, g_ref, w_ref, o_ref):
    o_ref[...] = _dot(_rms(m_ref[...], g_ref[...]).astype(BF16), w_ref[...]).astype(o_ref.dtype)


def _mem_kv(mem, gain, w):
    bsz = mem.shape[0]
    return pl.pallas_call(
        _memkv_kernel,
        grid=(bsz,),
        in_specs=[
            pl.BlockSpec((None, MEM_TOKENS, D_MODEL), lambda b: (b, 0, 0)),
            pl.BlockSpec((1, D_MODEL), lambda b: (0, 0)),
            pl.BlockSpec((D_MODEL, 2 * MEM_WIDTH), lambda b: (0, 0)),
        ],
        out_specs=pl.BlockSpec((None, MEM_TOKENS, 2 * MEM_WIDTH), lambda b: (b, 0, 0)),
        out_shape=jax.ShapeDtypeStruct((bsz, MEM_TOKENS, 2 * MEM_WIDTH), BF16),
        compiler_params=_cparams(("parallel",)),
        name="mem_kv",
    )(mem, gain, w)


def _memattn_kernel(q_ref, kv_ref, o_ref):
    scale = MEM_HEAD_DIM ** -0.5
    outs = []
    for h in range(MEM_HEADS):
        sl = slice(h * MEM_HEAD_DIM, (h + 1) * MEM_HEAD_DIM)
        vsl = slice(MEM_WIDTH + h * MEM_HEAD_DIM, MEM_WIDTH + (h + 1) * MEM_HEAD_DIM)
        s = _dot_nt(q_ref[:, sl], kv_ref[:, sl]) * scale
        m = jnp.max(s, axis=-1, keepdims=True)
        p = jnp.exp(s - m)
        l = jnp.sum(p, axis=-1, keepdims=True)
        outs.append(_dot(p.astype(BF16), kv_ref[:, vsl]) / l)
    o_ref[...] = jnp.concatenate(outs, axis=-1).astype(o_ref.dtype)


def _mem_attention(z, kv, bsz, seq):
    tm = min(1024, seq)
    z3 = z.reshape(bsz, seq, Z_COLS)
    out = pl.pallas_call(
        _memattn_kernel,
        grid=(bsz, seq // tm),
        in_specs=[
            pl.BlockSpec((None, tm, MEM_WIDTH), lambda b, i: (b, i, Z_MEM // MEM_WIDTH)),
            pl.BlockSpec((None, MEM_TOKENS, 2 * MEM_WIDTH), lambda b, i: (b, 0, 0)),
        ],
        out_specs=pl.BlockSpec((None, tm, MEM_WIDTH), lambda b, i: (b, i, 0)),
        out_shape=jax.ShapeDtypeStruct((bsz, seq, MEM_WIDTH), BF16),
        compiler_params=_cparams(("parallel", "parallel")),
        name="mem_attn",
    )(z3, kv)
    return out.reshape(bsz * seq, MEM_WIDTH)


def _merge_kernel(x_ref, ona_ref, y0_ref, y1_ref, b0_ref, b1_ref, gt_ref, omem_ref,
                  g0_ref, g1_ref, g2_ref, wb_ref, wo_ref, lw_ref, lb_ref, bones_ref, o_ref):
    bones = bones_ref[...]
    inv_n = 1.0 / RW_HEAD_DIM
    y = y0_ref[...] + y1_ref[...]
    mu = _dot(y.astype(BF16), bones) * inv_n
    dl = y - mu
    var = _dot((dl * dl).astype(BF16), bones) * inv_n
    yn = dl * lax.rsqrt(var + GN_EPS)
    bonus = b0_ref[...].astype(F32) + b1_ref[...].astype(F32)
    o_rw = (yn * lw_ref[...] + lb_ref[...] + bonus) * gt_ref[...].astype(F32)

    merged = jax.nn.sigmoid(g0_ref[...].astype(F32)) * _dot(ona_ref[...], wb_ref[0])
    merged = merged + jax.nn.sigmoid(g1_ref[...].astype(F32)) * _dot(o_rw.astype(BF16), wb_ref[1])
    merged = merged + jax.nn.sigmoid(g2_ref[...].astype(F32)) * _dot(omem_ref[...], wb_ref[2])
    o_ref[...] = x_ref[...] + _dot(merged.astype(BF16), wo_ref[...])


def _merge(x2d, z, o_na, rw_fwd, rw_bwd, o_mem, lp):
    t = x2d.shape[0]
    tm = min(512, t)
    y0, b0, gate = (a.reshape(t, RW_WIDTH) for a in rw_fwd)
    y1, b1, _ = (a.reshape(t, RW_WIDTH) for a in rw_bwd)
    gblk = Z_GATE // D_MODEL

    def tok(width):
        return pl.BlockSpec((tm, width), lambda i: (i, 0))

    def const(shape):
        return pl.BlockSpec(shape, lambda i: (0,) * len(shape))

    return pl.pallas_call(
        _merge_kernel,
        grid=(t // tm,),
        in_specs=[
            tok(D_MODEL), tok(NA_WIDTH), tok(RW_WIDTH), tok(RW_WIDTH), tok(RW_WIDTH), tok(RW_WIDTH), tok(RW_WIDTH),
            tok(MEM_WIDTH),
            pl.BlockSpec((tm, D_MODEL), lambda i: (i, gblk)),
            pl.BlockSpec((tm, D_MODEL), lambda i: (i, gblk + 1)),
            pl.BlockSpec((tm, D_MODEL), lambda i: (i, gblk + 2)),
            const((N_BRANCH, NA_WIDTH, D_MODEL)), const((D_MODEL, D_MODEL)),
            const((1, RW_WIDTH)), const((1, RW_WIDTH)), const((RW_WIDTH, RW_WIDTH)),
        ],
        out_specs=tok(D_MODEL),
        out_shape=jax.ShapeDtypeStruct((t, D_MODEL), F32),
        compiler_params=_cparams(("parallel",)),
        name="merge",
    )(x2d, o_na, y0, y1, b0, b1, gate, o_mem, z, z, z, lp["w_branch"], lp["w_out"],
      lp["rw_lnx_w"], lp["rw_lnx_b"], lp["bones"])


def _ffn_kernel(x_ref, xp_ref, xn_ref, g_ref, wv_ref, wg_ref, cv_ref, cg_ref, bv_ref, bg_ref, wd_ref,
                fg_ref, o_ref, h_ref, acc_ref, *, tm, tiles_per_seq, final_norm):
    i = pl.program_id(0)
    j = pl.program_id(1)
    ext = tm + 16

    @pl.when(j == 0)
    def _():
        gain = g_ref[...]
        seq_pos = i % tiles_per_seq
        h_ref[0:8, :] = jnp.where(seq_pos == 0, 0.0, _rms(xp_ref[...], gain)).astype(BF16)
        h_ref[8:8 + tm, :] = _rms(x_ref[...], gain).astype(BF16)
        h_ref[8 + tm:ext, :] = jnp.where(seq_pos == tiles_per_seq - 1, 0.0, _rms(xn_ref[...], gain)).astype(BF16)
        acc_ref[...] = jnp.zeros_like(acc_ref)

    hx = h_ref[...]

    def conv(u, cw, bias):
        return (pltpu.roll(u, 1, axis=0)[8:8 + tm] * cw[0:1, :] + u[8:8 + tm] * cw[1:2, :]
                + pltpu.roll(u, ext - 1, axis=0)[8:8 + tm] * cw[2:3, :] + bias)

    u_val = conv(_dot(hx, wv_ref[...]), cv_ref[...], bv_ref[...])
    u_gate = conv(_dot(hx, wg_ref[...]), cg_ref[...], bg_ref[...])
    act = (u_gate * jax.nn.sigmoid(u_gate) * u_val).astype(BF16)
    acc_ref[...] += _dot(act, wd_ref[...])

    @pl.when(j == pl.num_programs(1) - 1)
    def _():
        out = x_ref[...] + acc_ref[...]
        if final_norm:
            out = _rms(out, fg_ref[...])
        o_ref[...] = out


def _ffn(x2d, lp, final_gain, seq, final_norm):
    t = x2d.shape[0]
    tm = min(1024, seq)
    tn = 256
    nj = D_FF // tn
    t8 = tm // 8
    return pl.pallas_call(
        functools.partial(_ffn_kernel, tm=tm, tiles_per_seq=seq // tm, final_norm=final_norm),
        grid=(t // tm, nj),
        in_specs=[
            pl.BlockSpec((tm, D_MODEL), lambda i, j: (i, 0)),
            pl.BlockSpec((8, D_MODEL), lambda i, j: (jnp.maximum(i * t8 - 1, 0), 0)),
            pl.BlockSpec((8, D_MODEL), lambda i, j: (jnp.minimum((i + 1) * t8, t // 8 - 1), 0)),
            pl.BlockSpec((1, D_MODEL), lambda i, j: (0, 0)),
            pl.BlockSpec((D_MODEL, tn), lambda i, j: (0, j)),
            pl.BlockSpec((D_MODEL, tn), lambda i, j: (0, nj + j)),
            pl.BlockSpec((3, tn), lambda i, j: (0, j)),
            pl.BlockSpec((3, tn), lambda i, j: (0, nj + j)),
            pl.BlockSpec((1, tn), lambda i, j: (0, j)),
            pl.BlockSpec((1, tn), lambda i, j: (0, nj + j)),
            pl.BlockSpec((tn, D_MODEL), lambda i, j: (j, 0)),
            pl.BlockSpec((1, D_MODEL), lambda i, j: (0, 0)),
        ],
        out_specs=pl.BlockSpec((tm, D_MODEL), lambda i, j: (i, 0)),
        out_shape=jax.ShapeDtypeStruct((t, D_MODEL), F32),
        scratch_shapes=[pltpu.VMEM((tm + 16, D_MODEL), BF16), pltpu.VMEM((tm, D_MODEL), F32)],
        compiler_params=_cparams(("parallel", "arbitrary")),
        name="ffn",
    )(x2d, x2d, x2d, lp["ffn_norm"], lp["w_up"], lp["w_up"], lp["ffn_conv"], lp["ffn_conv"],
      lp["ffn_conv_b"], lp["ffn_conv_b"], lp["w_down"], final_gain)


def _prep_layer(l, attn_norm, w_in, na_rpb, rw_conv, rw_decay0, rw_decay2, rw_a0, rw_a2, rw_g2, rw_k_k,
                rw_k_a, rw_r_k, rw_lnx_w, rw_lnx_b, mem_norm, w_mem_kv, w_branch, w_out, ffn_norm, w_up,
                ffn_conv, ffn_conv_b, w_down):
    c1 = 3 * NA_WIDTH
    c2 = c1 + RW_COLS
    w = w_in[l]
    pad_cols = RW_COLS_PAD - RW_COLS
    w_new = jnp.concatenate(
        [w[:, c1:c2], jnp.zeros((D_MODEL, pad_cols), w.dtype), w[:, :c1], w[:, c2:]], axis=1)

    def lora_rows(m):
        zero = jnp.zeros_like(m[0])
        return jnp.stack([jnp.concatenate([m[0], zero], 0), jnp.concatenate([zero, m[1]], 0)]).astype(BF16)

    head = np.arange(RW_WIDTH) // RW_HEAD_DIM
    return dict(
        attn_norm=attn_norm[l][None, :],
        w_in=w_new.astype(BF16),
        na_bias=_na_bias_table(na_rpb[l]),
        rw_conv=jnp.pad(rw_conv[l], ((0, 0), (0, pad_cols))),
        rw_decay0=rw_decay0[l][:, None, :],
        rw_decay2=lora_rows(rw_decay2[l]),
        rw_a0=rw_a0[l][:, None, :],
        rw_a2=lora_rows(rw_a2[l]),
        rw_g2=jnp.pad(rw_g2[l], ((0, GATE_LORA_PAD - GATE_LORA), (0, 0))).astype(BF16),
        rw_k_k=rw_k_k[l][None, :],
        rw_k_a=rw_k_a[l][None, :],
        rw_r_k=rw_r_k[l].reshape(1, RW_WIDTH),
        rw_lnx_w=rw_lnx_w[l][None, :],
        rw_lnx_b=rw_lnx_b[l][None, :],
        bones=jnp.asarray(head[:, None] == head[None, :], BF16),
        mem_norm=mem_norm[l][None, :],
        w_mem_kv=w_mem_kv[l].astype(BF16),
        w_branch=w_branch[l].astype(BF16),
        w_out=w_out[l].astype(BF16),
        ffn_norm=ffn_norm[l][None, :],
        w_up=w_up[l].astype(BF16),
        ffn_conv=ffn_conv[l],
        ffn_conv_b=ffn_conv_b[l][None, :],
        w_down=w_down[l].astype(BF16),
    )


def _layer(x2d, mem, lp, bsz, seq, final_gain, final_norm):
    z = _in_proj(x2d, lp["attn_norm"], lp["w_in"])
    o_na = _na_attention(z, lp["na_bias"], bsz, seq)
    rw_fwd = _rwkv_scan(z, lp, bsz, seq, reverse=False)
    rw_bwd = _rwkv_scan(z, lp, bsz, seq, reverse=True)
    o_mem = _mem_attention(z, _mem_kv(mem, lp["mem_norm"], lp["w_mem_kv"]), bsz, seq)
    x2d = _merge(x2d, z, o_na, rw_fwd, rw_bwd, o_mem, lp)
    return _ffn(x2d, lp, final_gain, seq, final_norm)


def _trunk(x, mem, layers, final_gain):
    bsz, seq, _ = x.shape
    x2d = x.reshape(bsz * seq, D_MODEL)
    for l, lp in enumerate(layers):
        x2d = _layer(x2d, mem, lp, bsz, seq, final_gain, l == len(layers) - 1)
    return x2d.reshape(bsz, seq, D_MODEL)


def kernel(x_prompt, x_sample, mem_prompt, mem_sample, attn_norm, w_in, na_rpb, rw_conv, rw_decay0, rw_decay2, rw_a0, rw_a2, rw_g2, rw_k_k, rw_k_a, rw_r_k, rw_lnx_w, rw_lnx_b, mem_norm, w_mem_kv, w_branch, w_out, ffn_norm, w_up, ffn_conv, ffn_conv_b, w_down, final_norm):
    params = (attn_norm, w_in, na_rpb, rw_conv, rw_decay0, rw_decay2, rw_a0, rw_a2, rw_g2, rw_k_k, rw_k_a,
              rw_r_k, rw_lnx_w, rw_lnx_b, mem_norm, w_mem_kv, w_branch, w_out, ffn_norm, w_up, ffn_conv,
              ffn_conv_b, w_down)
    layers = [_prep_layer(l, *params) for l in range(attn_norm.shape[0])]
    final_gain = final_norm[None, :]
    y_prompt = _trunk(x_prompt, mem_prompt, layers, final_gain)
    y_sample = _trunk(x_sample, mem_sample, layers, final_gain)
    return (y_prompt, y_sample)
```

```python
import functools

import jax
import jax.numpy as jnp
import numpy as np
from jax import lax
from jax.experimental import pallas as pl
from jax.experimental.pallas import tpu as pltpu

F32 = jnp.float32
BF16 = jnp.bfloat16

D_MODEL = 1024
GRID_W = 64
NA_HEADS = 8
NA_HEAD_DIM = 64
NA_WIDTH = 512
NA_WIN_ROWS = 8
NA_WIN_COLS = 16
RW_HEADS = 8
RW_HEAD_DIM = 64
RW_WIDTH = 512
DECAY_LORA = 64
AAA_LORA = 64
GATE_LORA = 160
GATE_LORA_PAD = 256
MEM_TOKENS = 256
MEM_HEADS = 4
MEM_HEAD_DIM = 128
MEM_WIDTH = 512
N_BRANCH = 3
D_FF = 2816
RMS_EPS = 1e-6
GN_EPS = 64e-5

RW_COLS = 3 * RW_WIDTH + 2 * DECAY_LORA + 2 * AAA_LORA + GATE_LORA
RW_COLS_PAD = 3 * RW_WIDTH + 2 * DECAY_LORA + 2 * AAA_LORA + GATE_LORA_PAD
Z_RW = 0
Z_NA = RW_COLS_PAD
Z_MEM = Z_NA + 3 * NA_WIDTH
Z_GATE = Z_MEM + MEM_WIDTH
Z_COLS = Z_GATE + N_BRANCH * D_MODEL

CHUNK = 64
PAIR = 2 * RW_HEAD_DIM
N_PAIRS = RW_HEADS // 2
INV_LEVELS = 6
NEG_BIG = -1e30
VMEM_LIMIT = 56 * 1024 * 1024


def _cparams(sem):
    return pltpu.CompilerParams(dimension_semantics=sem, vmem_limit_bytes=VMEM_LIMIT)


def _rms(x, gain):
    ms = jnp.mean(x * x, axis=-1, keepdims=True)
    return x * lax.rsqrt(ms + RMS_EPS) * gain


def _dot(a, b):
    return jnp.dot(a, b, preferred_element_type=F32)


def _dot_nt(a, b):
    return lax.dot_general(a, b, (((1,), (1,)), ((), ())), preferred_element_type=F32)


def _split_hi_lo(x):
    hi = x.astype(BF16)
    lo = (x - hi.astype(F32)).astype(BF16)
    return hi, lo


def _sel_left(sel, x):
    hi, lo = _split_hi_lo(x)
    return _dot(sel, hi) + _dot(sel, lo)


def _sel_right(x, sel):
    hi, lo = _split_hi_lo(x)
    return _dot(hi, sel) + _dot(lo, sel)


def _inproj_kernel(x_ref, g_ref, w_ref, o_ref, h_ref):
    @pl.when(pl.program_id(1) == 0)
    def _():
        h_ref[...] = _rms(x_ref[...], g_ref[...]).astype(BF16)

    o_ref[...] = _dot(h_ref[...], w_ref[...]).astype(o_ref.dtype)


def _in_proj(x2d, gain, w):
    t, n = x2d.shape[0], w.shape[1]
    tm = min(2048, t)
    tn = 1024
    return pl.pallas_call(
        _inproj_kernel,
        grid=(t // tm, n // tn),
        in_specs=[
            pl.BlockSpec((tm, D_MODEL), lambda i, j: (i, 0)),
            pl.BlockSpec((1, D_MODEL), lambda i, j: (0, 0)),
            pl.BlockSpec((D_MODEL, tn), lambda i, j: (0, j)),
        ],
        out_specs=pl.BlockSpec((tm, tn), lambda i, j: (i, j)),
        out_shape=jax.ShapeDtypeStruct((t, n), BF16),
        scratch_shapes=[pltpu.VMEM((tm, D_MODEL), BF16)],
        compiler_params=_cparams(("parallel", "arbitrary")),
        name="in_proj",
    )(x2d, gain, w)


NA_ROW_BLOCK = 8
NA_BAND_ROWS = NA_ROW_BLOCK + NA_WIN_ROWS - 1


def _na_kernel(q_ref, k_ref, v_ref, t2_ref, o_ref, vl_ref, vr_ref, *, rows):
    half = NA_WIN_ROWS // 2
    band = NA_WIN_ROWS * GRID_W
    i0 = pl.program_id(1) * NA_ROW_BLOCK
    block_start = jnp.clip(i0 - half, 0, rows - NA_BAND_ROWS)
    scale = NA_HEAD_DIM ** -0.5

    lane = lax.broadcasted_iota(jnp.int32, (GRID_W, NA_WIDTH), 1)
    first = jnp.where(lane % PAIR < NA_HEAD_DIM, 1.0, 0.0).astype(BF16)
    second = jnp.where(lane % PAIR < NA_HEAD_DIM, 0.0, 1.0).astype(BF16)
    for r in range(NA_BAND_ROWS):
        v_row = v_ref[r]
        vl_ref[r] = v_row * first
        vr_ref[r] = v_row * second

    rows_per_iter = 8

    def rows_step(it, carry):
        s, v_sel = [], []
        for u in range(rows_per_iter):
            m = it * rows_per_iter + u
            i = i0 + m
            r0 = jnp.clip(i - half, 0, rows - NA_WIN_ROWS)
            off = r0 - block_start
            dr0 = (NA_WIN_ROWS - 1) - (i - r0)
            q = q_ref[m] * scale
            q_sel = (q * first, q * second)
            kb = k_ref[pl.ds(off, NA_WIN_ROWS)].reshape(band, NA_WIDTH)
            v_sel.append((vl_ref[pl.ds(off, NA_WIN_ROWS)].reshape(band, NA_WIDTH),
                          vr_ref[pl.ds(off, NA_WIN_ROWS)].reshape(band, NA_WIDTH)))
            for h in range(NA_HEADS):
                ps = slice((h // 2) * PAIR, (h // 2 + 1) * PAIR)
                bias = jnp.concatenate([t2_ref[h, dr0 + 2 * jj] for jj in range(NA_WIN_ROWS // 2)], axis=1)
                s.append(_dot_nt(q_sel[h % 2][:, ps], kb[:, ps]) + bias)
        mx = [jnp.max(x, axis=-1, keepdims=True) for x in s]
        p = [jnp.exp(x - m_) for x, m_ in zip(s, mx)]
        inv = [1.0 / jnp.sum(x, axis=-1, keepdims=True) for x in p]
        for u in range(rows_per_iter):
            outs = []
            for pp in range(NA_HEADS // 2):
                ps = slice(pp * PAIR, (pp + 1) * PAIR)
                e = u * NA_HEADS + 2 * pp
                o_l = _dot(p[e].astype(BF16), v_sel[u][0][:, ps])
                o_r = _dot(p[e + 1].astype(BF16), v_sel[u][1][:, ps])
                outs.append(o_l * inv[e] + o_r * inv[e + 1])
            o_ref[it * rows_per_iter + u] = jnp.concatenate(outs, axis=-1).astype(o_ref.dtype)
        return carry

    lax.fori_loop(0, NA_ROW_BLOCK // rows_per_iter, rows_step, 0)


def _na_bias_table(rpb):
    x = np.arange(GRID_W)[:, None]
    c = np.arange(GRID_W)[None, :]
    c0 = np.clip(x - NA_WIN_COLS // 2, 0, GRID_W - NA_WIN_COLS)
    valid = ((c >= c0) & (c < c0 + NA_WIN_COLS)).reshape(-1)
    dc = np.clip(c - x + (NA_WIN_COLS - 1), 0, 2 * NA_WIN_COLS - 2).reshape(-1)
    onehot = (dc[None, :] == np.arange(2 * NA_WIN_COLS - 1)[:, None]) & valid[None, :]
    tab = jnp.einsum('hrd,dn->hrn', rpb, jnp.asarray(onehot, F32), precision=lax.Precision.HIGHEST)
    tab = tab + jnp.asarray(np.where(valid, 0.0, NEG_BIG), F32)
    tab = tab.reshape(NA_HEADS, 2 * NA_WIN_ROWS - 1, GRID_W, GRID_W)
    return jnp.concatenate([tab[:, :-1], tab[:, 1:]], axis=-1)


def _na_attention(z, bias_tab, bsz, seq):
    rows = seq // GRID_W
    assert rows >= NA_BAND_ROWS and rows % NA_ROW_BLOCK == 0
    z4 = z.reshape(bsz, rows, GRID_W, Z_COLS)
    half = NA_WIN_ROWS // 2

    def kv_spec(col):
        return pl.BlockSpec(
            (pl.Squeezed(), pl.Element(NA_BAND_ROWS), pl.Element(GRID_W), pl.Element(NA_WIDTH)),
            lambda b, i: (b, jnp.clip(i * NA_ROW_BLOCK - half, 0, rows - NA_BAND_ROWS), 0, col))

    qblk = Z_NA // NA_WIDTH
    out = pl.pallas_call(
        functools.partial(_na_kernel, rows=rows),
        grid=(bsz, rows // NA_ROW_BLOCK),
        in_specs=[
            pl.BlockSpec((None, NA_ROW_BLOCK, GRID_W, NA_WIDTH), lambda b, i: (b, i, 0, qblk)),
            kv_spec(Z_NA + NA_WIDTH),
            kv_spec(Z_NA + 2 * NA_WIDTH),
            pl.BlockSpec(bias_tab.shape, lambda b, i: (0, 0, 0, 0)),
        ],
        out_specs=pl.BlockSpec((None, NA_ROW_BLOCK, GRID_W, NA_WIDTH), lambda b, i: (b, i, 0, 0)),
        out_shape=jax.ShapeDtypeStruct((bsz, rows, GRID_W, NA_WIDTH), BF16),
        scratch_shapes=[pltpu.VMEM((NA_BAND_ROWS, GRID_W, NA_WIDTH), BF16)] * 2,
        compiler_params=_cparams(("parallel", "arbitrary")),
        name="na_attn",
    )(z4, z4, z4, bias_tab)
    return out.reshape(bsz * seq, NA_WIDTH)


def _softplus(u):
    return jnp.maximum(u, 0.0) + jnp.log1p(jnp.exp(-jnp.abs(u)))


def _rwkv_kernel(zc_ref, zp_ref, zn_ref, cw_ref, d0_ref, d2_ref, a0_ref, a2_ref, g2_ref,
                 kk_ref, ka_ref, rk_ref, bones_ref, tri_ref,
                 y_ref, bonus_ref, gate_ref, s_ref, *bufs, tile, nt, reverse):
    i = pl.program_id(1)
    ti = nt - 1 - i if reverse else i

    @pl.when(i == 0)
    def _():
        s_ref[...] = jnp.zeros_like(s_ref)

    _rwkv_prep_tile(zc_ref, zp_ref, zn_ref, cw_ref, d0_ref, d2_ref, a0_ref, a2_ref, g2_ref,
                    kk_ref, ka_ref, rk_ref, bones_ref, tri_ref, bonus_ref, gate_ref, *bufs,
                    tile=tile, nt=nt, ti=ti, reverse=reverse)
    _rwkv_scan_tile(y_ref, s_ref, *bufs, tile=tile, reverse=reverse)


def _rwkv_prep_tile(zc_ref, zp_ref, zn_ref, cw_ref, d0_ref, d2_ref, a0_ref, a2_ref, g2_ref,
                    kk_ref, ka_ref, rk_ref, bones_ref, tri_ref, bonus_ref, gate_ref,
                    rt_ref, at_ref, bt_ref, kt_ref, bh_ref, kh_ref, v_ref, tot_ref,
                    *, tile, nt, ti, reverse):
    n_chunks = tile // CHUNK

    zc = zc_ref[...].astype(F32)
    prev_row = jnp.where(ti == 0, 0.0, zp_ref[...].astype(F32)[7:8, :])
    next_row = jnp.where(ti == nt - 1, 0.0, zn_ref[...].astype(F32)[0:1, :])
    zf = (pltpu.roll(zc, 1, axis=0) * cw_ref[0:1, :] + zc * cw_ref[1:2, :]
          + pltpu.roll(zc, tile - 1, axis=0) * cw_ref[2:3, :])
    rows8 = lax.broadcasted_iota(jnp.int32, (8, 1), 0)
    fix_first = jnp.where(rows8 == 0, (prev_row - zc[tile - 1:tile, :]) * cw_ref[0:1, :], 0.0)
    fix_last = jnp.where(rows8 == 7, (next_row - zc[0:1, :]) * cw_ref[2:3, :], 0.0)
    zf = jnp.concatenate([zf[0:8] + fix_first, zf[8:tile - 8], zf[tile - 8:tile] + fix_last], axis=0)

    o1, o2, o3 = RW_WIDTH, 2 * RW_WIDTH, 3 * RW_WIDTH
    o4 = o3 + 2 * DECAY_LORA
    o5 = o4 + 2 * AAA_LORA
    r = zf[:, 0:o1]
    k = zf[:, o1:o2]
    v = zf[:, o2:o3]
    xw = zf[:, o3:o4]
    xa = zf[:, o4:o5]
    xg = zf[:, o5:RW_COLS_PAD]

    bones = bones_ref[...]
    kk = k * kk_ref[...]
    kk = kk * lax.rsqrt(jnp.maximum(_dot((kk * kk).astype(BF16), bones), 1e-24))
    gate_ref[...] = _dot(jax.nn.sigmoid(xg).astype(BF16), g2_ref[...]).astype(gate_ref.dtype)

    w_log = -_softplus(-(d0_ref[...] + _dot(jnp.tanh(xw).astype(BF16), d2_ref[...]))) - 0.5
    logw = -jnp.exp(w_log)
    a = jax.nn.sigmoid(a0_ref[...] + _dot(xa.astype(BF16), a2_ref[...]))
    k_d = k * (1.0 + (a - 1.0) * ka_ref[...])
    b = kk * a
    bonus_ref[...] = (_dot((r * k_d * rk_ref[...]).astype(BF16), bones) * v).astype(bonus_ref.dtype)

    span = tri_ref.shape[0]
    g = jnp.concatenate([_sel_left(tri_ref[...], logw[r:r + span]) for r in range(0, tile, span)], axis=0)
    last = 0 if reverse else CHUNK - 1
    tot = jnp.concatenate(
        [jnp.broadcast_to(g[c * CHUNK + last:c * CHUNK + last + 1, :], (CHUNK, RW_WIDTH))
         for c in range(n_chunks)], axis=0)
    eng = jnp.exp(-g)
    ehat = jnp.exp(tot - g)
    rt_ref[...] = r * jnp.exp(g)
    at_ref[...] = -kk * jnp.exp(g - logw)
    bt_ref[...] = b * eng
    kt_ref[...] = k_d * eng
    bh_ref[...] = b * ehat
    kh_ref[...] = k_d * ehat
    v_ref[...] = v
    tot_ref[...] = tot


def _rwkv_scan_tile(y_ref, s_ref, rt_ref, at_ref, bt_ref, kt_ref, bh_ref, kh_ref, v_ref, tot_ref,
                    *, tile, reverse):
    n_chunks = tile // CHUNK

    ri = lax.broadcasted_iota(jnp.int32, (PAIR, PAIR), 0)
    ci = lax.broadcasted_iota(jnp.int32, (PAIR, PAIR), 1)
    same_head = jnp.where(jnp.right_shift(ri, 6) == jnp.right_shift(ci, 6), 1.0, 0.0).astype(F32)
    order = ci - ri if reverse else ri - ci
    m_strict = jnp.where(order > 0, same_head, 0.0)
    m_incl = jnp.where(order >= 0, same_head, 0.0)
    eye = jnp.where(ci == ri, 1.0, 0.0).astype(F32)
    lvl = []
    for q in range(INV_LEVELS):
        in_block = jnp.where(jnp.right_shift(ri, q + 1) == jnp.right_shift(ci, q + 1), 1.0, 0.0).astype(F32)
        lvl.append(jnp.where(jnp.right_shift(ri, q) != jnp.right_shift(ci, q), in_block, 0.0))
    lane = lax.broadcasted_iota(jnp.int32, (CHUNK, PAIR), 1)
    left = lane < RW_HEAD_DIM

    def stack(x):
        return jnp.concatenate([jnp.where(left, x, 0.0), jnp.where(left, 0.0, x)], axis=0)

    groups = [(c, p) for c in range(n_chunks) for p in range(N_PAIRS)]
    eye_b = eye.astype(BF16)

    def tile_of(ref, c, p):
        return ref[c * CHUNK:(c + 1) * CHUNK, p * PAIR:(p + 1) * PAIR]

    at_b = [stack(tile_of(at_ref, c, p)).astype(BF16) for c, p in groups]
    vs_b = [stack(tile_of(v_ref, c, p)).astype(BF16) for c, p in groups]
    o1_ = []
    for gi, (c, p) in enumerate(groups):
        bt = tile_of(bt_ref, c, p).astype(BF16)
        kt = tile_of(kt_ref, c, p).astype(BF16)
        lhs1 = jnp.concatenate([at_b[gi], stack(tile_of(rt_ref, c, p)).astype(BF16)], axis=0)
        o1_.append(_dot_nt(lhs1, jnp.concatenate([bt, bt, kt, kt], axis=0)))
    ms_b = m_strict.astype(BF16)
    mi_b = m_incl.astype(BF16)
    lvl_b = [m.astype(BF16) for m in lvl]
    n_b = [o[0:PAIR, 0:PAIR].astype(BF16) * ms_b for o in o1_]
    a_ak = [o[0:PAIR, PAIR:].astype(BF16) * ms_b for o in o1_]
    a_r = [jnp.concatenate([o[PAIR:, 0:PAIR].astype(BF16) * mi_b, o[PAIR:, PAIR:].astype(BF16) * mi_b], axis=1)
           for o in o1_]

    t_inv = [eye.astype(BF16) + n * lvl_b[0] for n in n_b]
    for q in range(1, INV_LEVELS):
        blk = 1 << q
        if blk < 16:
            xq = [_dot(n * lvl_b[q], t) for n, t in zip(n_b, t_inv)]
            t_inv = [t + _dot(t, x.astype(BF16)).astype(BF16) for t, x in zip(t_inv, xq)]
            continue
        n_blocks = PAIR // blk
        late = [k for k in range(n_blocks) if (k % 2 == 0) == reverse]

        def take(x):
            return jnp.concatenate([x[k * blk:(k + 1) * blk] for k in late], axis=0)

        def spread(x):
            zero = jnp.zeros((blk, PAIR), x.dtype)
            return jnp.concatenate(
                [x[late.index(k) * blk:(late.index(k) + 1) * blk] if k in late else zero
                 for k in range(n_blocks)], axis=0)

        xq = [_dot(take(n) * take(lvl_b[q]), t) for n, t in zip(n_b, t_inv)]
        upd = [_dot(take(t), spread(x.astype(BF16))) for t, x in zip(t_inv, xq)]
        t_inv = [t + spread(u.astype(BF16)) for t, u in zip(t_inv, upd)]

    av = [_dot(a, v_) for a, v_ in zip(a_ak, vs_b)]
    pu = [_dot(t, jnp.concatenate([a, x.astype(BF16)], axis=1))
          for t, a, x in zip(t_inv, at_b, av)]
    zero_b = jnp.zeros((PAIR, PAIR), BF16)
    qb, y_loc, n_st = [], [], []
    for gi, (c, p) in enumerate(groups):
        bh_t = stack(tile_of(bh_ref, c, p)).T
        kh_t = stack(tile_of(kh_ref, c, p)).T
        wmat = jnp.concatenate(
            [pu[gi].astype(BF16), jnp.concatenate([zero_b, vs_b[gi]], axis=1)], axis=0)
        lhs2 = jnp.concatenate([a_r[gi], jnp.concatenate([bh_t, kh_t], axis=1).astype(BF16)], axis=0)
        o2_ = _dot(lhs2, wmat)
        q_mat = stack(tile_of(rt_ref, c, p)) + o2_[0:PAIR, 0:PAIR]
        qb.append(jnp.concatenate([q_mat, o2_[PAIR:, 0:PAIR]], axis=0).astype(BF16))
        y_loc.append(o2_[0:PAIR, PAIR:])
        n_st.append(o2_[PAIR:, PAIR:])

    rep = PAIR // n_chunks
    w_tot = []
    for p in range(N_PAIRS):
        t_rows = jnp.concatenate(
            [tot_ref[c * CHUNK:c * CHUNK + rep, p * PAIR:(p + 1) * PAIR] for c in range(n_chunks)], axis=0)
        hi = t_rows.astype(BF16)
        r1 = t_rows - hi.astype(F32)
        mid = r1.astype(BF16)
        lo = (r1 - mid.astype(F32)).astype(BF16)
        cols = _dot_nt(eye_b, hi) + _dot_nt(eye_b, mid) + _dot_nt(eye_b, lo)
        w_tot.append([jnp.exp(cols[:, c * rep:c * rep + 1]) for c in range(n_chunks)])

    st = [s_ref[p] for p in range(N_PAIRS)]
    for c in (reversed(range(n_chunks)) if reverse else range(n_chunks)):
        o3_ = [_dot(qb[c * N_PAIRS + p], st[p].astype(BF16)) for p in range(N_PAIRS)]
        for p in range(N_PAIRS):
            gi = c * N_PAIRS + p
            y_bs = o3_[p][0:PAIR] + y_loc[gi]
            st[p] = st[p] * w_tot[p][c] + o3_[p][PAIR:] + n_st[gi]
            y_ref[c * CHUNK:(c + 1) * CHUNK, p * PAIR:(p + 1) * PAIR] = y_bs[0:CHUNK] + y_bs[CHUNK:]
    for p in range(N_PAIRS):
        s_ref[p] = st[p]


def _rwkv_scan(z, lp, bsz, seq, reverse):
    tile = min(512, seq)
    nt = seq // tile
    z3 = z.reshape(bsz, seq, Z_COLS)
    t8 = tile // 8
    d = int(reverse)

    def tidx(i):
        return nt - 1 - i if reverse else i

    span = min(256, tile)
    cidx = np.arange(span)
    same = (cidx[:, None] // CHUNK) == (cidx[None, :] // CHUNK)
    upto = cidx[None, :] >= cidx[:, None] if reverse else cidx[None, :] <= cidx[:, None]
    tri = jnp.asarray(same & upto, BF16)

    def full(shape):
        return pl.BlockSpec(shape, lambda b, i: (0,) * len(shape))

    def per_dir(shape):
        return pl.BlockSpec((None,) + shape, lambda b, i: (d,) + (0,) * len(shape))

    out_spec = pl.BlockSpec((None, tile, RW_WIDTH), lambda b, i: (b, tidx(i), 0))
    nat = pltpu.VMEM((tile, RW_WIDTH), F32)
    return pl.pallas_call(
        functools.partial(_rwkv_kernel, tile=tile, nt=nt, reverse=reverse),
        grid=(bsz, nt),
        in_specs=[
            pl.BlockSpec((None, tile, RW_COLS_PAD), lambda b, i: (b, tidx(i), 0)),
            pl.BlockSpec((None, 8, RW_COLS_PAD), lambda b, i: (b, jnp.maximum(tidx(i) * t8 - 1, 0), 0)),
            pl.BlockSpec((None, 8, RW_COLS_PAD),
                         lambda b, i: (b, jnp.minimum((tidx(i) + 1) * t8, seq // 8 - 1), 0)),
            full((3, RW_COLS_PAD)),
            per_dir((1, RW_WIDTH)), per_dir((2 * DECAY_LORA, RW_WIDTH)),
            per_dir((1, RW_WIDTH)), per_dir((2 * AAA_LORA, RW_WIDTH)),
            full((GATE_LORA_PAD, RW_WIDTH)),
            full((1, RW_WIDTH)), full((1, RW_WIDTH)), full((1, RW_WIDTH)),
            full((RW_WIDTH, RW_WIDTH)),
            full((span, span)),
        ],
        out_specs=[out_spec, out_spec, out_spec],
        out_shape=[jax.ShapeDtypeStruct((bsz, seq, RW_WIDTH), F32),
                   jax.ShapeDtypeStruct((bsz, seq, RW_WIDTH), BF16),
                   jax.ShapeDtypeStruct((bsz, seq, RW_WIDTH), BF16)],
        scratch_shapes=[pltpu.VMEM((N_PAIRS, PAIR, PAIR), F32)] + [nat] * 8,
        compiler_params=_cparams(("arbitrary", "arbitrary")),
        name="rwkv_bwd" if reverse else "rwkv_fwd",
    )(z3, z3, z3, lp["rw_conv"], lp["rw_decay0"], lp["rw_decay2"], lp["rw_a0"], lp["rw_a2"],
      lp["rw_g2"], lp["rw_k_k"], lp["rw_k_a"], lp["rw_r_k"], lp["bones"], tri)


def _memkv_kernel(m_ref, g_ref, w_ref, o_ref):
    o_ref[...] = _dot(_rms(m_ref[...], g_ref[...]).astype(BF16), w_ref[...]).astype(o_ref.dtype)


def _mem_kv(mem, gain, w):
    bsz = mem.shape[0]
    return pl.pallas_call(
        _memkv_kernel,
        grid=(bsz,),
        in_specs=[
            pl.BlockSpec((None, MEM_TOKENS, D_MODEL), lambda b: (b, 0, 0)),
            pl.BlockSpec((1, D_MODEL), lambda b: (0, 0)),
            pl.BlockSpec((D_MODEL, 2 * MEM_WIDTH), lambda b: (0, 0)),
        ],
        out_specs=pl.BlockSpec((None, MEM_TOKENS, 2 * MEM_WIDTH), lambda b: (b, 0, 0)),
        out_shape=jax.ShapeDtypeStruct((bsz, MEM_TOKENS, 2 * MEM_WIDTH), BF16),
        compiler_params=_cparams(("parallel",)),
        name="mem_kv",
    )(mem, gain, w)


def _memattn_kernel(q_ref, kv_ref, o_ref):
    scale = MEM_HEAD_DIM ** -0.5
    outs = []
    for h in range(MEM_HEADS):
        sl = slice(h * MEM_HEAD_DIM, (h + 1) * MEM_HEAD_DIM)
        vsl = slice(MEM_WIDTH + h * MEM_HEAD_DIM, MEM_WIDTH + (h + 1) * MEM_HEAD_DIM)
        s = _dot_nt(q_ref[:, sl], kv_ref[:, sl]) * scale
        m = jnp.max(s, axis=-1, keepdims=True)
        p = jnp.exp(s - m)
        l = jnp.sum(p, axis=-1, keepdims=True)
        outs.append(_dot(p.astype(BF16), kv_ref[:, vsl]) / l)
    o_ref[...] = jnp.concatenate(outs, axis=-1).astype(o_ref.dtype)


def _mem_attention(z, kv, bsz, seq):
    tm = min(1024, seq)
    z3 = z.reshape(bsz, seq, Z_COLS)
    out = pl.pallas_call(
        _memattn_kernel,
        grid=(bsz, seq // tm),
        in_specs=[
            pl.BlockSpec((None, tm, MEM_WIDTH), lambda b, i: (b, i, Z_MEM // MEM_WIDTH)),
            pl.BlockSpec((None, MEM_TOKENS, 2 * MEM_WIDTH), lambda b, i: (b, 0, 0)),
        ],
        out_specs=pl.BlockSpec((None, tm, MEM_WIDTH), lambda b, i: (b, i, 0)),
        out_shape=jax.ShapeDtypeStruct((bsz, seq, MEM_WIDTH), BF16),
        compiler_params=_cparams(("parallel", "parallel")),
        name="mem_attn",
    )(z3, kv)
    return out.reshape(bsz * seq, MEM_WIDTH)


def _merge_kernel(x_ref, ona_ref, y0_ref, y1_ref, b0_ref, b1_ref, gt_ref, omem_ref,
                  g0_ref, g1_ref, g2_ref, wb_ref, wo_ref, lw_ref, lb_ref, bones_ref, o_ref):
    bones = bones_ref[...]
    inv_n = 1.0 / RW_HEAD_DIM
    y = y0_ref[...] + y1_ref[...]
    mu = _dot(y.astype(BF16), bones) * inv_n
    dl = y - mu
    var = _dot((dl * dl).astype(BF16), bones) * inv_n
    yn = dl * lax.rsqrt(var + GN_EPS)
    bonus = b0_ref[...].astype(F32) + b1_ref[...].astype(F32)
    o_rw = (yn * lw_ref[...] + lb_ref[...] + bonus) * gt_ref[...].astype(F32)

    merged = jax.nn.sigmoid(g0_ref[...].astype(F32)) * _dot(ona_ref[...], wb_ref[0])
    merged = merged + jax.nn.sigmoid(g1_ref[...].astype(F32)) * _dot(o_rw.astype(BF16), wb_ref[1])
    merged = merged + jax.nn.sigmoid(g2_ref[...].astype(F32)) * _dot(omem_ref[...], wb_ref[2])
    o_ref[...] = x_ref[...] + _dot(merged.astype(BF16), wo_ref[...])


def _merge(x2d, z, o_na, rw_fwd, rw_bwd, o_mem, lp):
    t = x2d.shape[0]
    tm = min(512, t)
    y0, b0, gate = (a.reshape(t, RW_WIDTH) for a in rw_fwd)
    y1, b1, _ = (a.reshape(t, RW_WIDTH) for a in rw_bwd)
    gblk = Z_GATE // D_MODEL

    def tok(width):
        return pl.BlockSpec((tm, width), lambda i: (i, 0))

    def const(shape):
        return pl.BlockSpec(shape, lambda i: (0,) * len(shape))

    return pl.pallas_call(
        _merge_kernel,
        grid=(t // tm,),
        in_specs=[
            tok(D_MODEL), tok(NA_WIDTH), tok(RW_WIDTH), tok(RW_WIDTH), tok(RW_WIDTH), tok(RW_WIDTH), tok(RW_WIDTH),
            tok(MEM_WIDTH),
            pl.BlockSpec((tm, D_MODEL), lambda i: (i, gblk)),
            pl.BlockSpec((tm, D_MODEL), lambda i: (i, gblk + 1)),
            pl.BlockSpec((tm, D_MODEL), lambda i: (i, gblk + 2)),
            const((N_BRANCH, NA_WIDTH, D_MODEL)), const((D_MODEL, D_MODEL)),
            const((1, RW_WIDTH)), const((1, RW_WIDTH)), const((RW_WIDTH, RW_WIDTH)),
        ],
        out_specs=tok(D_MODEL),
        out_shape=jax.ShapeDtypeStruct((t, D_MODEL), F32),
        compiler_params=_cparams(("parallel",)),
        name="merge",
    )(x2d, o_na, y0, y1, b0, b1, gate, o_mem, z, z, z, lp["w_branch"], lp["w_out"],
      lp["rw_lnx_w"], lp["rw_lnx_b"], lp["bones"])


def _ffn_kernel(x_ref, xp_ref, xn_ref, g_ref, wv_ref, wg_ref, cv_ref, cg_ref, bv_ref, bg_ref, wd_ref,
                fg_ref, o_ref, h_ref, acc_ref, *, tm, tiles_per_seq, final_norm):
    i = pl.program_id(0)
    j = pl.program_id(1)
    ext = tm + 16

    @pl.when(j == 0)
    def _():
        gain = g_ref[...]
        seq_pos = i % tiles_per_seq
        h_ref[0:8, :] = jnp.where(seq_pos == 0, 0.0, _rms(xp_ref[...], gain)).astype(BF16)
        h_ref[8:8 + tm, :] = _rms(x_ref[...], gain).astype(BF16)
        h_ref[8 + tm:ext, :] = jnp.where(seq_pos == tiles_per_seq - 1, 0.0, _rms(xn_ref[...], gain)).astype(BF16)
        acc_ref[...] = jnp.zeros_like(acc_ref)

    hx = h_ref[...]

    def conv(u, cw, bias):
        return (pltpu.roll(u, 1, axis=0)[8:8 + tm] * cw[0:1, :] + u[8:8 + tm] * cw[1:2, :]
                + pltpu.roll(u, ext - 1, axis=0)[8:8 + tm] * cw[2:3, :] + bias)

    u_val = conv(_dot(hx, wv_ref[...]), cv_ref[...], bv_ref[...])
    u_gate = conv(_dot(hx, wg_ref[...]), cg_ref[...], bg_ref[...])
    act = (u_gate * jax.nn.sigmoid(u_gate) * u_val).astype(BF16)
    acc_ref[...] += _dot(act, wd_ref[...])

    @pl.when(j == pl.num_programs(1) - 1)
    def _():
        out = x_ref[...] + acc_ref[...]
        if final_norm:
            out = _rms(out, fg_ref[...])
        o_ref[...] = out


def _ffn(x2d, lp, final_gain, seq, final_norm):
    t = x2d.shape[0]
    tm = min(2048, seq)
    tn = 256
    nj = D_FF // tn
    t8 = tm // 8
    return pl.pallas_call(
        functools.partial(_ffn_kernel, tm=tm, tiles_per_seq=seq // tm, final_norm=final_norm),
        grid=(t // tm, nj),
        in_specs=[
            pl.BlockSpec((tm, D_MODEL), lambda i, j: (i, 0)),
            pl.BlockSpec((8, D_MODEL), lambda i, j: (jnp.maximum(i * t8 - 1, 0), 0)),
            pl.BlockSpec((8, D_MODEL), lambda i, j: (jnp.minimum((i + 1) * t8, t // 8 - 1), 0)),
            pl.BlockSpec((1, D_MODEL), lambda i, j: (0, 0)),
            pl.BlockSpec((D_MODEL, tn), lambda i, j: (0, j)),
            pl.BlockSpec((D_MODEL, tn), lambda i, j: (0, nj + j)),
            pl.BlockSpec((3, tn), lambda i, j: (0, j)),
            pl.BlockSpec((3, tn), lambda i, j: (0, nj + j)),
            pl.BlockSpec((1, tn), lambda i, j: (0, j)),
            pl.BlockSpec((1, tn), lambda i, j: (0, nj + j)),
            pl.BlockSpec((tn, D_MODEL), lambda i, j: (j, 0)),
            pl.BlockSpec((1, D_MODEL), lambda i, j: (0, 0)),
        ],
        out_specs=pl.BlockSpec((tm, D_MODEL), lambda i, j: (i, 0)),
        out_shape=jax.ShapeDtypeStruct((t, D_MODEL), F32),
        scratch_shapes=[pltpu.VMEM((tm + 16, D_MODEL), BF16), pltpu.VMEM((tm, D_MODEL), F32)],
        compiler_params=_cparams(("parallel", "arbitrary")),
        name="ffn",
    )(x2d, x2d, x2d, lp["ffn_norm"], lp["w_up"], lp["w_up"], lp["ffn_conv"], lp["ffn_conv"],
      lp["ffn_conv_b"], lp["ffn_conv_b"], lp["w_down"], final_gain)


def _prep_layer(l, attn_norm, w_in, na_rpb, rw_conv, rw_decay0, rw_decay2, rw_a0, rw_a2, rw_g2, rw_k_k,
                rw_k_a, rw_r_k, rw_lnx_w, rw_lnx_b, mem_norm, w_mem_kv, w_branch, w_out, ffn_norm, w_up,
                ffn_conv, ffn_conv_b, w_down):
    c1 = 3 * NA_WIDTH
    c2 = c1 + RW_COLS
    w = w_in[l]
    pad_cols = RW_COLS_PAD - RW_COLS
    w_new = jnp.concatenate(
        [w[:, c1:c2], jnp.zeros((D_MODEL, pad_cols), w.dtype), w[:, :c1], w[:, c2:]], axis=1)

    def lora_rows(m):
        zero = jnp.zeros_like(m[0])
        return jnp.stack([jnp.concatenate([m[0], zero], 0), jnp.concatenate([zero, m[1]], 0)]).astype(BF16)

    head = np.arange(RW_WIDTH) // RW_HEAD_DIM
    return dict(
        attn_norm=attn_norm[l][None, :],
        w_in=w_new.astype(BF16),
        na_bias=_na_bias_table(na_rpb[l]),
        rw_conv=jnp.pad(rw_conv[l], ((0, 0), (0, pad_cols))),
        rw_decay0=rw_decay0[l][:, None, :],
        rw_decay2=lora_rows(rw_decay2[l]),
        rw_a0=rw_a0[l][:, None, :],
        rw_a2=lora_rows(rw_a2[l]),
        rw_g2=jnp.pad(rw_g2[l], ((0, GATE_LORA_PAD - GATE_LORA), (0, 0))).astype(BF16),
        rw_k_k=rw_k_k[l][None, :],
        rw_k_a=rw_k_a[l][None, :],
        rw_r_k=rw_r_k[l].reshape(1, RW_WIDTH),
        rw_lnx_w=rw_lnx_w[l][None, :],
        rw_lnx_b=rw_lnx_b[l][None, :],
        bones=jnp.asarray(head[:, None] == head[None, :], BF16),
        mem_norm=mem_norm[l][None, :],
        w_mem_kv=w_mem_kv[l].astype(BF16),
        w_branch=w_branch[l].astype(BF16),
        w_out=w_out[l].astype(BF16),
        ffn_norm=ffn_norm[l][None, :],
        w_up=w_up[l].astype(BF16),
        ffn_conv=ffn_conv[l],
        ffn_conv_b=ffn_conv_b[l][None, :],
        w_down=w_down[l].astype(BF16),
    )


def _layer(x2d, mem, lp, bsz, seq, final_gain, final_norm):
    z = _in_proj(x2d, lp["attn_norm"], lp["w_in"])
    o_na = _na_attention(z, lp["na_bias"], bsz, seq)
    rw_fwd = _rwkv_scan(z, lp, bsz, seq, reverse=False)
    rw_bwd = _rwkv_scan(z, lp, bsz, seq, reverse=True)
    o_mem = _mem_attention(z, _mem_kv(mem, lp["mem_norm"], lp["w_mem_kv"]), bsz, seq)
    x2d = _merge(x2d, z, o_na, rw_fwd, rw_bwd, o_mem, lp)
    return _ffn(x2d, lp, final_gain, seq, final_norm)


def _trunk(x, mem, layers, final_gain):
    bsz, seq, _ = x.shape
    x2d = x.reshape(bsz * seq, D_MODEL)
    for l, lp in enumerate(layers):
        x2d = _layer(x2d, mem, lp, bsz, seq, final_gain, l == len(layers) - 1)
    return x2d.reshape(bsz, seq, D_MODEL)


def kernel(x_prompt, x_sample, mem_prompt, mem_sample, attn_norm, w_in, na_rpb, rw_conv, rw_decay0, rw_decay2, rw_a0, rw_a2, rw_g2, rw_k_k, rw_k_a, rw_r_k, rw_lnx_w, rw_lnx_b, mem_norm, w_mem_kv, w_branch, w_out, ffn_norm, w_up, ffn_conv, ffn_conv_b, w_down, final_norm):
    params = (attn_norm, w_in, na_rpb, rw_conv, rw_decay0, rw_decay2, rw_a0, rw_a2, rw_g2, rw_k_k, rw_k_a,
              rw_r_k, rw_lnx_w, rw_lnx_b, mem_norm, w_mem_kv, w_branch, w_out, ffn_norm, w_up, ffn_conv,
              ffn_conv_b, w_down)
    layers = [_prep_layer(l, *params) for l in range(attn_norm.shape[0])]
    final_gain = final_norm[None, :]
    y_prompt = _trunk(x_prompt, mem_prompt, layers, final_gain)
    y_sample = _trunk(x_sample, mem_sample, layers, final_gain)
    return (y_prompt, y_sample)
```

```python
import functools

import jax
import jax.numpy as jnp
import numpy as np
from jax import lax
from jax.experimental import pallas as pl
from jax.experimental.pallas import tpu as pltpu

F32 = jnp.float32
BF16 = jnp.bfloat16

D_MODEL = 1024
GRID_W = 64
NA_HEADS = 8
NA_HEAD_DIM = 64
NA_WIDTH = 512
NA_WIN_ROWS = 8
NA_WIN_COLS = 16
RW_HEADS = 8
RW_HEAD_DIM = 64
RW_WIDTH = 512
DECAY_LORA = 64
AAA_LORA = 64
GATE_LORA = 160
GATE_LORA_PAD = 256
MEM_TOKENS = 256
MEM_HEADS = 4
MEM_HEAD_DIM = 128
MEM_WIDTH = 512
N_BRANCH = 3
D_FF = 2816
RMS_EPS = 1e-6
GN_EPS = 64e-5

RW_COLS = 3 * RW_WIDTH + 2 * DECAY_LORA + 2 * AAA_LORA + GATE_LORA
RW_COLS_PAD = 3 * RW_WIDTH + 2 * DECAY_LORA + 2 * AAA_LORA + GATE_LORA_PAD
Z_RW = 0
Z_NA = RW_COLS_PAD
Z_MEM = Z_NA + 3 * NA_WIDTH
Z_GATE = Z_MEM + MEM_WIDTH
Z_COLS = Z_GATE + N_BRANCH * D_MODEL

CHUNK = 64
PAIR = 2 * RW_HEAD_DIM
N_PAIRS = RW_HEADS // 2
INV_LEVELS = 6
NEG_BIG = -1e30
VMEM_LIMIT = 56 * 1024 * 1024


def _cparams(sem):
    return pltpu.CompilerParams(dimension_semantics=sem, vmem_limit_bytes=VMEM_LIMIT)


def _rms(x, gain):
    ms = jnp.mean(x * x, axis=-1, keepdims=True)
    return x * lax.rsqrt(ms + RMS_EPS) * gain


def _dot(a, b):
    return jnp.dot(a, b, preferred_element_type=F32)


def _dot_nt(a, b):
    return lax.dot_general(a, b, (((1,), (1,)), ((), ())), preferred_element_type=F32)


def _split_hi_lo(x):
    hi = x.astype(BF16)
    lo = (x - hi.astype(F32)).astype(BF16)
    return hi, lo


def _sel_left(sel, x):
    hi, lo = _split_hi_lo(x)
    return _dot(sel, hi) + _dot(sel, lo)


def _sel_right(x, sel):
    hi, lo = _split_hi_lo(x)
    return _dot(hi, sel) + _dot(lo, sel)


def _inproj_kernel(x_ref, g_ref, w_ref, o_ref, h_ref):
    @pl.when(pl.program_id(1) == 0)
    def _():
        h_ref[...] = _rms(x_ref[...], g_ref[...]).astype(BF16)

    o_ref[...] = _dot(h_ref[...], w_ref[...]).astype(o_ref.dtype)


def _in_proj(x2d, gain, w):
    t, n = x2d.shape[0], w.shape[1]
    tm = min(2048, t)
    tn = 1792
    return pl.pallas_call(
        _inproj_kernel,
        grid=(t // tm, n // tn),
        in_specs=[
            pl.BlockSpec((tm, D_MODEL), lambda i, j: (i, 0)),
            pl.BlockSpec((1, D_MODEL), lambda i, j: (0, 0)),
            pl.BlockSpec((D_MODEL, tn), lambda i, j: (0, j)),
        ],
        out_specs=pl.BlockSpec((tm, tn), lambda i, j: (i, j)),
        out_shape=jax.ShapeDtypeStruct((t, n), BF16),
        scratch_shapes=[pltpu.VMEM((tm, D_MODEL), BF16)],
        compiler_params=_cparams(("parallel", "arbitrary")),
        name="in_proj",
    )(x2d, gain, w)


NA_ROW_BLOCK = 16
NA_BAND_ROWS = NA_ROW_BLOCK + NA_WIN_ROWS - 1


def _na_kernel(q_ref, k_ref, v_ref, t2_ref, o_ref, vl_ref, vr_ref, *, rows):
    half = NA_WIN_ROWS // 2
    band = NA_WIN_ROWS * GRID_W
    i0 = pl.program_id(1) * NA_ROW_BLOCK
    block_start = jnp.clip(i0 - half, 0, rows - NA_BAND_ROWS)
    scale = NA_HEAD_DIM ** -0.5

    lane = lax.broadcasted_iota(jnp.int32, (GRID_W, NA_WIDTH), 1)
    first = jnp.where(lane % PAIR < NA_HEAD_DIM, 1.0, 0.0).astype(BF16)
    second = jnp.where(lane % PAIR < NA_HEAD_DIM, 0.0, 1.0).astype(BF16)
    for r in range(NA_BAND_ROWS):
        v_row = v_ref[r]
        vl_ref[r] = v_row * first
        vr_ref[r] = v_row * second

    rows_per_iter = 8

    def rows_step(it, carry):
        s, v_sel = [], []
        for u in range(rows_per_iter):
            m = it * rows_per_iter + u
            i = i0 + m
            r0 = jnp.clip(i - half, 0, rows - NA_WIN_ROWS)
            off = r0 - block_start
            dr0 = (NA_WIN_ROWS - 1) - (i - r0)
            q = q_ref[m] * scale
            q_sel = (q * first, q * second)
            kb = k_ref[pl.ds(off, NA_WIN_ROWS)].reshape(band, NA_WIDTH)
            v_sel.append((vl_ref[pl.ds(off, NA_WIN_ROWS)].reshape(band, NA_WIDTH),
                          vr_ref[pl.ds(off, NA_WIN_ROWS)].reshape(band, NA_WIDTH)))
            for h in range(NA_HEADS):
                ps = slice((h // 2) * PAIR, (h // 2 + 1) * PAIR)
                bias = jnp.concatenate([t2_ref[h, dr0 + 2 * jj] for jj in range(NA_WIN_ROWS // 2)], axis=1)
                s.append(_dot_nt(q_sel[h % 2][:, ps], kb[:, ps]) + bias)
        mx = [jnp.max(x, axis=-1, keepdims=True) for x in s]
        p = [jnp.exp(x - m_) for x, m_ in zip(s, mx)]
        inv = [1.0 / jnp.sum(x, axis=-1, keepdims=True) for x in p]
        for u in range(rows_per_iter):
            outs = []
            for pp in range(NA_HEADS // 2):
                ps = slice(pp * PAIR, (pp + 1) * PAIR)
                e = u * NA_HEADS + 2 * pp
                o_l = _dot(p[e].astype(BF16), v_sel[u][0][:, ps])
                o_r = _dot(p[e + 1].astype(BF16), v_sel[u][1][:, ps])
                outs.append(o_l * inv[e] + o_r * inv[e + 1])
            o_ref[it * rows_per_iter + u] = jnp.concatenate(outs, axis=-1).astype(o_ref.dtype)
        return carry

    lax.fori_loop(0, NA_ROW_BLOCK // rows_per_iter, rows_step, 0)


def _na_bias_table(rpb):
    x = np.arange(GRID_W)[:, None]
    c = np.arange(GRID_W)[None, :]
    c0 = np.clip(x - NA_WIN_COLS // 2, 0, GRID_W - NA_WIN_COLS)
    valid = ((c >= c0) & (c < c0 + NA_WIN_COLS)).reshape(-1)
    dc = np.clip(c - x + (NA_WIN_COLS - 1), 0, 2 * NA_WIN_COLS - 2).reshape(-1)
    onehot = (dc[None, :] == np.arange(2 * NA_WIN_COLS - 1)[:, None]) & valid[None, :]
    tab = jnp.einsum('hrd,dn->hrn', rpb, jnp.asarray(onehot, F32), precision=lax.Precision.HIGHEST)
    tab = tab + jnp.asarray(np.where(valid, 0.0, NEG_BIG), F32)
    tab = tab.reshape(NA_HEADS, 2 * NA_WIN_ROWS - 1, GRID_W, GRID_W)
    return jnp.concatenate([tab[:, :-1], tab[:, 1:]], axis=-1)


def _na_attention(z, bias_tab, bsz, seq):
    rows = seq // GRID_W
    assert rows >= NA_BAND_ROWS and rows % NA_ROW_BLOCK == 0
    z4 = z.reshape(bsz, rows, GRID_W, Z_COLS)
    half = NA_WIN_ROWS // 2

    def kv_spec(col):
        return pl.BlockSpec(
            (pl.Squeezed(), pl.Element(NA_BAND_ROWS), pl.Element(GRID_W), pl.Element(NA_WIDTH)),
            lambda b, i: (b, jnp.clip(i * NA_ROW_BLOCK - half, 0, rows - NA_BAND_ROWS), 0, col))

    qblk = Z_NA // NA_WIDTH
    out = pl.pallas_call(
        functools.partial(_na_kernel, rows=rows),
        grid=(bsz, rows // NA_ROW_BLOCK),
        in_specs=[
            pl.BlockSpec((None, NA_ROW_BLOCK, GRID_W, NA_WIDTH), lambda b, i: (b, i, 0, qblk)),
            kv_spec(Z_NA + NA_WIDTH),
            kv_spec(Z_NA + 2 * NA_WIDTH),
            pl.BlockSpec(bias_tab.shape, lambda b, i: (0, 0, 0, 0)),
        ],
        out_specs=pl.BlockSpec((None, NA_ROW_BLOCK, GRID_W, NA_WIDTH), lambda b, i: (b, i, 0, 0)),
        out_shape=jax.ShapeDtypeStruct((bsz, rows, GRID_W, NA_WIDTH), BF16),
        scratch_shapes=[pltpu.VMEM((NA_BAND_ROWS, GRID_W, NA_WIDTH), BF16)] * 2,
        compiler_params=_cparams(("parallel", "arbitrary")),
        name="na_attn",
    )(z4, z4, z4, bias_tab)
    return out.reshape(bsz * seq, NA_WIDTH)


def _softplus(u):
    return jnp.maximum(u, 0.0) + jnp.log1p(jnp.exp(-jnp.abs(u)))


def _rwkv_kernel(zc_ref, zp_ref, zn_ref, cw_ref, d0_ref, d2_ref, a0_ref, a2_ref, g2_ref,
                 kk_ref, ka_ref, rk_ref, bones_ref, tri_ref,
                 y_ref, bonus_ref, gate_ref, s_ref, *bufs, tile, nt, reverse):
    i = pl.program_id(1)
    ti = nt - 1 - i if reverse else i

    @pl.when(i == 0)
    def _():
        s_ref[...] = jnp.zeros_like(s_ref)

    _rwkv_prep_tile(zc_ref, zp_ref, zn_ref, cw_ref, d0_ref, d2_ref, a0_ref, a2_ref, g2_ref,
                    kk_ref, ka_ref, rk_ref, bones_ref, tri_ref, bonus_ref, gate_ref, *bufs,
                    tile=tile, nt=nt, ti=ti, reverse=reverse)
    _rwkv_scan_tile(y_ref, s_ref, *bufs, tile=tile, reverse=reverse)


def _rwkv_prep_tile(zc_ref, zp_ref, zn_ref, cw_ref, d0_ref, d2_ref, a0_ref, a2_ref, g2_ref,
                    kk_ref, ka_ref, rk_ref, bones_ref, tri_ref, bonus_ref, gate_ref,
                    rt_ref, at_ref, bt_ref, kt_ref, bh_ref, kh_ref, v_ref, tot_ref,
                    *, tile, nt, ti, reverse):
    n_chunks = tile // CHUNK

    zc = zc_ref[...].astype(F32)
    prev_row = jnp.where(ti == 0, 0.0, zp_ref[...].astype(F32)[7:8, :])
    next_row = jnp.where(ti == nt - 1, 0.0, zn_ref[...].astype(F32)[0:1, :])
    zf = (pltpu.roll(zc, 1, axis=0) * cw_ref[0:1, :] + zc * cw_ref[1:2, :]
          + pltpu.roll(zc, tile - 1, axis=0) * cw_ref[2:3, :])
    rows8 = lax.broadcasted_iota(jnp.int32, (8, 1), 0)
    fix_first = jnp.where(rows8 == 0, (prev_row - zc[tile - 1:tile, :]) * cw_ref[0:1, :], 0.0)
    fix_last = jnp.where(rows8 == 7, (next_row - zc[0:1, :]) * cw_ref[2:3, :], 0.0)
    zf = jnp.concatenate([zf[0:8] + fix_first, zf[8:tile - 8], zf[tile - 8:tile] + fix_last], axis=0)

    o1, o2, o3 = RW_WIDTH, 2 * RW_WIDTH, 3 * RW_WIDTH
    o4 = o3 + 2 * DECAY_LORA
    o5 = o4 + 2 * AAA_LORA
    r = zf[:, 0:o1]
    k = zf[:, o1:o2]
    v = zf[:, o2:o3]
    xw = zf[:, o3:o4]
    xa = zf[:, o4:o5]
    xg = zf[:, o5:RW_COLS_PAD]

    bones = bones_ref[...]
    kk = k * kk_ref[...]
    kk = kk * lax.rsqrt(jnp.maximum(_dot((kk * kk).astype(BF16), bones), 1e-24))
    gate_ref[...] = _dot(jax.nn.sigmoid(xg).astype(BF16), g2_ref[...]).astype(gate_ref.dtype)

    w_log = -_softplus(-(d0_ref[...] + _dot(jnp.tanh(xw).astype(BF16), d2_ref[...]))) - 0.5
    logw = -jnp.exp(w_log)
    a = jax.nn.sigmoid(a0_ref[...] + _dot(xa.astype(BF16), a2_ref[...]))
    k_d = k * (1.0 + (a - 1.0) * ka_ref[...])
    b = kk * a
    bonus_ref[...] = (_dot((r * k_d * rk_ref[...]).astype(BF16), bones) * v).astype(bonus_ref.dtype)

    span = tri_ref.shape[0]
    g = jnp.concatenate([_sel_left(tri_ref[...], logw[r:r + span]) for r in range(0, tile, span)], axis=0)
    last = 0 if reverse else CHUNK - 1
    tot = jnp.concatenate(
        [jnp.broadcast_to(g[c * CHUNK + last:c * CHUNK + last + 1, :], (CHUNK, RW_WIDTH))
         for c in range(n_chunks)], axis=0)
    eng = jnp.exp(-g)
    ehat = jnp.exp(tot - g)
    rt_ref[...] = r * jnp.exp(g)
    at_ref[...] = -kk * jnp.exp(g - logw)
    bt_ref[...] = b * eng
    kt_ref[...] = k_d * eng
    bh_ref[...] = b * ehat
    kh_ref[...] = k_d * ehat
    v_ref[...] = v
    tot_ref[...] = tot


def _rwkv_scan_tile(y_ref, s_ref, rt_ref, at_ref, bt_ref, kt_ref, bh_ref, kh_ref, v_ref, tot_ref,
                    *, tile, reverse):
    n_chunks = tile // CHUNK

    ri = lax.broadcasted_iota(jnp.int32, (PAIR, PAIR), 0)
    ci = lax.broadcasted_iota(jnp.int32, (PAIR, PAIR), 1)
    same_head = jnp.where(jnp.right_shift(ri, 6) == jnp.right_shift(ci, 6), 1.0, 0.0).astype(F32)
    order = ci - ri if reverse else ri - ci
    m_strict = jnp.where(order > 0, same_head, 0.0)
    m_incl = jnp.where(order >= 0, same_head, 0.0)
    eye = jnp.where(ci == ri, 1.0, 0.0).astype(F32)
    lvl = []
    for q in range(INV_LEVELS):
        in_block = jnp.where(jnp.right_shift(ri, q + 1) == jnp.right_shift(ci, q + 1), 1.0, 0.0).astype(F32)
        lvl.append(jnp.where(jnp.right_shift(ri, q) != jnp.right_shift(ci, q), in_block, 0.0))
    lane = lax.broadcasted_iota(jnp.int32, (CHUNK, PAIR), 1)
    left = lane < RW_HEAD_DIM

    def stack(x):
        return jnp.concatenate([jnp.where(left, x, 0.0), jnp.where(left, 0.0, x)], axis=0)

    groups = [(c, p) for c in range(n_chunks) for p in range(N_PAIRS)]
    eye_b = eye.astype(BF16)

    def tile_of(ref, c, p):
        return ref[c * CHUNK:(c + 1) * CHUNK, p * PAIR:(p + 1) * PAIR]

    at_b = [stack(tile_of(at_ref, c, p)).astype(BF16) for c, p in groups]
    vs_b = [stack(tile_of(v_ref, c, p)).astype(BF16) for c, p in groups]
    o1_ = []
    for gi, (c, p) in enumerate(groups):
        bt = tile_of(bt_ref, c, p).astype(BF16)
        kt = tile_of(kt_ref, c, p).astype(BF16)
        lhs1 = jnp.concatenate([at_b[gi], stack(tile_of(rt_ref, c, p)).astype(BF16)], axis=0)
        o1_.append(_dot_nt(lhs1, jnp.concatenate([bt, bt, kt, kt], axis=0)))
    ms_b = m_strict.astype(BF16)
    mi_b = m_incl.astype(BF16)
    lvl_b = [m.astype(BF16) for m in lvl]
    n_b = [o[0:PAIR, 0:PAIR].astype(BF16) * ms_b for o in o1_]
    a_ak = [o[0:PAIR, PAIR:].astype(BF16) * ms_b for o in o1_]
    a_r = [jnp.concatenate([o[PAIR:, 0:PAIR].astype(BF16) * mi_b, o[PAIR:, PAIR:].astype(BF16) * mi_b], axis=1)
           for o in o1_]

    t_inv = [eye.astype(BF16) + n * lvl_b[0] for n in n_b]
    for q in range(1, INV_LEVELS):
        blk = 1 << q
        if blk < 16:
            xq = [_dot(n * lvl_b[q], t) for n, t in zip(n_b, t_inv)]
            t_inv = [t + _dot(t, x.astype(BF16)).astype(BF16) for t, x in zip(t_inv, xq)]
            continue
        n_blocks = PAIR // blk
        late = [k for k in range(n_blocks) if (k % 2 == 0) == reverse]

        def take(x):
            return jnp.concatenate([x[k * blk:(k + 1) * blk] for k in late], axis=0)

        def spread(x):
            zero = jnp.zeros((blk, PAIR), x.dtype)
            return jnp.concatenate(
                [x[late.index(k) * blk:(late.index(k) + 1) * blk] if k in late else zero
                 for k in range(n_blocks)], axis=0)

        xq = [_dot(take(n) * take(lvl_b[q]), t) for n, t in zip(n_b, t_inv)]
        upd = [_dot(take(t), spread(x.astype(BF16))) for t, x in zip(t_inv, xq)]
        t_inv = [t + spread(u.astype(BF16)) for t, u in zip(t_inv, upd)]

    av = [_dot(a, v_) for a, v_ in zip(a_ak, vs_b)]
    pu = [_dot(t, jnp.concatenate([a, x.astype(BF16)], axis=1))
          for t, a, x in zip(t_inv, at_b, av)]
    zero_b = jnp.zeros((PAIR, PAIR), BF16)
    qb, y_loc, n_st = [], [], []
    for gi, (c, p) in enumerate(groups):
        bh_t = stack(tile_of(bh_ref, c, p)).T
        kh_t = stack(tile_of(kh_ref, c, p)).T
        wmat = jnp.concatenate(
            [pu[gi].astype(BF16), jnp.concatenate([zero_b, vs_b[gi]], axis=1)], axis=0)
        lhs2 = jnp.concatenate([a_r[gi], jnp.concatenate([bh_t, kh_t], axis=1).astype(BF16)], axis=0)
        o2_ = _dot(lhs2, wmat)
        q_mat = stack(tile_of(rt_ref, c, p)) + o2_[0:PAIR, 0:PAIR]
        qb.append(jnp.concatenate([q_mat, o2_[PAIR:, 0:PAIR]], axis=0).astype(BF16))
        y_loc.append(o2_[0:PAIR, PAIR:])
        n_st.append(o2_[PAIR:, PAIR:])

    rep = PAIR // n_chunks
    w_tot = []
    for p in range(N_PAIRS):
        t_rows = jnp.concatenate(
            [tot_ref[c * CHUNK:c * CHUNK + rep, p * PAIR:(p + 1) * PAIR] for c in range(n_chunks)], axis=0)
        hi = t_rows.astype(BF16)
        r1 = t_rows - hi.astype(F32)
        mid = r1.astype(BF16)
        lo = (r1 - mid.astype(F32)).astype(BF16)
        cols = _dot_nt(eye_b, hi) + _dot_nt(eye_b, mid) + _dot_nt(eye_b, lo)
        w_tot.append([jnp.exp(cols[:, c * rep:c * rep + 1]) for c in range(n_chunks)])

    st = [s_ref[p] for p in range(N_PAIRS)]
    for c in (reversed(range(n_chunks)) if reverse else range(n_chunks)):
        o3_ = [_dot(qb[c * N_PAIRS + p], st[p].astype(BF16)) for p in range(N_PAIRS)]
        for p in range(N_PAIRS):
            gi = c * N_PAIRS + p
            y_bs = o3_[p][0:PAIR] + y_loc[gi]
            st[p] = st[p] * w_tot[p][c] + o3_[p][PAIR:] + n_st[gi]
            y_ref[c * CHUNK:(c + 1) * CHUNK, p * PAIR:(p + 1) * PAIR] = y_bs[0:CHUNK] + y_bs[CHUNK:]
    for p in range(N_PAIRS):
        s_ref[p] = st[p]


def _rwkv_scan(z, lp, bsz, seq, reverse):
    tile = min(512, seq)
    nt = seq // tile
    z3 = z.reshape(bsz, seq, Z_COLS)
    t8 = tile // 8
    d = int(reverse)

    def tidx(i):
        return nt - 1 - i if reverse else i

    span = min(256, tile)
    cidx = np.arange(span)
    same = (cidx[:, None] // CHUNK) == (cidx[None, :] // CHUNK)
    upto = cidx[None, :] >= cidx[:, None] if reverse else cidx[None, :] <= cidx[:, None]
    tri = jnp.asarray(same & upto, BF16)

    def full(shape):
        return pl.BlockSpec(shape, lambda b, i: (0,) * len(shape))

    def per_dir(shape):
        return pl.BlockSpec((None,) + shape, lambda b, i: (d,) + (0,) * len(shape))

    out_spec = pl.BlockSpec((None, tile, RW_WIDTH), lambda b, i: (b, tidx(i), 0))
    nat = pltpu.VMEM((tile, RW_WIDTH), F32)
    return pl.pallas_call(
        functools.partial(_rwkv_kernel, tile=tile, nt=nt, reverse=reverse),
        grid=(bsz, nt),
        in_specs=[
            pl.BlockSpec((None, tile, RW_COLS_PAD), lambda b, i: (b, tidx(i), 0)),
            pl.BlockSpec((None, 8, RW_COLS_PAD), lambda b, i: (b, jnp.maximum(tidx(i) * t8 - 1, 0), 0)),
            pl.BlockSpec((None, 8, RW_COLS_PAD),
                         lambda b, i: (b, jnp.minimum((tidx(i) + 1) * t8, seq // 8 - 1), 0)),
            full((3, RW_COLS_PAD)),
            per_dir((1, RW_WIDTH)), per_dir((2 * DECAY_LORA, RW_WIDTH)),
            per_dir((1, RW_WIDTH)), per_dir((2 * AAA_LORA, RW_WIDTH)),
            full((GATE_LORA_PAD, RW_WIDTH)),
            full((1, RW_WIDTH)), full((1, RW_WIDTH)), full((1, RW_WIDTH)),
            full((RW_WIDTH, RW_WIDTH)),
            full((span, span)),
        ],
        out_specs=[out_spec, out_spec, out_spec],
        out_shape=[jax.ShapeDtypeStruct((bsz, seq, RW_WIDTH), F32),
                   jax.ShapeDtypeStruct((bsz, seq, RW_WIDTH), BF16),
                   jax.ShapeDtypeStruct((bsz, seq, RW_WIDTH), BF16)],
        scratch_shapes=[pltpu.VMEM((N_PAIRS, PAIR, PAIR), F32)] + [nat] * 8,
        compiler_params=_cparams(("arbitrary", "arbitrary")),
        name="rwkv_bwd" if reverse else "rwkv_fwd",
    )(z3, z3, z3, lp["rw_conv"], lp["rw_decay0"], lp["rw_decay2"], lp["rw_a0"], lp["rw_a2"],
      lp["rw_g2"], lp["rw_k_k"], lp["rw_k_a"], lp["rw_r_k"], lp["bones"], tri)


def _memkv_kernel(m_ref, g_ref, w_ref, o_ref):
    o_ref[...] = _dot(_rms(m_ref[...], g_ref[...]).astype(BF16), w_ref[...]).astype(o_ref.dtype)


def _mem_kv(mem, gain, w):
    bsz = mem.shape[0]
    return pl.pallas_call(
        _memkv_kernel,
        grid=(bsz,),
        in_specs=[
            pl.BlockSpec((None, MEM_TOKENS, D_MODEL), lambda b: (b, 0, 0)),
            pl.BlockSpec((1, D_MODEL), lambda b: (0, 0)),
            pl.BlockSpec((D_MODEL, 2 * MEM_WIDTH), lambda b: (0, 0)),
        ],
        out_specs=pl.BlockSpec((None, MEM_TOKENS, 2 * MEM_WIDTH), lambda b: (b, 0, 0)),
        out_shape=jax.ShapeDtypeStruct((bsz, MEM_TOKENS, 2 * MEM_WIDTH), BF16),
        compiler_params=_cparams(("parallel",)),
        name="mem_kv",
    )(mem, gain, w)


def _memattn_kernel(q_ref, kv_ref, o_ref):
    scale = MEM_HEAD_DIM ** -0.5
    outs = []
    for h in range(MEM_HEADS):
        sl = slice(h * MEM_HEAD_DIM, (h + 1) * MEM_HEAD_DIM)
        vsl = slice(MEM_WIDTH + h * MEM_HEAD_DIM, MEM_WIDTH + (h + 1) * MEM_HEAD_DIM)
        s = _dot_nt(q_ref[:, sl], kv_ref[:, sl]) * scale
        m = jnp.max(s, axis=-1, keepdims=True)
        p = jnp.exp(s - m)
        l = jnp.sum(p, axis=-1, keepdims=True)
        outs.append(_dot(p.astype(BF16), kv_ref[:, vsl]) / l)
    o_ref[...] = jnp.concatenate(outs, axis=-1).astype(o_ref.dtype)


def _mem_attention(z, kv, bsz, seq):
    tm = min(1024, seq)
    z3 = z.reshape(bsz, seq, Z_COLS)
    out = pl.pallas_call(
        _memattn_kernel,
        grid=(bsz, seq // tm),
        in_specs=[
            pl.BlockSpec((None, tm, MEM_WIDTH), lambda b, i: (b, i, Z_MEM // MEM_WIDTH)),
            pl.BlockSpec((None, MEM_TOKENS, 2 * MEM_WIDTH), lambda b, i: (b, 0, 0)),
        ],
        out_specs=pl.BlockSpec((None, tm, MEM_WIDTH), lambda b, i: (b, i, 0)),
        out_shape=jax.ShapeDtypeStruct((bsz, seq, MEM_WIDTH), BF16),
        compiler_params=_cparams(("parallel", "parallel")),
        name="mem_attn",
    )(z3, kv)
    return out.reshape(bsz * seq, MEM_WIDTH)


def _merge_kernel(x_ref, ona_ref, y0_ref, y1_ref, b0_ref, b1_ref, gt_ref, omem_ref,
                  g0_ref, g1_ref, g2_ref, wb_ref, wo_ref, lw_ref, lb_ref, bones_ref, o_ref):
    bones = bones_ref[...]
    inv_n = 1.0 / RW_HEAD_DIM
    y = y0_ref[...] + y1_ref[...]
    mu = _dot(y.astype(BF16), bones) * inv_n
    dl = y - mu
    var = _dot((dl * dl).astype(BF16), bones) * inv_n
    yn = dl * lax.rsqrt(var + GN_EPS)
    bonus = b0_ref[...].astype(F32) + b1_ref[...].astype(F32)
    o_rw = (yn * lw_ref[...] + lb_ref[...] + bonus) * gt_ref[...].astype(F32)

    merged = jax.nn.sigmoid(g0_ref[...].astype(F32)) * _dot(ona_ref[...], wb_ref[0])
    merged = merged + jax.nn.sigmoid(g1_ref[...].astype(F32)) * _dot(o_rw.astype(BF16), wb_ref[1])
    merged = merged + jax.nn.sigmoid(g2_ref[...].astype(F32)) * _dot(omem_ref[...], wb_ref[2])
    o_ref[...] = x_ref[...] + _dot(merged.astype(BF16), wo_ref[...])


def _merge(x2d, z, o_na, rw_fwd, rw_bwd, o_mem, lp):
    t = x2d.shape[0]
    tm = min(512, t)
    y0, b0, gate = (a.reshape(t, RW_WIDTH) for a in rw_fwd)
    y1, b1, _ = (a.reshape(t, RW_WIDTH) for a in rw_bwd)
    gblk = Z_GATE // D_MODEL

    def tok(width):
        return pl.BlockSpec((tm, width), lambda i: (i, 0))

    def const(shape):
        return pl.BlockSpec(shape, lambda i: (0,) * len(shape))

    return pl.pallas_call(
        _merge_kernel,
        grid=(t // tm,),
        in_specs=[
            tok(D_MODEL), tok(NA_WIDTH), tok(RW_WIDTH), tok(RW_WIDTH), tok(RW_WIDTH), tok(RW_WIDTH), tok(RW_WIDTH),
            tok(MEM_WIDTH),
            pl.BlockSpec((tm, D_MODEL), lambda i: (i, gblk)),
            pl.BlockSpec((tm, D_MODEL), lambda i: (i, gblk + 1)),
            pl.BlockSpec((tm, D_MODEL), lambda i: (i, gblk + 2)),
            const((N_BRANCH, NA_WIDTH, D_MODEL)), const((D_MODEL, D_MODEL)),
            const((1, RW_WIDTH)), const((1, RW_WIDTH)), const((RW_WIDTH, RW_WIDTH)),
        ],
        out_specs=tok(D_MODEL),
        out_shape=jax.ShapeDtypeStruct((t, D_MODEL), F32),
        compiler_params=_cparams(("parallel",)),
        name="merge",
    )(x2d, o_na, y0, y1, b0, b1, gate, o_mem, z, z, z, lp["w_branch"], lp["w_out"],
      lp["rw_lnx_w"], lp["rw_lnx_b"], lp["bones"])


def _ffn_kernel(x_ref, xp_ref, xn_ref, g_ref, wv_ref, wg_ref, cv_ref, cg_ref, bv_ref, bg_ref, wd_ref,
                fg_ref, o_ref, h_ref, acc_ref, *, tm, tiles_per_seq, final_norm):
    i = pl.program_id(0)
    j = pl.program_id(1)
    ext = tm + 16

    @pl.when(j == 0)
    def _():
        gain = g_ref[...]
        seq_pos = i % tiles_per_seq
        h_ref[0:8, :] = jnp.where(seq_pos == 0, 0.0, _rms(xp_ref[...], gain)).astype(BF16)
        h_ref[8:8 + tm, :] = _rms(x_ref[...], gain).astype(BF16)
        h_ref[8 + tm:ext, :] = jnp.where(seq_pos == tiles_per_seq - 1, 0.0, _rms(xn_ref[...], gain)).astype(BF16)
        acc_ref[...] = jnp.zeros_like(acc_ref)

    hx = h_ref[...]

    def conv(u, cw, bias):
        return (pltpu.roll(u, 1, axis=0)[8:8 + tm] * cw[0:1, :] + u[8:8 + tm] * cw[1:2, :]
                + pltpu.roll(u, ext - 1, axis=0)[8:8 + tm] * cw[2:3, :] + bias)

    u_val = conv(_dot(hx, wv_ref[...]), cv_ref[...], bv_ref[...])
    u_gate = conv(_dot(hx, wg_ref[...]), cg_ref[...], bg_ref[...])
    act = (u_gate * jax.nn.sigmoid(u_gate) * u_val).astype(BF16)
    acc_ref[...] += _dot(act, wd_ref[...])

    @pl.when(j == pl.num_programs(1) - 1)
    def _():
        out = x_ref[...] + acc_ref[...]
        if final_norm:
            out = _rms(out, fg_ref[...])
        o_ref[...] = out


def _ffn(x2d, lp, final_gain, seq, final_norm):
    t = x2d.shape[0]
    tm = min(2048, seq)
    tn = 256
    nj = D_FF // tn
    t8 = tm // 8
    return pl.pallas_call(
        functools.partial(_ffn_kernel, tm=tm, tiles_per_seq=seq // tm, final_norm=final_norm),
        grid=(t // tm, nj),
        in_specs=[
            pl.BlockSpec((tm, D_MODEL), lambda i, j: (i, 0)),
            pl.BlockSpec((8, D_MODEL), lambda i, j: (jnp.maximum(i * t8 - 1, 0), 0)),
            pl.BlockSpec((8, D_MODEL), lambda i, j: (jnp.minimum((i + 1) * t8, t // 8 - 1), 0)),
            pl.BlockSpec((1, D_MODEL), lambda i, j: (0, 0)),
            pl.BlockSpec((D_MODEL, tn), lambda i, j: (0, j)),
            pl.BlockSpec((D_MODEL, tn), lambda i, j: (0, nj + j)),
            pl.BlockSpec((3, tn), lambda i, j: (0, j)),
            pl.BlockSpec((3, tn), lambda i, j: (0, nj + j)),
            pl.BlockSpec((1, tn), lambda i, j: (0, j)),
            pl.BlockSpec((1, tn), lambda i, j: (0, nj + j)),
            pl.BlockSpec((tn, D_MODEL), lambda i, j: (j, 0)),
            pl.BlockSpec((1, D_MODEL), lambda i, j: (0, 0)),
        ],
        out_specs=pl.BlockSpec((tm, D_MODEL), lambda i, j: (i, 0)),
        out_shape=jax.ShapeDtypeStruct((t, D_MODEL), F32),
        scratch_shapes=[pltpu.VMEM((tm + 16, D_MODEL), BF16), pltpu.VMEM((tm, D_MODEL), F32)],
        compiler_params=_cparams(("parallel", "arbitrary")),
        name="ffn",
    )(x2d, x2d, x2d, lp["ffn_norm"], lp["w_up"], lp["w_up"], lp["ffn_conv"], lp["ffn_conv"],
      lp["ffn_conv_b"], lp["ffn_conv_b"], lp["w_down"], final_gain)


def _prep_layer(l, attn_norm, w_in, na_rpb, rw_conv, rw_decay0, rw_decay2, rw_a0, rw_a2, rw_g2, rw_k_k,
                rw_k_a, rw_r_k, rw_lnx_w, rw_lnx_b, mem_norm, w_mem_kv, w_branch, w_out, ffn_norm, w_up,
                ffn_conv, ffn_conv_b, w_down):
    c1 = 3 * NA_WIDTH
    c2 = c1 + RW_COLS
    w = w_in[l].astype(BF16)
    pad_cols = RW_COLS_PAD - RW_COLS
    w_new = jnp.concatenate(
        [w[:, c1:c2], jnp.zeros((D_MODEL, pad_cols), w.dtype), w[:, :c1], w[:, c2:]], axis=1)

    def lora_rows(m):
        zero = jnp.zeros_like(m[0])
        return jnp.stack([jnp.concatenate([m[0], zero], 0), jnp.concatenate([zero, m[1]], 0)]).astype(BF16)

    head = np.arange(RW_WIDTH) // RW_HEAD_DIM
    return dict(
        attn_norm=attn_norm[l][None, :],
        w_in=w_new.astype(BF16),
        na_bias=_na_bias_table(na_rpb[l]),
        rw_conv=jnp.pad(rw_conv[l], ((0, 0), (0, pad_cols))),
        rw_decay0=rw_decay0[l][:, None, :],
        rw_decay2=lora_rows(rw_decay2[l]),
        rw_a0=rw_a0[l][:, None, :],
        rw_a2=lora_rows(rw_a2[l]),
        rw_g2=jnp.pad(rw_g2[l], ((0, GATE_LORA_PAD - GATE_LORA), (0, 0))).astype(BF16),
        rw_k_k=rw_k_k[l][None, :],
        rw_k_a=rw_k_a[l][None, :],
        rw_r_k=rw_r_k[l].reshape(1, RW_WIDTH),
        rw_lnx_w=rw_lnx_w[l][None, :],
        rw_lnx_b=rw_lnx_b[l][None, :],
        bones=jnp.asarray(head[:, None] == head[None, :], BF16),
        mem_norm=mem_norm[l][None, :],
        w_mem_kv=w_mem_kv[l].astype(BF16),
        w_branch=w_branch[l].astype(BF16),
        w_out=w_out[l].astype(BF16),
        ffn_norm=ffn_norm[l][None, :],
        w_up=w_up[l].astype(BF16),
        ffn_conv=ffn_conv[l],
        ffn_conv_b=ffn_conv_b[l][None, :],
        w_down=w_down[l].astype(BF16),
    )


def _layer(x2d, mem, lp, bsz, seq, final_gain, final_norm):
    z = _in_proj(x2d, lp["attn_norm"], lp["w_in"])
    o_na = _na_attention(z, lp["na_bias"], bsz, seq)
    rw_fwd = _rwkv_scan(z, lp, bsz, seq, reverse=False)
    rw_bwd = _rwkv_scan(z, lp, bsz, seq, reverse=True)
    o_mem = _mem_attention(z, _mem_kv(mem, lp["mem_norm"], lp["w_mem_kv"]), bsz, seq)
    x2d = _merge(x2d, z, o_na, rw_fwd, rw_bwd, o_mem, lp)
    return _ffn(x2d, lp, final_gain, seq, final_norm)


def _trunk(x, mem, layers, final_gain):
    bsz, seq, _ = x.shape
    x2d = x.reshape(bsz * seq, D_MODEL)
    for l, lp in enumerate(layers):
        x2d = _layer(x2d, mem, lp, bsz, seq, final_gain, l == len(layers) - 1)
    return x2d.reshape(bsz, seq, D_MODEL)


def kernel(x_prompt, x_sample, mem_prompt, mem_sample, attn_norm, w_in, na_rpb, rw_conv, rw_decay0, rw_decay2, rw_a0, rw_a2, rw_g2, rw_k_k, rw_k_a, rw_r_k, rw_lnx_w, rw_lnx_b, mem_norm, w_mem_kv, w_branch, w_out, ffn_norm, w_up, ffn_conv, ffn_conv_b, w_down, final_norm):
    params = (attn_norm, w_in, na_rpb, rw_conv, rw_decay0, rw_decay2, rw_a0, rw_a2, rw_g2, rw_k_k, rw_k_a,
              rw_r_k, rw_lnx_w, rw_lnx_b, mem_norm, w_mem_kv, w_branch, w_out, ffn_norm, w_up, ffn_conv,
              ffn_conv_b, w_down)
    layers = [_prep_layer(l, *params) for l in range(attn_norm.shape[0])]
    final_gain = final_norm[None, :]
    y_prompt = _trunk(x_prompt, mem_prompt, layers, final_gain)
    y_sample = _trunk(x_sample, mem_sample, layers, final_gain)
    return (y_prompt, y_sample)
```

```python
import functools

import jax
import jax.numpy as jnp
import numpy as np
from jax import lax
from jax.experimental import pallas as pl
from jax.experimental.pallas import tpu as pltpu

F32 = jnp.float32
BF16 = jnp.bfloat16

D_MODEL = 1024
GRID_W = 64
NA_HEADS = 8
NA_HEAD_DIM = 64
NA_WIDTH = 512
NA_WIN_ROWS = 8
NA_WIN_COLS = 16
RW_HEADS = 8
RW_HEAD_DIM = 64
RW_WIDTH = 512
DECAY_LORA = 64
AAA_LORA = 64
GATE_LORA = 160
GATE_LORA_PAD = 256
MEM_TOKENS = 256
MEM_HEADS = 4
MEM_HEAD_DIM = 128
MEM_WIDTH = 512
N_BRANCH = 3
D_FF = 2816
RMS_EPS = 1e-6
GN_EPS = 64e-5

RW_COLS = 3 * RW_WIDTH + 2 * DECAY_LORA + 2 * AAA_LORA + GATE_LORA
RW_COLS_PAD = 3 * RW_WIDTH + 2 * DECAY_LORA + 2 * AAA_LORA + GATE_LORA_PAD
Z_RW = 0
Z_NA = RW_COLS_PAD
Z_MEM = Z_NA + 3 * NA_WIDTH
Z_GATE = Z_MEM + MEM_WIDTH
Z_COLS = Z_GATE + N_BRANCH * D_MODEL

CHUNK = 64
PAIR = 2 * RW_HEAD_DIM
N_PAIRS = RW_HEADS // 2
INV_LEVELS = 6
NEG_BIG = -1e30
VMEM_LIMIT = 56 * 1024 * 1024


def _cparams(sem):
    return pltpu.CompilerParams(dimension_semantics=sem, vmem_limit_bytes=VMEM_LIMIT)


def _rms(x, gain):
    ms = jnp.mean(x * x, axis=-1, keepdims=True)
    return x * lax.rsqrt(ms + RMS_EPS) * gain


def _dot(a, b):
    return jnp.dot(a, b, preferred_element_type=F32)


def _dot_nt(a, b):
    return lax.dot_general(a, b, (((1,), (1,)), ((), ())), preferred_element_type=F32)


def _split_hi_lo(x):
    hi = x.astype(BF16)
    lo = (x - hi.astype(F32)).astype(BF16)
    return hi, lo


def _sel_left(sel, x):
    hi, lo = _split_hi_lo(x)
    return _dot(sel, hi) + _dot(sel, lo)


def _sel_right(x, sel):
    hi, lo = _split_hi_lo(x)
    return _dot(hi, sel) + _dot(lo, sel)


def _inproj_kernel(x_ref, g_ref, w_ref, o_ref, h_ref):
    @pl.when(pl.program_id(1) == 0)
    def _():
        h_ref[...] = _rms(x_ref[...], g_ref[...]).astype(BF16)

    o_ref[...] = _dot(h_ref[...], w_ref[...]).astype(o_ref.dtype)


def _in_proj(x2d, gain, w):
    t, n = x2d.shape[0], w.shape[1]
    tm = min(2048, t)
    tn = 1792
    return pl.pallas_call(
        _inproj_kernel,
        grid=(t // tm, n // tn),
        in_specs=[
            pl.BlockSpec((tm, D_MODEL), lambda i, j: (i, 0)),
            pl.BlockSpec((1, D_MODEL), lambda i, j: (0, 0)),
            pl.BlockSpec((D_MODEL, tn), lambda i, j: (0, j)),
        ],
        out_specs=pl.BlockSpec((tm, tn), lambda i, j: (i, j)),
        out_shape=jax.ShapeDtypeStruct((t, n), BF16),
        scratch_shapes=[pltpu.VMEM((tm, D_MODEL), BF16)],
        compiler_params=_cparams(("parallel", "arbitrary")),
        name="in_proj",
    )(x2d, gain, w)


NA_ROW_BLOCK = 16
NA_BAND_ROWS = NA_ROW_BLOCK + NA_WIN_ROWS - 1


def _na_kernel(q_ref, k_ref, v_ref, t2_ref, o_ref, vl_ref, vr_ref, *, rows):
    half = NA_WIN_ROWS // 2
    band = NA_WIN_ROWS * GRID_W
    i0 = pl.program_id(1) * NA_ROW_BLOCK
    block_start = jnp.clip(i0 - half, 0, rows - NA_BAND_ROWS)
    scale = NA_HEAD_DIM ** -0.5

    lane = lax.broadcasted_iota(jnp.int32, (GRID_W, NA_WIDTH), 1)
    first = jnp.where(lane % PAIR < NA_HEAD_DIM, 1.0, 0.0).astype(BF16)
    second = jnp.where(lane % PAIR < NA_HEAD_DIM, 0.0, 1.0).astype(BF16)
    for r in range(NA_BAND_ROWS):
        v_row = v_ref[r]
        vl_ref[r] = v_row * first
        vr_ref[r] = v_row * second

    rows_per_iter = 8

    def rows_step(it, carry):
        s, v_sel = [], []
        for u in range(rows_per_iter):
            m = it * rows_per_iter + u
            i = i0 + m
            r0 = jnp.clip(i - half, 0, rows - NA_WIN_ROWS)
            off = r0 - block_start
            dr0 = (NA_WIN_ROWS - 1) - (i - r0)
            q = q_ref[m] * scale
            q_sel = (q * first, q * second)
            kb = k_ref[pl.ds(off, NA_WIN_ROWS)].reshape(band, NA_WIDTH)
            v_sel.append((vl_ref[pl.ds(off, NA_WIN_ROWS)].reshape(band, NA_WIDTH),
                          vr_ref[pl.ds(off, NA_WIN_ROWS)].reshape(band, NA_WIDTH)))
            for h in range(NA_HEADS):
                ps = slice((h // 2) * PAIR, (h // 2 + 1) * PAIR)
                bias = jnp.concatenate([t2_ref[h, dr0 + 2 * jj] for jj in range(NA_WIN_ROWS // 2)], axis=1)
                s.append(_dot_nt(q_sel[h % 2][:, ps], kb[:, ps]) + bias)
        mx = [jnp.max(x, axis=-1, keepdims=True) for x in s]
        p = [jnp.exp(x - m_) for x, m_ in zip(s, mx)]
        inv = [1.0 / jnp.sum(x, axis=-1, keepdims=True) for x in p]
        for u in range(rows_per_iter):
            outs = []
            for pp in range(NA_HEADS // 2):
                ps = slice(pp * PAIR, (pp + 1) * PAIR)
                e = u * NA_HEADS + 2 * pp
                o_l = _dot(p[e].astype(BF16), v_sel[u][0][:, ps])
                o_r = _dot(p[e + 1].astype(BF16), v_sel[u][1][:, ps])
                outs.append(o_l * inv[e] + o_r * inv[e + 1])
            o_ref[it * rows_per_iter + u] = jnp.concatenate(outs, axis=-1).astype(o_ref.dtype)
        return carry

    lax.fori_loop(0, NA_ROW_BLOCK // rows_per_iter, rows_step, 0)


def _na_bias_table(rpb):
    x = np.arange(GRID_W)[:, None]
    c = np.arange(GRID_W)[None, :]
    c0 = np.clip(x - NA_WIN_COLS // 2, 0, GRID_W - NA_WIN_COLS)
    valid = ((c >= c0) & (c < c0 + NA_WIN_COLS)).reshape(-1)
    dc = np.clip(c - x + (NA_WIN_COLS - 1), 0, 2 * NA_WIN_COLS - 2).reshape(-1)
    onehot = (dc[None, :] == np.arange(2 * NA_WIN_COLS - 1)[:, None]) & valid[None, :]
    tab = jnp.einsum('hrd,dn->hrn', rpb, jnp.asarray(onehot, F32), precision=lax.Precision.HIGHEST)
    tab = tab + jnp.asarray(np.where(valid, 0.0, NEG_BIG), F32)
    tab = tab.reshape(NA_HEADS, 2 * NA_WIN_ROWS - 1, GRID_W, GRID_W)
    return jnp.concatenate([tab[:, :-1], tab[:, 1:]], axis=-1)


def _na_attention(z, bias_tab, bsz, seq):
    rows = seq // GRID_W
    assert rows >= NA_BAND_ROWS and rows % NA_ROW_BLOCK == 0
    z4 = z.reshape(bsz, rows, GRID_W, Z_COLS)
    half = NA_WIN_ROWS // 2

    def kv_spec(col):
        return pl.BlockSpec(
            (pl.Squeezed(), pl.Element(NA_BAND_ROWS), pl.Element(GRID_W), pl.Element(NA_WIDTH)),
            lambda b, i: (b, jnp.clip(i * NA_ROW_BLOCK - half, 0, rows - NA_BAND_ROWS), 0, col))

    qblk = Z_NA // NA_WIDTH
    out = pl.pallas_call(
        functools.partial(_na_kernel, rows=rows),
        grid=(bsz, rows // NA_ROW_BLOCK),
        in_specs=[
            pl.BlockSpec((None, NA_ROW_BLOCK, GRID_W, NA_WIDTH), lambda b, i: (b, i, 0, qblk)),
            kv_spec(Z_NA + NA_WIDTH),
            kv_spec(Z_NA + 2 * NA_WIDTH),
            pl.BlockSpec(bias_tab.shape, lambda b, i: (0, 0, 0, 0)),
        ],
        out_specs=pl.BlockSpec((None, NA_ROW_BLOCK, GRID_W, NA_WIDTH), lambda b, i: (b, i, 0, 0)),
        out_shape=jax.ShapeDtypeStruct((bsz, rows, GRID_W, NA_WIDTH), BF16),
        scratch_shapes=[pltpu.VMEM((NA_BAND_ROWS, GRID_W, NA_WIDTH), BF16)] * 2,
        compiler_params=_cparams(("parallel", "arbitrary")),
        name="na_attn",
    )(z4, z4, z4, bias_tab)
    return out.reshape(bsz * seq, NA_WIDTH)


def _softplus(u):
    return jnp.maximum(u, 0.0) + jnp.log1p(jnp.exp(-jnp.abs(u)))


def _rwkv_kernel(zc_ref, zp_ref, zn_ref, cw_ref, d0_ref, d2_ref, a0_ref, a2_ref, g2_ref,
                 kk_ref, ka_ref, rk_ref, bones_ref, tri_ref,
                 y_ref, bonus_ref, gate_ref, s_ref, *bufs, tile, nt, reverse):
    i = pl.program_id(1)
    ti = nt - 1 - i if reverse else i

    @pl.when(i == 0)
    def _():
        s_ref[...] = jnp.zeros_like(s_ref)

    _rwkv_prep_tile(zc_ref, zp_ref, zn_ref, cw_ref, d0_ref, d2_ref, a0_ref, a2_ref, g2_ref,
                    kk_ref, ka_ref, rk_ref, bones_ref, tri_ref, bonus_ref, gate_ref, *bufs,
                    tile=tile, nt=nt, ti=ti, reverse=reverse)
    _rwkv_scan_tile(y_ref, s_ref, *bufs, tile=tile, reverse=reverse)


def _rwkv_prep_tile(zc_ref, zp_ref, zn_ref, cw_ref, d0_ref, d2_ref, a0_ref, a2_ref, g2_ref,
                    kk_ref, ka_ref, rk_ref, bones_ref, tri_ref, bonus_ref, gate_ref,
                    rt_ref, at_ref, bt_ref, kt_ref, bh_ref, kh_ref, v_ref, tot_ref,
                    *, tile, nt, ti, reverse):
    n_chunks = tile // CHUNK

    zc = zc_ref[...].astype(F32)
    prev_row = jnp.where(ti == 0, 0.0, zp_ref[...].astype(F32)[7:8, :])
    next_row = jnp.where(ti == nt - 1, 0.0, zn_ref[...].astype(F32)[0:1, :])
    zf = (pltpu.roll(zc, 1, axis=0) * cw_ref[0:1, :] + zc * cw_ref[1:2, :]
          + pltpu.roll(zc, tile - 1, axis=0) * cw_ref[2:3, :])
    rows8 = lax.broadcasted_iota(jnp.int32, (8, 1), 0)
    fix_first = jnp.where(rows8 == 0, (prev_row - zc[tile - 1:tile, :]) * cw_ref[0:1, :], 0.0)
    fix_last = jnp.where(rows8 == 7, (next_row - zc[0:1, :]) * cw_ref[2:3, :], 0.0)
    zf = jnp.concatenate([zf[0:8] + fix_first, zf[8:tile - 8], zf[tile - 8:tile] + fix_last], axis=0)

    o1, o2, o3 = RW_WIDTH, 2 * RW_WIDTH, 3 * RW_WIDTH
    o4 = o3 + 2 * DECAY_LORA
    o5 = o4 + 2 * AAA_LORA
    r = zf[:, 0:o1]
    k = zf[:, o1:o2]
    v = zf[:, o2:o3]
    xw = zf[:, o3:o4]
    xa = zf[:, o4:o5]
    xg = zf[:, o5:RW_COLS_PAD]

    bones = bones_ref[...]
    kk = k * kk_ref[...]
    kk = kk * lax.rsqrt(jnp.maximum(_dot((kk * kk).astype(BF16), bones), 1e-24))
    gate_ref[...] = _dot(jax.nn.sigmoid(xg).astype(BF16), g2_ref[...]).astype(gate_ref.dtype)

    w_log = -_softplus(-(d0_ref[...] + _dot(jnp.tanh(xw).astype(BF16), d2_ref[...]))) - 0.5
    logw = -jnp.exp(w_log)
    a = jax.nn.sigmoid(a0_ref[...] + _dot(xa.astype(BF16), a2_ref[...]))
    k_d = k * (1.0 + (a - 1.0) * ka_ref[...])
    b = kk * a
    bonus_ref[...] = (_dot((r * k_d * rk_ref[...]).astype(BF16), bones) * v).astype(bonus_ref.dtype)

    span = tri_ref.shape[0]
    g = jnp.concatenate([_sel_left(tri_ref[...], logw[r:r + span]) for r in range(0, tile, span)], axis=0)
    last = 0 if reverse else CHUNK - 1
    tot = jnp.concatenate(
        [jnp.broadcast_to(g[c * CHUNK + last:c * CHUNK + last + 1, :], (CHUNK, RW_WIDTH))
         for c in range(n_chunks)], axis=0)
    eng = jnp.exp(-g)
    ehat = jnp.exp(tot - g)
    rt_ref[...] = r * jnp.exp(g)
    at_ref[...] = -kk * jnp.exp(g - logw)
    bt_ref[...] = b * eng
    kt_ref[...] = k_d * eng
    bh_ref[...] = b * ehat
    kh_ref[...] = k_d * ehat
    v_ref[...] = v
    tot_ref[...] = tot


def _rwkv_scan_tile(y_ref, s_ref, rt_ref, at_ref, bt_ref, kt_ref, bh_ref, kh_ref, v_ref, tot_ref,
                    *, tile, reverse):
    n_chunks = tile // CHUNK

    ri = lax.broadcasted_iota(jnp.int32, (PAIR, PAIR), 0)
    ci = lax.broadcasted_iota(jnp.int32, (PAIR, PAIR), 1)
    same_head = jnp.where(jnp.right_shift(ri, 6) == jnp.right_shift(ci, 6), 1.0, 0.0).astype(F32)
    order = ci - ri if reverse else ri - ci
    m_strict = jnp.where(order > 0, same_head, 0.0)
    m_incl = jnp.where(order >= 0, same_head, 0.0)
    eye = jnp.where(ci == ri, 1.0, 0.0).astype(F32)
    lvl = []
    for q in range(INV_LEVELS):
        in_block = jnp.where(jnp.right_shift(ri, q + 1) == jnp.right_shift(ci, q + 1), 1.0, 0.0).astype(F32)
        lvl.append(jnp.where(jnp.right_shift(ri, q) != jnp.right_shift(ci, q), in_block, 0.0))
    lane = lax.broadcasted_iota(jnp.int32, (CHUNK, PAIR), 1)
    left = lane < RW_HEAD_DIM

    def stack(x):
        return jnp.concatenate([jnp.where(left, x, 0.0), jnp.where(left, 0.0, x)], axis=0)

    groups = [(c, p) for c in range(n_chunks) for p in range(N_PAIRS)]
    eye_b = eye.astype(BF16)

    def tile_of(ref, c, p):
        return ref[c * CHUNK:(c + 1) * CHUNK, p * PAIR:(p + 1) * PAIR]

    at_b = [stack(tile_of(at_ref, c, p)).astype(BF16) for c, p in groups]
    vs_b = [stack(tile_of(v_ref, c, p)).astype(BF16) for c, p in groups]
    o1_ = []
    for gi, (c, p) in enumerate(groups):
        bt = tile_of(bt_ref, c, p).astype(BF16)
        kt = tile_of(kt_ref, c, p).astype(BF16)
        lhs1 = jnp.concatenate([at_b[gi], stack(tile_of(rt_ref, c, p)).astype(BF16)], axis=0)
        o1_.append(_dot_nt(lhs1, jnp.concatenate([bt, bt, kt, kt], axis=0)))
    ms_b = m_strict.astype(BF16)
    mi_b = m_incl.astype(BF16)
    lvl_b = [m.astype(BF16) for m in lvl]
    n_b = [o[0:PAIR, 0:PAIR].astype(BF16) * ms_b for o in o1_]
    a_ak = [o[0:PAIR, PAIR:].astype(BF16) * ms_b for o in o1_]
    a_r = [jnp.concatenate([o[PAIR:, 0:PAIR].astype(BF16) * mi_b, o[PAIR:, PAIR:].astype(BF16) * mi_b], axis=1)
           for o in o1_]

    t_inv = [eye.astype(BF16) + n * lvl_b[0] for n in n_b]
    for q in range(1, INV_LEVELS):
        blk = 1 << q
        if blk < 16:
            xq = [_dot(n * lvl_b[q], t) for n, t in zip(n_b, t_inv)]
            t_inv = [t + _dot(t, x.astype(BF16)).astype(BF16) for t, x in zip(t_inv, xq)]
            continue
        n_blocks = PAIR // blk
        late = [k for k in range(n_blocks) if (k % 2 == 0) == reverse]

        def take(x):
            return jnp.concatenate([x[k * blk:(k + 1) * blk] for k in late], axis=0)

        def spread(x):
            zero = jnp.zeros((blk, PAIR), x.dtype)
            return jnp.concatenate(
                [x[late.index(k) * blk:(late.index(k) + 1) * blk] if k in late else zero
                 for k in range(n_blocks)], axis=0)

        xq = [_dot(take(n) * take(lvl_b[q]), t) for n, t in zip(n_b, t_inv)]
        upd = [_dot(take(t), spread(x.astype(BF16))) for t, x in zip(t_inv, xq)]
        t_inv = [t + spread(u.astype(BF16)) for t, u in zip(t_inv, upd)]

    av = [_dot(a, v_) for a, v_ in zip(a_ak, vs_b)]
    pu = [_dot(t, jnp.concatenate([a, x.astype(BF16)], axis=1))
          for t, a, x in zip(t_inv, at_b, av)]
    zero_b = jnp.zeros((PAIR, PAIR), BF16)
    qb, y_loc, n_st = [], [], []
    for gi, (c, p) in enumerate(groups):
        bh_t = stack(tile_of(bh_ref, c, p)).T
        kh_t = stack(tile_of(kh_ref, c, p)).T
        wmat = jnp.concatenate(
            [pu[gi].astype(BF16), jnp.concatenate([zero_b, vs_b[gi]], axis=1)], axis=0)
        lhs2 = jnp.concatenate([a_r[gi], jnp.concatenate([bh_t, kh_t], axis=1).astype(BF16)], axis=0)
        o2_ = _dot(lhs2, wmat)
        q_mat = stack(tile_of(rt_ref, c, p)) + o2_[0:PAIR, 0:PAIR]
        qb.append(jnp.concatenate([q_mat, o2_[PAIR:, 0:PAIR]], axis=0).astype(BF16))
        y_loc.append(o2_[0:PAIR, PAIR:])
        n_st.append(o2_[PAIR:, PAIR:])

    rep = PAIR // n_chunks
    w_tot = []
    for p in range(N_PAIRS):
        t_rows = jnp.concatenate(
            [tot_ref[c * CHUNK:c * CHUNK + rep, p * PAIR:(p + 1) * PAIR] for c in range(n_chunks)], axis=0)
        hi = t_rows.astype(BF16)
        r1 = t_rows - hi.astype(F32)
        mid = r1.astype(BF16)
        lo = (r1 - mid.astype(F32)).astype(BF16)
        cols = _dot_nt(eye_b, hi) + _dot_nt(eye_b, mid) + _dot_nt(eye_b, lo)
        w_tot.append([jnp.exp(cols[:, c * rep:c * rep + 1]) for c in range(n_chunks)])

    st = [s_ref[p] for p in range(N_PAIRS)]
    for c in (reversed(range(n_chunks)) if reverse else range(n_chunks)):
        o3_ = [_dot(qb[c * N_PAIRS + p], st[p].astype(BF16)) for p in range(N_PAIRS)]
        for p in range(N_PAIRS):
            gi = c * N_PAIRS + p
            y_bs = o3_[p][0:PAIR] + y_loc[gi]
            st[p] = st[p] * w_tot[p][c] + o3_[p][PAIR:] + n_st[gi]
            y_ref[c * CHUNK:(c + 1) * CHUNK, p * PAIR:(p + 1) * PAIR] = y_bs[0:CHUNK] + y_bs[CHUNK:]
    for p in range(N_PAIRS):
        s_ref[p] = st[p]


def _rwkv_scan(z, lp, bsz, seq, reverse):
    tile = min(512, seq)
    nt = seq // tile
    z3 = z.reshape(bsz, seq, Z_COLS)
    t8 = tile // 8
    d = int(reverse)

    def tidx(i):
        return nt - 1 - i if reverse else i

    span = min(256, tile)
    cidx = np.arange(span)
    same = (cidx[:, None] // CHUNK) == (cidx[None, :] // CHUNK)
    upto = cidx[None, :] >= cidx[:, None] if reverse else cidx[None, :] <= cidx[:, None]
    tri = jnp.asarray(same & upto, BF16)

    def full(shape):
        return pl.BlockSpec(shape, lambda b, i: (0,) * len(shape))

    def per_dir(shape):
        return pl.BlockSpec((None,) + shape, lambda b, i: (d,) + (0,) * len(shape))

    out_spec = pl.BlockSpec((None, tile, RW_WIDTH), lambda b, i: (b, tidx(i), 0))
    nat = pltpu.VMEM((tile, RW_WIDTH), F32)
    return pl.pallas_call(
        functools.partial(_rwkv_kernel, tile=tile, nt=nt, reverse=reverse),
        grid=(bsz, nt),
        in_specs=[
            pl.BlockSpec((None, tile, RW_COLS_PAD), lambda b, i: (b, tidx(i), 0)),
            pl.BlockSpec((None, 8, RW_COLS_PAD), lambda b, i: (b, jnp.maximum(tidx(i) * t8 - 1, 0), 0)),
            pl.BlockSpec((None, 8, RW_COLS_PAD),
                         lambda b, i: (b, jnp.minimum((tidx(i) + 1) * t8, seq // 8 - 1), 0)),
            full((3, RW_COLS_PAD)),
            per_dir((1, RW_WIDTH)), per_dir((2 * DECAY_LORA, RW_WIDTH)),
            per_dir((1, RW_WIDTH)), per_dir((2 * AAA_LORA, RW_WIDTH)),
            full((GATE_LORA_PAD, RW_WIDTH)),
            full((1, RW_WIDTH)), full((1, RW_WIDTH)), full((1, RW_WIDTH)),
            full((RW_WIDTH, RW_WIDTH)),
            full((span, span)),
        ],
        out_specs=[out_spec, out_spec, out_spec],
        out_shape=[jax.ShapeDtypeStruct((bsz, seq, RW_WIDTH), F32),
                   jax.ShapeDtypeStruct((bsz, seq, RW_WIDTH), BF16),
                   jax.ShapeDtypeStruct((bsz, seq, RW_WIDTH), BF16)],
        scratch_shapes=[pltpu.VMEM((N_PAIRS, PAIR, PAIR), F32)] + [nat] * 8,
        compiler_params=_cparams(("arbitrary", "arbitrary")),
        name="rwkv_bwd" if reverse else "rwkv_fwd",
    )(z3, z3, z3, lp["rw_conv"], lp["rw_decay0"], lp["rw_decay2"], lp["rw_a0"], lp["rw_a2"],
      lp["rw_g2"], lp["rw_k_k"], lp["rw_k_a"], lp["rw_r_k"], lp["bones"], tri)


def _memkv_kernel(m_ref, g_ref, w_ref, o_ref):
    o_ref[...] = _dot(_rms(m_ref[...], g_ref[...]).astype(BF16), w_ref[...]).astype(o_ref.dtype)


def _mem_kv(mem, gain, w):
    bsz = mem.shape[0]
    return pl.pallas_call(
        _memkv_kernel,
        grid=(bsz,),
        in_specs=[
            pl.BlockSpec((None, MEM_TOKENS, D_MODEL), lambda b: (b, 0, 0)),
            pl.BlockSpec((1, D_MODEL), lambda b: (0, 0)),
            pl.BlockSpec((D_MODEL, 2 * MEM_WIDTH), lambda b: (0, 0)),
        ],
        out_specs=pl.BlockSpec((None, MEM_TOKENS, 2 * MEM_WIDTH), lambda b: (b, 0, 0)),
        out_shape=jax.ShapeDtypeStruct((bsz, MEM_TOKENS, 2 * MEM_WIDTH), BF16),
        compiler_params=_cparams(("parallel",)),
        name="mem_kv",
    )(mem, gain, w)


def _memattn_kernel(q_ref, kv_ref, o_ref):
    scale = MEM_HEAD_DIM ** -0.5
    outs = []
    for h in range(MEM_HEADS):
        sl = slice(h * MEM_HEAD_DIM, (h + 1) * MEM_HEAD_DIM)
        vsl = slice(MEM_WIDTH + h * MEM_HEAD_DIM, MEM_WIDTH + (h + 1) * MEM_HEAD_DIM)
        s = _dot_nt(q_ref[:, sl], kv_ref[:, sl]) * scale
        m = jnp.max(s, axis=-1, keepdims=True)
        p = jnp.exp(s - m)
        l = jnp.sum(p, axis=-1, keepdims=True)
        outs.append(_dot(p.astype(BF16), kv_ref[:, vsl]) / l)
    o_ref[...] = jnp.concatenate(outs, axis=-1).astype(o_ref.dtype)


def _mem_attention(z, kv, bsz, seq):
    tm = min(1024, seq)
    z3 = z.reshape(bsz, seq, Z_COLS)
    out = pl.pallas_call(
        _memattn_kernel,
        grid=(bsz, seq // tm),
        in_specs=[
            pl.BlockSpec((None, tm, MEM_WIDTH), lambda b, i: (b, i, Z_MEM // MEM_WIDTH)),
            pl.BlockSpec((None, MEM_TOKENS, 2 * MEM_WIDTH), lambda b, i: (b, 0, 0)),
        ],
        out_specs=pl.BlockSpec((None, tm, MEM_WIDTH), lambda b, i: (b, i, 0)),
        out_shape=jax.ShapeDtypeStruct((bsz, seq, MEM_WIDTH), BF16),
        compiler_params=_cparams(("parallel", "parallel")),
        name="mem_attn",
    )(z3, kv)
    return out.reshape(bsz * seq, MEM_WIDTH)


def _merge_kernel(x_ref, ona_ref, y0_ref, y1_ref, b0_ref, b1_ref, gt_ref, omem_ref,
                  g0_ref, g1_ref, g2_ref, wb_ref, wo_ref, lw_ref, lb_ref, bones_ref, o_ref):
    bones = bones_ref[...]
    inv_n = 1.0 / RW_HEAD_DIM
    y = y0_ref[...] + y1_ref[...]
    mu = _dot(y.astype(BF16), bones) * inv_n
    dl = y - mu
    var = _dot((dl * dl).astype(BF16), bones) * inv_n
    yn = dl * lax.rsqrt(var + GN_EPS)
    bonus = b0_ref[...].astype(F32) + b1_ref[...].astype(F32)
    o_rw = (yn * lw_ref[...] + lb_ref[...] + bonus) * gt_ref[...].astype(F32)

    merged = jax.nn.sigmoid(g0_ref[...].astype(F32)) * _dot(ona_ref[...], wb_ref[0])
    merged = merged + jax.nn.sigmoid(g1_ref[...].astype(F32)) * _dot(o_rw.astype(BF16), wb_ref[1])
    merged = merged + jax.nn.sigmoid(g2_ref[...].astype(F32)) * _dot(omem_ref[...], wb_ref[2])
    o_ref[...] = x_ref[...] + _dot(merged.astype(BF16), wo_ref[...])


def _merge(x2d, z, o_na, rw_fwd, rw_bwd, o_mem, lp):
    t = x2d.shape[0]
    tm = min(512, t)
    y0, b0, gate = (a.reshape(t, RW_WIDTH) for a in rw_fwd)
    y1, b1, _ = (a.reshape(t, RW_WIDTH) for a in rw_bwd)
    gblk = Z_GATE // D_MODEL

    def tok(width):
        return pl.BlockSpec((tm, width), lambda i: (i, 0))

    def const(shape):
        return pl.BlockSpec(shape, lambda i: (0,) * len(shape))

    return pl.pallas_call(
        _merge_kernel,
        grid=(t // tm,),
        in_specs=[
            tok(D_MODEL), tok(NA_WIDTH), tok(RW_WIDTH), tok(RW_WIDTH), tok(RW_WIDTH), tok(RW_WIDTH), tok(RW_WIDTH),
            tok(MEM_WIDTH),
            pl.BlockSpec((tm, D_MODEL), lambda i: (i, gblk)),
            pl.BlockSpec((tm, D_MODEL), lambda i: (i, gblk + 1)),
            pl.BlockSpec((tm, D_MODEL), lambda i: (i, gblk + 2)),
            const((N_BRANCH, NA_WIDTH, D_MODEL)), const((D_MODEL, D_MODEL)),
            const((1, RW_WIDTH)), const((1, RW_WIDTH)), const((RW_WIDTH, RW_WIDTH)),
        ],
        out_specs=tok(D_MODEL),
        out_shape=jax.ShapeDtypeStruct((t, D_MODEL), F32),
        compiler_params=_cparams(("parallel",)),
        name="merge",
    )(x2d, o_na, y0, y1, b0, b1, gate, o_mem, z, z, z, lp["w_branch"], lp["w_out"],
      lp["rw_lnx_w"], lp["rw_lnx_b"], lp["bones"])


def _ffn_kernel(x_ref, xp_ref, xn_ref, g_ref, wv_ref, wg_ref, cv_ref, cg_ref, bv_ref, bg_ref, wd_ref,
                fg_ref, o_ref, h_ref, acc_ref, *, tm, tiles_per_seq, final_norm):
    i = pl.program_id(0)
    j = pl.program_id(1)
    ext = tm + 16

    @pl.when(j == 0)
    def _():
        gain = g_ref[...]
        seq_pos = i % tiles_per_seq
        h_ref[0:8, :] = jnp.where(seq_pos == 0, 0.0, _rms(xp_ref[...], gain)).astype(BF16)
        h_ref[8:8 + tm, :] = _rms(x_ref[...], gain).astype(BF16)
        h_ref[8 + tm:ext, :] = jnp.where(seq_pos == tiles_per_seq - 1, 0.0, _rms(xn_ref[...], gain)).astype(BF16)
        acc_ref[...] = jnp.zeros_like(acc_ref)

    n_parts = 4 if tm % 64 == 0 else 1
    pm = tm // n_parts

    def conv(u, cw, bias):
        return (pltpu.roll(u, 1, axis=0)[8:8 + pm] * cw[0:1, :] + u[8:8 + pm] * cw[1:2, :]
                + pltpu.roll(u, pm + 15, axis=0)[8:8 + pm] * cw[2:3, :] + bias)

    ups = []
    for part in range(n_parts):
        hx = h_ref[part * pm:part * pm + pm + 16, :]
        ups.append((_dot(hx, wv_ref[...]), _dot(hx, wg_ref[...])))
    for part, (uv, ug) in enumerate(ups):
        u_val = conv(uv, cv_ref[...], bv_ref[...])
        u_gate = conv(ug, cg_ref[...], bg_ref[...])
        act = (u_gate * jax.nn.sigmoid(u_gate) * u_val).astype(BF16)
        acc_ref[part * pm:(part + 1) * pm, :] += _dot(act, wd_ref[...])

    @pl.when(j == pl.num_programs(1) - 1)
    def _():
        out = x_ref[...] + acc_ref[...]
        if final_norm:
            out = _rms(out, fg_ref[...])
        o_ref[...] = out


def _ffn(x2d, lp, final_gain, seq, final_norm):
    t = x2d.shape[0]
    tm = min(2048, seq)
    tn = 256
    nj = D_FF // tn
    t8 = tm // 8
    return pl.pallas_call(
        functools.partial(_ffn_kernel, tm=tm, tiles_per_seq=seq // tm, final_norm=final_norm),
        grid=(t // tm, nj),
        in_specs=[
            pl.BlockSpec((tm, D_MODEL), lambda i, j: (i, 0)),
            pl.BlockSpec((8, D_MODEL), lambda i, j: (jnp.maximum(i * t8 - 1, 0), 0)),
            pl.BlockSpec((8, D_MODEL), lambda i, j: (jnp.minimum((i + 1) * t8, t // 8 - 1), 0)),
            pl.BlockSpec((1, D_MODEL), lambda i, j: (0, 0)),
            pl.BlockSpec((D_MODEL, tn), lambda i, j: (0, j)),
            pl.BlockSpec((D_MODEL, tn), lambda i, j: (0, nj + j)),
            pl.BlockSpec((3, tn), lambda i, j: (0, j)),
            pl.BlockSpec((3, tn), lambda i, j: (0, nj + j)),
            pl.BlockSpec((1, tn), lambda i, j: (0, j)),
            pl.BlockSpec((1, tn), lambda i, j: (0, nj + j)),
            pl.BlockSpec((tn, D_MODEL), lambda i, j: (j, 0)),
            pl.BlockSpec((1, D_MODEL), lambda i, j: (0, 0)),
        ],
        out_specs=pl.BlockSpec((tm, D_MODEL), lambda i, j: (i, 0)),
        out_shape=jax.ShapeDtypeStruct((t, D_MODEL), F32),
        scratch_shapes=[pltpu.VMEM((tm + 16, D_MODEL), BF16), pltpu.VMEM((tm, D_MODEL), F32)],
        compiler_params=_cparams(("parallel", "arbitrary")),
        name="ffn",
    )(x2d, x2d, x2d, lp["ffn_norm"], lp["w_up"], lp["w_up"], lp["ffn_conv"], lp["ffn_conv"],
      lp["ffn_conv_b"], lp["ffn_conv_b"], lp["w_down"], final_gain)


def _prep_layer(l, attn_norm, w_in, na_rpb, rw_conv, rw_decay0, rw_decay2, rw_a0, rw_a2, rw_g2, rw_k_k,
                rw_k_a, rw_r_k, rw_lnx_w, rw_lnx_b, mem_norm, w_mem_kv, w_branch, w_out, ffn_norm, w_up,
                ffn_conv, ffn_conv_b, w_down):
    c1 = 3 * NA_WIDTH
    c2 = c1 + RW_COLS
    w = w_in[l].astype(BF16)
    pad_cols = RW_COLS_PAD - RW_COLS
    w_new = jnp.concatenate(
        [w[:, c1:c2], jnp.zeros((D_MODEL, pad_cols), w.dtype), w[:, :c1], w[:, c2:]], axis=1)

    def lora_rows(m):
        zero = jnp.zeros_like(m[0])
        return jnp.stack([jnp.concatenate([m[0], zero], 0), jnp.concatenate([zero, m[1]], 0)]).astype(BF16)

    head = np.arange(RW_WIDTH) // RW_HEAD_DIM
    return dict(
        attn_norm=attn_norm[l][None, :],
        w_in=w_new.astype(BF16),
        na_bias=_na_bias_table(na_rpb[l]),
        rw_conv=jnp.pad(rw_conv[l], ((0, 0), (0, pad_cols))),
        rw_decay0=rw_decay0[l][:, None, :],
        rw_decay2=lora_rows(rw_decay2[l]),
        rw_a0=rw_a0[l][:, None, :],
        rw_a2=lora_rows(rw_a2[l]),
        rw_g2=jnp.pad(rw_g2[l], ((0, GATE_LORA_PAD - GATE_LORA), (0, 0))).astype(BF16),
        rw_k_k=rw_k_k[l][None, :],
        rw_k_a=rw_k_a[l][None, :],
        rw_r_k=rw_r_k[l].reshape(1, RW_WIDTH),
        rw_lnx_w=rw_lnx_w[l][None, :],
        rw_lnx_b=rw_lnx_b[l][None, :],
        bones=jnp.asarray(head[:, None] == head[None, :], BF16),
        mem_norm=mem_norm[l][None, :],
        w_mem_kv=w_mem_kv[l].astype(BF16),
        w_branch=w_branch[l].astype(BF16),
        w_out=w_out[l].astype(BF16),
        ffn_norm=ffn_norm[l][None, :],
        w_up=w_up[l].astype(BF16),
        ffn_conv=ffn_conv[l],
        ffn_conv_b=ffn_conv_b[l][None, :],
        w_down=w_down[l].astype(BF16),
    )


def _layer(x2d, mem, lp, bsz, seq, final_gain, final_norm):
    z = _in_proj(x2d, lp["attn_norm"], lp["w_in"])
    o_na = _na_attention(z, lp["na_bias"], bsz, seq)
    rw_fwd = _rwkv_scan(z, lp, bsz, seq, reverse=False)
    rw_bwd = _rwkv_scan(z, lp, bsz, seq, reverse=True)
    o_mem = _mem_attention(z, _mem_kv(mem, lp["mem_norm"], lp["w_mem_kv"]), bsz, seq)
    x2d = _merge(x2d, z, o_na, rw_fwd, rw_bwd, o_mem, lp)
    return _ffn(x2d, lp, final_gain, seq, final_norm)


def _trunk(x, mem, layers, final_gain):
    bsz, seq, _ = x.shape
    x2d = x.reshape(bsz * seq, D_MODEL)
    for l, lp in enumerate(layers):
        x2d = _layer(x2d, mem, lp, bsz, seq, final_gain, l == len(layers) - 1)
    return x2d.reshape(bsz, seq, D_MODEL)


def kernel(x_prompt, x_sample, mem_prompt, mem_sample, attn_norm, w_in, na_rpb, rw_conv, rw_decay0, rw_decay2, rw_a0, rw_a2, rw_g2, rw_k_k, rw_k_a, rw_r_k, rw_lnx_w, rw_lnx_b, mem_norm, w_mem_kv, w_branch, w_out, ffn_norm, w_up, ffn_conv, ffn_conv_b, w_down, final_norm):
    params = (attn_norm, w_in, na_rpb, rw_conv, rw_decay0, rw_decay2, rw_a0, rw_a2, rw_g2, rw_k_k, rw_k_a,
              rw_r_k, rw_lnx_w, rw_lnx_b, mem_norm, w_mem_kv, w_branch, w_out, ffn_norm, w_up, ffn_conv,
              ffn_conv_b, w_down)
    layers = [_prep_layer(l, *params) for l in range(attn_norm.shape[0])]
    final_gain = final_norm[None, :]
    y_prompt = _trunk(x_prompt, mem_prompt, layers, final_gain)
    y_sample = _trunk(x_sample, mem_sample, layers, final_gain)
    return (y_prompt, y_sample)
```

```python
import functools

import jax
import jax.numpy as jnp
import numpy as np
from jax import lax
from jax.experimental import pallas as pl
from jax.experimental.pallas import tpu as pltpu

F32 = jnp.float32
BF16 = jnp.bfloat16

D_MODEL = 1024
GRID_W = 64
NA_HEADS = 8
NA_HEAD_DIM = 64
NA_WIDTH = 512
NA_WIN_ROWS = 8
NA_WIN_COLS = 16
RW_HEADS = 8
RW_HEAD_DIM = 64
RW_WIDTH = 512
DECAY_LORA = 64
AAA_LORA = 64
GATE_LORA = 160
GATE_LORA_PAD = 256
MEM_TOKENS = 256
MEM_HEADS = 4
MEM_HEAD_DIM = 128
MEM_WIDTH = 512
N_BRANCH = 3
D_FF = 2816
RMS_EPS = 1e-6
GN_EPS = 64e-5

RW_COLS = 3 * RW_WIDTH + 2 * DECAY_LORA + 2 * AAA_LORA + GATE_LORA
RW_COLS_PAD = 3 * RW_WIDTH + 2 * DECAY_LORA + 2 * AAA_LORA + GATE_LORA_PAD
Z_RW = 0
Z_NA = RW_COLS_PAD
Z_MEM = Z_NA + 3 * NA_WIDTH
Z_GATE = Z_MEM + MEM_WIDTH
Z_COLS = Z_GATE + N_BRANCH * D_MODEL

CHUNK = 64
PAIR = 2 * RW_HEAD_DIM
N_PAIRS = RW_HEADS // 2
INV_LEVELS = 6
NEG_BIG = -1e30
VMEM_LIMIT = 56 * 1024 * 1024


def _cparams(sem):
    return pltpu.CompilerParams(dimension_semantics=sem, vmem_limit_bytes=VMEM_LIMIT)


def _rms(x, gain):
    ms = jnp.mean(x * x, axis=-1, keepdims=True)
    return x * lax.rsqrt(ms + RMS_EPS) * gain


def _dot(a, b):
    return jnp.dot(a, b, preferred_element_type=F32)


def _dot_nt(a, b):
    return lax.dot_general(a, b, (((1,), (1,)), ((), ())), preferred_element_type=F32)


def _split_hi_lo(x):
    hi = x.astype(BF16)
    lo = (x - hi.astype(F32)).astype(BF16)
    return hi, lo


def _sel_left(sel, x):
    hi, lo = _split_hi_lo(x)
    return _dot(sel, hi) + _dot(sel, lo)


def _sel_right(x, sel):
    hi, lo = _split_hi_lo(x)
    return _dot(hi, sel) + _dot(lo, sel)


def _inproj_kernel(x_ref, g_ref, w_ref, o_ref, h_ref):
    @pl.when(pl.program_id(1) == 0)
    def _():
        h_ref[...] = _rms(x_ref[...], g_ref[...]).astype(BF16)

    o_ref[...] = _dot(h_ref[...], w_ref[...]).astype(o_ref.dtype)


def _in_proj(x2d, gain, w):
    t, n = x2d.shape[0], w.shape[1]
    tm = min(2048, t)
    tn = 1792
    return pl.pallas_call(
        _inproj_kernel,
        grid=(t // tm, n // tn),
        in_specs=[
            pl.BlockSpec((tm, D_MODEL), lambda i, j: (i, 0)),
            pl.BlockSpec((1, D_MODEL), lambda i, j: (0, 0)),
            pl.BlockSpec((D_MODEL, tn), lambda i, j: (0, j)),
        ],
        out_specs=pl.BlockSpec((tm, tn), lambda i, j: (i, j)),
        out_shape=jax.ShapeDtypeStruct((t, n), BF16),
        scratch_shapes=[pltpu.VMEM((tm, D_MODEL), BF16)],
        compiler_params=_cparams(("parallel", "arbitrary")),
        name="in_proj",
    )(x2d, gain, w)


NA_ROW_BLOCK = 16
NA_BAND_ROWS = NA_ROW_BLOCK + NA_WIN_ROWS - 1


def _na_kernel(q_ref, k_ref, v_ref, t2_ref, o_ref, vl_ref, vr_ref, *, rows):
    half = NA_WIN_ROWS // 2
    band = NA_WIN_ROWS * GRID_W
    i0 = pl.program_id(1) * NA_ROW_BLOCK
    block_start = jnp.clip(i0 - half, 0, rows - NA_BAND_ROWS)
    scale = NA_HEAD_DIM ** -0.5

    lane = lax.broadcasted_iota(jnp.int32, (GRID_W, NA_WIDTH), 1)
    first = jnp.where(lane % PAIR < NA_HEAD_DIM, 1.0, 0.0).astype(BF16)
    second = jnp.where(lane % PAIR < NA_HEAD_DIM, 0.0, 1.0).astype(BF16)
    for r in range(NA_BAND_ROWS):
        v_row = v_ref[r]
        vl_ref[r] = v_row * first
        vr_ref[r] = v_row * second

    rows_per_iter = 8

    def rows_step(it, carry):
        s, v_sel = [], []
        for u in range(rows_per_iter):
            m = it * rows_per_iter + u
            i = i0 + m
            r0 = jnp.clip(i - half, 0, rows - NA_WIN_ROWS)
            off = r0 - block_start
            dr0 = (NA_WIN_ROWS - 1) - (i - r0)
            q = q_ref[m] * scale
            q_sel = (q * first, q * second)
            kb = k_ref[pl.ds(off, NA_WIN_ROWS)].reshape(band, NA_WIDTH)
            v_sel.append((vl_ref[pl.ds(off, NA_WIN_ROWS)].reshape(band, NA_WIDTH),
                          vr_ref[pl.ds(off, NA_WIN_ROWS)].reshape(band, NA_WIDTH)))
            for h in range(NA_HEADS):
                ps = slice((h // 2) * PAIR, (h // 2 + 1) * PAIR)
                bias = jnp.concatenate([t2_ref[h, dr0 + 2 * jj] for jj in range(NA_WIN_ROWS // 2)], axis=1)
                s.append(_dot_nt(q_sel[h % 2][:, ps], kb[:, ps]) + bias)
        mx = [jnp.max(x, axis=-1, keepdims=True) for x in s]
        p = [jnp.exp(x - m_) for x, m_ in zip(s, mx)]
        inv = [1.0 / jnp.sum(x, axis=-1, keepdims=True) for x in p]
        for u in range(rows_per_iter):
            outs = []
            for pp in range(NA_HEADS // 2):
                ps = slice(pp * PAIR, (pp + 1) * PAIR)
                e = u * NA_HEADS + 2 * pp
                o_l = _dot(p[e].astype(BF16), v_sel[u][0][:, ps])
                o_r = _dot(p[e + 1].astype(BF16), v_sel[u][1][:, ps])
                outs.append(o_l * inv[e] + o_r * inv[e + 1])
            o_ref[it * rows_per_iter + u] = jnp.concatenate(outs, axis=-1).astype(o_ref.dtype)
        return carry

    lax.fori_loop(0, NA_ROW_BLOCK // rows_per_iter, rows_step, 0)


def _na_bias_table(rpb):
    x = np.arange(GRID_W)[:, None]
    c = np.arange(GRID_W)[None, :]
    c0 = np.clip(x - NA_WIN_COLS // 2, 0, GRID_W - NA_WIN_COLS)
    valid = ((c >= c0) & (c < c0 + NA_WIN_COLS)).reshape(-1)
    dc = np.clip(c - x + (NA_WIN_COLS - 1), 0, 2 * NA_WIN_COLS - 2).reshape(-1)
    onehot = (dc[None, :] == np.arange(2 * NA_WIN_COLS - 1)[:, None]) & valid[None, :]
    tab = jnp.einsum('hrd,dn->hrn', rpb, jnp.asarray(onehot, F32), precision=lax.Precision.HIGHEST)
    tab = tab + jnp.asarray(np.where(valid, 0.0, NEG_BIG), F32)
    tab = tab.reshape(NA_HEADS, 2 * NA_WIN_ROWS - 1, GRID_W, GRID_W)
    return jnp.concatenate([tab[:, :-1], tab[:, 1:]], axis=-1)


def _na_attention(z, bias_tab, bsz, seq):
    rows = seq // GRID_W
    assert rows >= NA_BAND_ROWS and rows % NA_ROW_BLOCK == 0
    z4 = z.reshape(bsz, rows, GRID_W, Z_COLS)
    half = NA_WIN_ROWS // 2

    def kv_spec(col):
        return pl.BlockSpec(
            (pl.Squeezed(), pl.Element(NA_BAND_ROWS), pl.Element(GRID_W), pl.Element(NA_WIDTH)),
            lambda b, i: (b, jnp.clip(i * NA_ROW_BLOCK - half, 0, rows - NA_BAND_ROWS), 0, col))

    qblk = Z_NA // NA_WIDTH
    out = pl.pallas_call(
        functools.partial(_na_kernel, rows=rows),
        grid=(bsz, rows // NA_ROW_BLOCK),
        in_specs=[
            pl.BlockSpec((None, NA_ROW_BLOCK, GRID_W, NA_WIDTH), lambda b, i: (b, i, 0, qblk)),
            kv_spec(Z_NA + NA_WIDTH),
            kv_spec(Z_NA + 2 * NA_WIDTH),
            pl.BlockSpec(bias_tab.shape, lambda b, i: (0, 0, 0, 0)),
        ],
        out_specs=pl.BlockSpec((None, NA_ROW_BLOCK, GRID_W, NA_WIDTH), lambda b, i: (b, i, 0, 0)),
        out_shape=jax.ShapeDtypeStruct((bsz, rows, GRID_W, NA_WIDTH), BF16),
        scratch_shapes=[pltpu.VMEM((NA_BAND_ROWS, GRID_W, NA_WIDTH), BF16)] * 2,
        compiler_params=_cparams(("parallel", "arbitrary")),
        name="na_attn",
    )(z4, z4, z4, bias_tab)
    return out.reshape(bsz * seq, NA_WIDTH)


def _softplus(u):
    return jnp.maximum(u, 0.0) + jnp.log1p(jnp.exp(-jnp.abs(u)))


def _rwkv_kernel(zc_ref, zp_ref, zn_ref, cw_ref, d0_ref, d2_ref, a0_ref, a2_ref, g2_ref,
                 kk_ref, ka_ref, rk_ref, bones_ref, tri_ref,
                 y_ref, bonus_ref, gate_ref, s_ref, *bufs, tile, nt, reverse):
    i = pl.program_id(1)
    ti = nt - 1 - i if reverse else i

    @pl.when(i == 0)
    def _():
        s_ref[...] = jnp.zeros_like(s_ref)

    _rwkv_prep_tile(zc_ref, zp_ref, zn_ref, cw_ref, d0_ref, d2_ref, a0_ref, a2_ref, g2_ref,
                    kk_ref, ka_ref, rk_ref, bones_ref, tri_ref, bonus_ref, gate_ref, *bufs,
                    tile=tile, nt=nt, ti=ti, reverse=reverse)
    _rwkv_scan_tile(y_ref, s_ref, *bufs, tile=tile, reverse=reverse)


def _rwkv_prep_tile(zc_ref, zp_ref, zn_ref, cw_ref, d0_ref, d2_ref, a0_ref, a2_ref, g2_ref,
                    kk_ref, ka_ref, rk_ref, bones_ref, tri_ref, bonus_ref, gate_ref,
                    rt_ref, at_ref, bt_ref, kt_ref, bh_ref, kh_ref, v_ref, tot_ref,
                    *, tile, nt, ti, reverse):
    n_chunks = tile // CHUNK

    zc = zc_ref[...].astype(F32)
    prev_row = jnp.where(ti == 0, 0.0, zp_ref[...].astype(F32)[7:8, :])
    next_row = jnp.where(ti == nt - 1, 0.0, zn_ref[...].astype(F32)[0:1, :])
    zf = (pltpu.roll(zc, 1, axis=0) * cw_ref[0:1, :] + zc * cw_ref[1:2, :]
          + pltpu.roll(zc, tile - 1, axis=0) * cw_ref[2:3, :])
    rows8 = lax.broadcasted_iota(jnp.int32, (8, 1), 0)
    fix_first = jnp.where(rows8 == 0, (prev_row - zc[tile - 1:tile, :]) * cw_ref[0:1, :], 0.0)
    fix_last = jnp.where(rows8 == 7, (next_row - zc[0:1, :]) * cw_ref[2:3, :], 0.0)
    zf = jnp.concatenate([zf[0:8] + fix_first, zf[8:tile - 8], zf[tile - 8:tile] + fix_last], axis=0)

    o1, o2, o3 = RW_WIDTH, 2 * RW_WIDTH, 3 * RW_WIDTH
    o4 = o3 + 2 * DECAY_LORA
    o5 = o4 + 2 * AAA_LORA
    r = zf[:, 0:o1]
    k = zf[:, o1:o2]
    v = zf[:, o2:o3]
    xw = zf[:, o3:o4]
    xa = zf[:, o4:o5]
    xg = zf[:, o5:RW_COLS_PAD]

    bones = bones_ref[...]
    kk = k * kk_ref[...]
    kk = kk * lax.rsqrt(jnp.maximum(_dot((kk * kk).astype(BF16), bones), 1e-24))
    gate_ref[...] = _dot(jax.nn.sigmoid(xg).astype(BF16), g2_ref[...]).astype(gate_ref.dtype)

    w_log = -_softplus(-(d0_ref[...] + _dot(jnp.tanh(xw).astype(BF16), d2_ref[...]))) - 0.5
    logw = -jnp.exp(w_log)
    a = jax.nn.sigmoid(a0_ref[...] + _dot(xa.astype(BF16), a2_ref[...]))
    k_d = k * (1.0 + (a - 1.0) * ka_ref[...])
    b = kk * a
    bonus_ref[...] = (_dot((r * k_d * rk_ref[...]).astype(BF16), bones) * v).astype(bonus_ref.dtype)

    span = tri_ref.shape[0]
    g = jnp.concatenate([_sel_left(tri_ref[...], logw[r:r + span]) for r in range(0, tile, span)], axis=0)
    last = 0 if reverse else CHUNK - 1
    tot = jnp.concatenate(
        [jnp.broadcast_to(g[c * CHUNK + last:c * CHUNK + last + 1, :], (CHUNK, RW_WIDTH))
         for c in range(n_chunks)], axis=0)
    eng = jnp.exp(-g)
    ehat = jnp.exp(tot - g)
    rt_ref[...] = r * jnp.exp(g)
    at_ref[...] = -kk * jnp.exp(g - logw)
    bt_ref[...] = b * eng
    kt_ref[...] = k_d * eng
    bh_ref[...] = b * ehat
    kh_ref[...] = k_d * ehat
    v_ref[...] = v
    tot_ref[...] = tot


def _rwkv_scan_tile(y_ref, s_ref, rt_ref, at_ref, bt_ref, kt_ref, bh_ref, kh_ref, v_ref, tot_ref,
                    *, tile, reverse):
    n_chunks = tile // CHUNK

    ri = lax.broadcasted_iota(jnp.int32, (PAIR, PAIR), 0)
    ci = lax.broadcasted_iota(jnp.int32, (PAIR, PAIR), 1)
    same_head = jnp.where(jnp.right_shift(ri, 6) == jnp.right_shift(ci, 6), 1.0, 0.0).astype(F32)
    order = ci - ri if reverse else ri - ci
    m_strict = jnp.where(order > 0, same_head, 0.0)
    m_incl = jnp.where(order >= 0, same_head, 0.0)
    eye = jnp.where(ci == ri, 1.0, 0.0).astype(F32)
    lvl = []
    for q in range(INV_LEVELS):
        in_block = jnp.where(jnp.right_shift(ri, q + 1) == jnp.right_shift(ci, q + 1), 1.0, 0.0).astype(F32)
        lvl.append(jnp.where(jnp.right_shift(ri, q) != jnp.right_shift(ci, q), in_block, 0.0))
    lane = lax.broadcasted_iota(jnp.int32, (CHUNK, PAIR), 1)
    left = lane < RW_HEAD_DIM

    def stack(x):
        return jnp.concatenate([jnp.where(left, x, 0.0), jnp.where(left, 0.0, x)], axis=0)

    groups = [(c, p) for c in range(n_chunks) for p in range(N_PAIRS)]
    eye_b = eye.astype(BF16)

    def tile_of(ref, c, p):
        return ref[c * CHUNK:(c + 1) * CHUNK, p * PAIR:(p + 1) * PAIR]

    at_b = [stack(tile_of(at_ref, c, p)).astype(BF16) for c, p in groups]
    vs_b = [stack(tile_of(v_ref, c, p)).astype(BF16) for c, p in groups]
    o1_ = []
    for gi, (c, p) in enumerate(groups):
        bt = tile_of(bt_ref, c, p).astype(BF16)
        kt = tile_of(kt_ref, c, p).astype(BF16)
        lhs1 = jnp.concatenate([at_b[gi], stack(tile_of(rt_ref, c, p)).astype(BF16)], axis=0)
        o1_.append(_dot_nt(lhs1, jnp.concatenate([bt, bt, kt, kt], axis=0)))
    ms_b = m_strict.astype(BF16)
    mi_b = m_incl.astype(BF16)
    lvl_b = [m.astype(BF16) for m in lvl]
    n_b = [o[0:PAIR, 0:PAIR].astype(BF16) * ms_b for o in o1_]
    a_ak = [o[0:PAIR, PAIR:].astype(BF16) * ms_b for o in o1_]
    a_r = [jnp.concatenate([o[PAIR:, 0:PAIR].astype(BF16) * mi_b, o[PAIR:, PAIR:].astype(BF16) * mi_b], axis=1)
           for o in o1_]

    t_inv = [eye.astype(BF16) + n * lvl_b[0] for n in n_b]
    for q in range(1, INV_LEVELS):
        blk = 1 << q
        if blk < 16:
            xq = [_dot(n * lvl_b[q], t) for n, t in zip(n_b, t_inv)]
            t_inv = [t + _dot(t, x.astype(BF16)).astype(BF16) for t, x in zip(t_inv, xq)]
            continue
        n_blocks = PAIR // blk
        late = [k for k in range(n_blocks) if (k % 2 == 0) == reverse]

        def take(x):
            return jnp.concatenate([x[k * blk:(k + 1) * blk] for k in late], axis=0)

        def spread(x):
            zero = jnp.zeros((blk, PAIR), x.dtype)
            return jnp.concatenate(
                [x[late.index(k) * blk:(late.index(k) + 1) * blk] if k in late else zero
                 for k in range(n_blocks)], axis=0)

        xq = [_dot(take(n) * take(lvl_b[q]), t) for n, t in zip(n_b, t_inv)]
        upd = [_dot(take(t), spread(x.astype(BF16))) for t, x in zip(t_inv, xq)]
        t_inv = [t + spread(u.astype(BF16)) for t, u in zip(t_inv, upd)]

    av = [_dot(a, v_) for a, v_ in zip(a_ak, vs_b)]
    pu = [_dot(t, jnp.concatenate([a, x.astype(BF16)], axis=1))
          for t, a, x in zip(t_inv, at_b, av)]
    zero_b = jnp.zeros((PAIR, PAIR), BF16)
    qb, y_loc, n_st = [], [], []
    for gi, (c, p) in enumerate(groups):
        bh_t = stack(tile_of(bh_ref, c, p)).T
        kh_t = stack(tile_of(kh_ref, c, p)).T
        wmat = jnp.concatenate(
            [pu[gi].astype(BF16), jnp.concatenate([zero_b, vs_b[gi]], axis=1)], axis=0)
        lhs2 = jnp.concatenate([a_r[gi], jnp.concatenate([bh_t, kh_t], axis=1).astype(BF16)], axis=0)
        o2_ = _dot(lhs2, wmat)
        q_mat = stack(tile_of(rt_ref, c, p)) + o2_[0:PAIR, 0:PAIR]
        qb.append(jnp.concatenate([q_mat, o2_[PAIR:, 0:PAIR]], axis=0).astype(BF16))
        y_loc.append(o2_[0:PAIR, PAIR:])
        n_st.append(o2_[PAIR:, PAIR:])

    rep = PAIR // n_chunks
    w_tot = []
    for p in range(N_PAIRS):
        t_rows = jnp.concatenate(
            [tot_ref[c * CHUNK:c * CHUNK + rep, p * PAIR:(p + 1) * PAIR] for c in range(n_chunks)], axis=0)
        hi = t_rows.astype(BF16)
        r1 = t_rows - hi.astype(F32)
        mid = r1.astype(BF16)
        lo = (r1 - mid.astype(F32)).astype(BF16)
        cols = _dot_nt(eye_b, hi) + _dot_nt(eye_b, mid) + _dot_nt(eye_b, lo)
        w_tot.append([jnp.exp(cols[:, c * rep:c * rep + 1]) for c in range(n_chunks)])

    st = [s_ref[p] for p in range(N_PAIRS)]
    for c in (reversed(range(n_chunks)) if reverse else range(n_chunks)):
        o3_ = [_dot(qb[c * N_PAIRS + p], st[p].astype(BF16)) for p in range(N_PAIRS)]
        for p in range(N_PAIRS):
            gi = c * N_PAIRS + p
            y_bs = o3_[p][0:PAIR] + y_loc[gi]
            st[p] = st[p] * w_tot[p][c] + o3_[p][PAIR:] + n_st[gi]
            y_ref[c * CHUNK:(c + 1) * CHUNK, p * PAIR:(p + 1) * PAIR] = y_bs[0:CHUNK] + y_bs[CHUNK:]
    for p in range(N_PAIRS):
        s_ref[p] = st[p]


def _rwkv_scan(z, lp, bsz, seq, reverse):
    tile = min(512, seq)
    nt = seq // tile
    z3 = z.reshape(bsz, seq, Z_COLS)
    t8 = tile // 8
    d = int(reverse)

    def tidx(i):
        return nt - 1 - i if reverse else i

    span = min(256, tile)
    cidx = np.arange(span)
    same = (cidx[:, None] // CHUNK) == (cidx[None, :] // CHUNK)
    upto = cidx[None, :] >= cidx[:, None] if reverse else cidx[None, :] <= cidx[:, None]
    tri = jnp.asarray(same & upto, BF16)

    def full(shape):
        return pl.BlockSpec(shape, lambda b, i: (0,) * len(shape))

    def per_dir(shape):
        return pl.BlockSpec((None,) + shape, lambda b, i: (d,) + (0,) * len(shape))

    out_spec = pl.BlockSpec((None, tile, RW_WIDTH), lambda b, i: (b, tidx(i), 0))
    nat = pltpu.VMEM((tile, RW_WIDTH), F32)
    return pl.pallas_call(
        functools.partial(_rwkv_kernel, tile=tile, nt=nt, reverse=reverse),
        grid=(bsz, nt),
        in_specs=[
            pl.BlockSpec((None, tile, RW_COLS_PAD), lambda b, i: (b, tidx(i), 0)),
            pl.BlockSpec((None, 8, RW_COLS_PAD), lambda b, i: (b, jnp.maximum(tidx(i) * t8 - 1, 0), 0)),
            pl.BlockSpec((None, 8, RW_COLS_PAD),
                         lambda b, i: (b, jnp.minimum((tidx(i) + 1) * t8, seq // 8 - 1), 0)),
            full((3, RW_COLS_PAD)),
            per_dir((1, RW_WIDTH)), per_dir((2 * DECAY_LORA, RW_WIDTH)),
            per_dir((1, RW_WIDTH)), per_dir((2 * AAA_LORA, RW_WIDTH)),
            full((GATE_LORA_PAD, RW_WIDTH)),
            full((1, RW_WIDTH)), full((1, RW_WIDTH)), full((1, RW_WIDTH)),
            full((RW_WIDTH, RW_WIDTH)),
            full((span, span)),
        ],
        out_specs=[out_spec, out_spec, out_spec],
        out_shape=[jax.ShapeDtypeStruct((bsz, seq, RW_WIDTH), F32),
                   jax.ShapeDtypeStruct((bsz, seq, RW_WIDTH), BF16),
                   jax.ShapeDtypeStruct((bsz, seq, RW_WIDTH), BF16)],
        scratch_shapes=[pltpu.VMEM((N_PAIRS, PAIR, PAIR), F32)] + [nat] * 8,
        compiler_params=_cparams(("arbitrary", "arbitrary")),
        name="rwkv_bwd" if reverse else "rwkv_fwd",
    )(z3, z3, z3, lp["rw_conv"], lp["rw_decay0"], lp["rw_decay2"], lp["rw_a0"], lp["rw_a2"],
      lp["rw_g2"], lp["rw_k_k"], lp["rw_k_a"], lp["rw_r_k"], lp["bones"], tri)


def _memkv_kernel(m_ref, g_ref, w_ref, o_ref):
    o_ref[...] = _dot(_rms(m_ref[...], g_ref[...]).astype(BF16), w_ref[...]).astype(o_ref.dtype)


def _mem_kv(mem, gain, w):
    bsz = mem.shape[0]
    return pl.pallas_call(
        _memkv_kernel,
        grid=(bsz,),
        in_specs=[
            pl.BlockSpec((None, MEM_TOKENS, D_MODEL), lambda b: (b, 0, 0)),
            pl.BlockSpec((1, D_MODEL), lambda b: (0, 0)),
            pl.BlockSpec((D_MODEL, 2 * MEM_WIDTH), lambda b: (0, 0)),
        ],
        out_specs=pl.BlockSpec((None, MEM_TOKENS, 2 * MEM_WIDTH), lambda b: (b, 0, 0)),
        out_shape=jax.ShapeDtypeStruct((bsz, MEM_TOKENS, 2 * MEM_WIDTH), BF16),
        compiler_params=_cparams(("parallel",)),
        name="mem_kv",
    )(mem, gain, w)


def _memattn_kernel(q_ref, kv_ref, o_ref):
    scale = MEM_HEAD_DIM ** -0.5
    outs = []
    for h in range(MEM_HEADS):
        sl = slice(h * MEM_HEAD_DIM, (h + 1) * MEM_HEAD_DIM)
        vsl = slice(MEM_WIDTH + h * MEM_HEAD_DIM, MEM_WIDTH + (h + 1) * MEM_HEAD_DIM)
        s = _dot_nt(q_ref[:, sl], kv_ref[:, sl]) * scale
        m = jnp.max(s, axis=-1, keepdims=True)
        p = jnp.exp(s - m)
        l = jnp.sum(p, axis=-1, keepdims=True)
        outs.append(_dot(p.astype(BF16), kv_ref[:, vsl]) / l)
    o_ref[...] = jnp.concatenate(outs, axis=-1).astype(o_ref.dtype)


def _mem_attention(z, kv, bsz, seq):
    tm = min(1024, seq)
    z3 = z.reshape(bsz, seq, Z_COLS)
    out = pl.pallas_call(
        _memattn_kernel,
        grid=(bsz, seq // tm),
        in_specs=[
            pl.BlockSpec((None, tm, MEM_WIDTH), lambda b, i: (b, i, Z_MEM // MEM_WIDTH)),
            pl.BlockSpec((None, MEM_TOKENS, 2 * MEM_WIDTH), lambda b, i: (b, 0, 0)),
        ],
        out_specs=pl.BlockSpec((None, tm, MEM_WIDTH), lambda b, i: (b, i, 0)),
        out_shape=jax.ShapeDtypeStruct((bsz, seq, MEM_WIDTH), BF16),
        compiler_params=_cparams(("parallel", "parallel")),
        name="mem_attn",
    )(z3, kv)
    return out.reshape(bsz * seq, MEM_WIDTH)


def _merge_kernel(x_ref, ona_ref, y0_ref, y1_ref, b0_ref, b1_ref, gt_ref, omem_ref,
                  g0_ref, g1_ref, g2_ref, wb_ref, wo_ref, lw_ref, lb_ref, bones_ref, o_ref):
    bones = bones_ref[...]
    inv_n = 1.0 / RW_HEAD_DIM
    y = y0_ref[...] + y1_ref[...]
    mu = _dot(y.astype(BF16), bones) * inv_n
    dl = y - mu
    var = _dot((dl * dl).astype(BF16), bones) * inv_n
    yn = dl * lax.rsqrt(var + GN_EPS)
    bonus = b0_ref[...].astype(F32) + b1_ref[...].astype(F32)
    o_rw = (yn * lw_ref[...] + lb_ref[...] + bonus) * gt_ref[...].astype(F32)

    merged = jax.nn.sigmoid(g0_ref[...].astype(F32)) * _dot(ona_ref[...], wb_ref[0])
    merged = merged + jax.nn.sigmoid(g1_ref[...].astype(F32)) * _dot(o_rw.astype(BF16), wb_ref[1])
    merged = merged + jax.nn.sigmoid(g2_ref[...].astype(F32)) * _dot(omem_ref[...], wb_ref[2])
    o_ref[...] = x_ref[...] + _dot(merged.astype(BF16), wo_ref[...])


def _merge(x2d, z, o_na, rw_fwd, rw_bwd, o_mem, lp):
    t = x2d.shape[0]
    tm = min(512, t)
    y0, b0, gate = (a.reshape(t, RW_WIDTH) for a in rw_fwd)
    y1, b1, _ = (a.reshape(t, RW_WIDTH) for a in rw_bwd)
    gblk = Z_GATE // D_MODEL

    def tok(width):
        return pl.BlockSpec((tm, width), lambda i: (i, 0))

    def const(shape):
        return pl.BlockSpec(shape, lambda i: (0,) * len(shape))

    return pl.pallas_call(
        _merge_kernel,
        grid=(t // tm,),
        in_specs=[
            tok(D_MODEL), tok(NA_WIDTH), tok(RW_WIDTH), tok(RW_WIDTH), tok(RW_WIDTH), tok(RW_WIDTH), tok(RW_WIDTH),
            tok(MEM_WIDTH),
            pl.BlockSpec((tm, D_MODEL), lambda i: (i, gblk)),
            pl.BlockSpec((tm, D_MODEL), lambda i: (i, gblk + 1)),
            pl.BlockSpec((tm, D_MODEL), lambda i: (i, gblk + 2)),
            const((N_BRANCH, NA_WIDTH, D_MODEL)), const((D_MODEL, D_MODEL)),
            const((1, RW_WIDTH)), const((1, RW_WIDTH)), const((RW_WIDTH, RW_WIDTH)),
        ],
        out_specs=tok(D_MODEL),
        out_shape=jax.ShapeDtypeStruct((t, D_MODEL), F32),
        compiler_params=_cparams(("parallel",)),
        name="merge",
    )(x2d, o_na, y0, y1, b0, b1, gate, o_mem, z, z, z, lp["w_branch"], lp["w_out"],
      lp["rw_lnx_w"], lp["rw_lnx_b"], lp["bones"])


def _ffn_kernel(x_ref, xp_ref, xn_ref, g_ref, wv_ref, wg_ref, cv_ref, cg_ref, bv_ref, bg_ref, wd_ref,
                fg_ref, o_ref, h_ref, acc_ref, *, tm, tiles_per_seq, final_norm):
    i = pl.program_id(0)
    j = pl.program_id(1)
    ext = tm + 16

    n_parts = 4 if tm % 64 == 0 else 1
    pm = tm // n_parts

    def conv(u, cw, bias):
        return (pltpu.roll(u, 1, axis=0)[8:8 + pm] * cw[0:1, :] + u[8:8 + pm] * cw[1:2, :]
                + pltpu.roll(u, pm + 15, axis=0)[8:8 + pm] * cw[2:3, :] + bias)

    def body(first):
        ups = []
        for part in range(n_parts):
            if first:
                gain = g_ref[...]
                seq_pos = i % tiles_per_seq
                lo = part * pm + (8 if part else 0)
                hi = (part + 1) * pm + (8 if part + 1 < n_parts else 0)
                if part == 0:
                    h_ref[0:8, :] = jnp.where(seq_pos == 0, 0.0, _rms(xp_ref[...], gain)).astype(BF16)
                h_ref[8 + lo:8 + hi, :] = _rms(x_ref[lo:hi, :], gain).astype(BF16)
                if part == n_parts - 1:
                    h_ref[8 + tm:ext, :] = jnp.where(
                        seq_pos == tiles_per_seq - 1, 0.0, _rms(xn_ref[...], gain)).astype(BF16)
            hx = h_ref[part * pm:part * pm + pm + 16, :]
            ups.append((_dot(hx, wv_ref[...]), _dot(hx, wg_ref[...])))
        for part, (uv, ug) in enumerate(ups):
            u_val = conv(uv, cv_ref[...], bv_ref[...])
            u_gate = conv(ug, cg_ref[...], bg_ref[...])
            act = (u_gate * jax.nn.sigmoid(u_gate) * u_val).astype(BF16)
            down = _dot(act, wd_ref[...])
            if first:
                acc_ref[part * pm:(part + 1) * pm, :] = down
            else:
                acc_ref[part * pm:(part + 1) * pm, :] += down

    pl.when(j == 0)(functools.partial(body, True))
    pl.when(j > 0)(functools.partial(body, False))

    @pl.when(j == pl.num_programs(1) - 1)
    def _():
        out = x_ref[...] + acc_ref[...]
        if final_norm:
            out = _rms(out, fg_ref[...])
        o_ref[...] = out


def _ffn(x2d, lp, final_gain, seq, final_norm):
    t = x2d.shape[0]
    tm = min(2048, seq)
    tn = 256
    nj = D_FF // tn
    t8 = tm // 8
    return pl.pallas_call(
        functools.partial(_ffn_kernel, tm=tm, tiles_per_seq=seq // tm, final_norm=final_norm),
        grid=(t // tm, nj),
        in_specs=[
            pl.BlockSpec((tm, D_MODEL), lambda i, j: (i, 0)),
            pl.BlockSpec((8, D_MODEL), lambda i, j: (jnp.maximum(i * t8 - 1, 0), 0)),
            pl.BlockSpec((8, D_MODEL), lambda i, j: (jnp.minimum((i + 1) * t8, t // 8 - 1), 0)),
            pl.BlockSpec((1, D_MODEL), lambda i, j: (0, 0)),
            pl.BlockSpec((D_MODEL, tn), lambda i, j: (0, j)),
            pl.BlockSpec((D_MODEL, tn), lambda i, j: (0, nj + j)),
            pl.BlockSpec((3, tn), lambda i, j: (0, j)),
            pl.BlockSpec((3, tn), lambda i, j: (0, nj + j)),
            pl.BlockSpec((1, tn), lambda i, j: (0, j)),
            pl.BlockSpec((1, tn), lambda i, j: (0, nj + j)),
            pl.BlockSpec((tn, D_MODEL), lambda i, j: (j, 0)),
            pl.BlockSpec((1, D_MODEL), lambda i, j: (0, 0)),
        ],
        out_specs=pl.BlockSpec((tm, D_MODEL), lambda i, j: (i, 0)),
        out_shape=jax.ShapeDtypeStruct((t, D_MODEL), F32),
        scratch_shapes=[pltpu.VMEM((tm + 16, D_MODEL), BF16), pltpu.VMEM((tm, D_MODEL), F32)],
        compiler_params=_cparams(("parallel", "arbitrary")),
        name="ffn",
    )(x2d, x2d, x2d, lp["ffn_norm"], lp["w_up"], lp["w_up"], lp["ffn_conv"], lp["ffn_conv"],
      lp["ffn_conv_b"], lp["ffn_conv_b"], lp["w_down"], final_gain)


def _prep_layer(l, attn_norm, w_in, na_rpb, rw_conv, rw_decay0, rw_decay2, rw_a0, rw_a2, rw_g2, rw_k_k,
                rw_k_a, rw_r_k, rw_lnx_w, rw_lnx_b, mem_norm, w_mem_kv, w_branch, w_out, ffn_norm, w_up,
                ffn_conv, ffn_conv_b, w_down):
    c1 = 3 * NA_WIDTH
    c2 = c1 + RW_COLS
    w = w_in[l].astype(BF16)
    pad_cols = RW_COLS_PAD - RW_COLS
    w_new = jnp.concatenate(
        [w[:, c1:c2], jnp.zeros((D_MODEL, pad_cols), w.dtype), w[:, :c1], w[:, c2:]], axis=1)

    def lora_rows(m):
        zero = jnp.zeros_like(m[0])
        return jnp.stack([jnp.concatenate([m[0], zero], 0), jnp.concatenate([zero, m[1]], 0)]).astype(BF16)

    head = np.arange(RW_WIDTH) // RW_HEAD_DIM
    return dict(
        attn_norm=attn_norm[l][None, :],
        w_in=w_new.astype(BF16),
        na_bias=_na_bias_table(na_rpb[l]),
        rw_conv=jnp.pad(rw_conv[l], ((0, 0), (0, pad_cols))),
        rw_decay0=rw_decay0[l][:, None, :],
        rw_decay2=lora_rows(rw_decay2[l]),
        rw_a0=rw_a0[l][:, None, :],
        rw_a2=lora_rows(rw_a2[l]),
        rw_g2=jnp.pad(rw_g2[l], ((0, GATE_LORA_PAD - GATE_LORA), (0, 0))).astype(BF16),
        rw_k_k=rw_k_k[l][None, :],
        rw_k_a=rw_k_a[l][None, :],
        rw_r_k=rw_r_k[l].reshape(1, RW_WIDTH),
        rw_lnx_w=rw_lnx_w[l][None, :],
        rw_lnx_b=rw_lnx_b[l][None, :],
        bones=jnp.asarray(head[:, None] == head[None, :], BF16),
        mem_norm=mem_norm[l][None, :],
        w_mem_kv=w_mem_kv[l].astype(BF16),
        w_branch=w_branch[l].astype(BF16),
        w_out=w_out[l].astype(BF16),
        ffn_norm=ffn_norm[l][None, :],
        w_up=w_up[l].astype(BF16),
        ffn_conv=ffn_conv[l],
        ffn_conv_b=ffn_conv_b[l][None, :],
        w_down=w_down[l].astype(BF16),
    )


def _layer(x2d, mem, lp, bsz, seq, final_gain, final_norm):
    z = _in_proj(x2d, lp["attn_norm"], lp["w_in"])
    o_na = _na_attention(z, lp["na_bias"], bsz, seq)
    rw_fwd = _rwkv_scan(z, lp, bsz, seq, reverse=False)
    rw_bwd = _rwkv_scan(z, lp, bsz, seq, reverse=True)
    o_mem = _mem_attention(z, _mem_kv(mem, lp["mem_norm"], lp["w_mem_kv"]), bsz, seq)
    x2d = _merge(x2d, z, o_na, rw_fwd, rw_bwd, o_mem, lp)
    return _ffn(x2d, lp, final_gain, seq, final_norm)


def _trunk(x, mem, layers, final_gain):
    bsz, seq, _ = x.shape
    x2d = x.reshape(bsz * seq, D_MODEL)
    for l, lp in enumerate(layers):
        x2d = _layer(x2d, mem, lp, bsz, seq, final_gain, l == len(layers) - 1)
    return x2d.reshape(bsz, seq, D_MODEL)


def kernel(x_prompt, x_sample, mem_prompt, mem_sample, attn_norm, w_in, na_rpb, rw_conv, rw_decay0, rw_decay2, rw_a0, rw_a2, rw_g2, rw_k_k, rw_k_a, rw_r_k, rw_lnx_w, rw_lnx_b, mem_norm, w_mem_kv, w_branch, w_out, ffn_norm, w_up, ffn_conv, ffn_conv_b, w_down, final_norm):
    params = (attn_norm, w_in, na_rpb, rw_conv, rw_decay0, rw_decay2, rw_a0, rw_a2, rw_g2, rw_k_k, rw_k_a,
              rw_r_k, rw_lnx_w, rw_lnx_b, mem_norm, w_mem_kv, w_branch, w_out, ffn_norm, w_up, ffn_conv,
              ffn_conv_b, w_down)
    layers = [_prep_layer(l, *params) for l in range(attn_norm.shape[0])]
    final_gain = final_norm[None, :]
    y_prompt = _trunk(x_prompt, mem_prompt, layers, final_gain)
    y_sample = _trunk(x_sample, mem_sample, layers, final_gain)
    return (y_prompt, y_sample)
```

```python
import functools

import jax
import jax.numpy as jnp
import numpy as np
from jax import lax
from jax.experimental import pallas as pl
from jax.experimental.pallas import tpu as pltpu

F32 = jnp.float32
BF16 = jnp.bfloat16

D_MODEL = 1024
GRID_W = 64
NA_HEADS = 8
NA_HEAD_DIM = 64
NA_WIDTH = 512
NA_WIN_ROWS = 8
NA_WIN_COLS = 16
RW_HEADS = 8
RW_HEAD_DIM = 64
RW_WIDTH = 512
DECAY_LORA = 64
AAA_LORA = 64
GATE_LORA = 160
GATE_LORA_PAD = 256
MEM_TOKENS = 256
MEM_HEADS = 4
MEM_HEAD_DIM = 128
MEM_WIDTH = 512
N_BRANCH = 3
D_FF = 2816
RMS_EPS = 1e-6
GN_EPS = 64e-5

RW_COLS = 3 * RW_WIDTH + 2 * DECAY_LORA + 2 * AAA_LORA + GATE_LORA
RW_COLS_PAD = 3 * RW_WIDTH + 2 * DECAY_LORA + 2 * AAA_LORA + GATE_LORA_PAD
Z_RW = 0
Z_NA = RW_COLS_PAD
Z_MEM = Z_NA + 3 * NA_WIDTH
Z_GATE = Z_MEM + MEM_WIDTH
Z_COLS = Z_GATE + N_BRANCH * D_MODEL

CHUNK = 64
PAIR = 2 * RW_HEAD_DIM
N_PAIRS = RW_HEADS // 2
INV_LEVELS = 6
NEG_BIG = -1e30
VMEM_LIMIT = 56 * 1024 * 1024


def _cparams(sem):
    return pltpu.CompilerParams(dimension_semantics=sem, vmem_limit_bytes=VMEM_LIMIT)


def _rms(x, gain):
    ms = jnp.mean(x * x, axis=-1, keepdims=True)
    return x * lax.rsqrt(ms + RMS_EPS) * gain


def _dot(a, b):
    return jnp.dot(a, b, preferred_element_type=F32)


def _dot_nt(a, b):
    return lax.dot_general(a, b, (((1,), (1,)), ((), ())), preferred_element_type=F32)


def _split_hi_lo(x):
    hi = x.astype(BF16)
    lo = (x - hi.astype(F32)).astype(BF16)
    return hi, lo


def _sel_left(sel, x):
    hi, lo = _split_hi_lo(x)
    return _dot(sel, hi) + _dot(sel, lo)


def _sel_right(x, sel):
    hi, lo = _split_hi_lo(x)
    return _dot(hi, sel) + _dot(lo, sel)


def _inproj_kernel(x_ref, g_ref, w_ref, o_ref, h_ref):
    j = pl.program_id(1)
    tm = x_ref.shape[0]
    n_parts = 4 if tm % 64 == 0 else 1
    pm = tm // n_parts

    @pl.when(j == 0)
    def _():
        for part in range(n_parts):
            rs = slice(part * pm, (part + 1) * pm)
            h = _rms(x_ref[rs, :], g_ref[...]).astype(BF16)
            h_ref[rs, :] = h
            o_ref[rs, :] = _dot(h, w_ref[...]).astype(o_ref.dtype)

    @pl.when(j > 0)
    def _():
        o_ref[...] = _dot(h_ref[...], w_ref[...]).astype(o_ref.dtype)


def _in_proj(x2d, gain, w):
    t, n = x2d.shape[0], w.shape[1]
    tm = min(2048, t)
    tn = 1792
    return pl.pallas_call(
        _inproj_kernel,
        grid=(t // tm, n // tn),
        in_specs=[
            pl.BlockSpec((tm, D_MODEL), lambda i, j: (i, 0)),
            pl.BlockSpec((1, D_MODEL), lambda i, j: (0, 0)),
            pl.BlockSpec((D_MODEL, tn), lambda i, j: (0, j)),
        ],
        out_specs=pl.BlockSpec((tm, tn), lambda i, j: (i, j)),
        out_shape=jax.ShapeDtypeStruct((t, n), BF16),
        scratch_shapes=[pltpu.VMEM((tm, D_MODEL), BF16)],
        compiler_params=_cparams(("parallel", "arbitrary")),
        name="in_proj",
    )(x2d, gain, w)


NA_ROW_BLOCK = 16
NA_BAND_ROWS = NA_ROW_BLOCK + NA_WIN_ROWS - 1


def _na_kernel(q_ref, k_ref, v_ref, t2_ref, o_ref, vl_ref, vr_ref, *, rows):
    half = NA_WIN_ROWS // 2
    band = NA_WIN_ROWS * GRID_W
    i0 = pl.program_id(1) * NA_ROW_BLOCK
    block_start = jnp.clip(i0 - half, 0, rows - NA_BAND_ROWS)
    scale = NA_HEAD_DIM ** -0.5

    lane = lax.broadcasted_iota(jnp.int32, (GRID_W, NA_WIDTH), 1)
    first = jnp.where(lane % PAIR < NA_HEAD_DIM, 1.0, 0.0).astype(BF16)
    second = jnp.where(lane % PAIR < NA_HEAD_DIM, 0.0, 1.0).astype(BF16)
    for r in range(NA_BAND_ROWS):
        v_row = v_ref[r]
        vl_ref[r] = v_row * first
        vr_ref[r] = v_row * second

    rows_per_iter = 8

    def rows_step(it, carry):
        s, v_sel = [], []
        for u in range(rows_per_iter):
            m = it * rows_per_iter + u
            i = i0 + m
            r0 = jnp.clip(i - half, 0, rows - NA_WIN_ROWS)
            off = r0 - block_start
            dr0 = (NA_WIN_ROWS - 1) - (i - r0)
            q = q_ref[m] * scale
            q_sel = (q * first, q * second)
            kb = k_ref[pl.ds(off, NA_WIN_ROWS)].reshape(band, NA_WIDTH)
            v_sel.append((vl_ref[pl.ds(off, NA_WIN_ROWS)].reshape(band, NA_WIDTH),
                          vr_ref[pl.ds(off, NA_WIN_ROWS)].reshape(band, NA_WIDTH)))
            for h in range(NA_HEADS):
                ps = slice((h // 2) * PAIR, (h // 2 + 1) * PAIR)
                bias = jnp.concatenate([t2_ref[h, dr0 + 2 * jj] for jj in range(NA_WIN_ROWS // 2)], axis=1)
                s.append(_dot_nt(q_sel[h % 2][:, ps], kb[:, ps]) + bias)
        mx = [jnp.max(x, axis=-1, keepdims=True) for x in s]
        p = [jnp.exp(x - m_) for x, m_ in zip(s, mx)]
        inv = [1.0 / jnp.sum(x, axis=-1, keepdims=True) for x in p]
        for u in range(rows_per_iter):
            outs = []
            for pp in range(NA_HEADS // 2):
                ps = slice(pp * PAIR, (pp + 1) * PAIR)
                e = u * NA_HEADS + 2 * pp
                o_l = _dot(p[e].astype(BF16), v_sel[u][0][:, ps])
                o_r = _dot(p[e + 1].astype(BF16), v_sel[u][1][:, ps])
                outs.append(o_l * inv[e] + o_r * inv[e + 1])
            o_ref[it * rows_per_iter + u] = jnp.concatenate(outs, axis=-1).astype(o_ref.dtype)
        return carry

    lax.fori_loop(0, NA_ROW_BLOCK // rows_per_iter, rows_step, 0)


def _na_bias_table(rpb):
    x = np.arange(GRID_W)[:, None]
    c = np.arange(GRID_W)[None, :]
    c0 = np.clip(x - NA_WIN_COLS // 2, 0, GRID_W - NA_WIN_COLS)
    valid = ((c >= c0) & (c < c0 + NA_WIN_COLS)).reshape(-1)
    dc = np.clip(c - x + (NA_WIN_COLS - 1), 0, 2 * NA_WIN_COLS - 2).reshape(-1)
    onehot = (dc[None, :] == np.arange(2 * NA_WIN_COLS - 1)[:, None]) & valid[None, :]
    tab = jnp.einsum('hrd,dn->hrn', rpb, jnp.asarray(onehot, F32), precision=lax.Precision.HIGHEST)
    tab = tab + jnp.asarray(np.where(valid, 0.0, NEG_BIG), F32)
    tab = tab.reshape(NA_HEADS, 2 * NA_WIN_ROWS - 1, GRID_W, GRID_W)
    return jnp.concatenate([tab[:, :-1], tab[:, 1:]], axis=-1)


def _na_attention(z, bias_tab, bsz, seq):
    rows = seq // GRID_W
    assert rows >= NA_BAND_ROWS and rows % NA_ROW_BLOCK == 0
    z4 = z.reshape(bsz, rows, GRID_W, Z_COLS)
    half = NA_WIN_ROWS // 2

    def kv_spec(col):
        return pl.BlockSpec(
            (pl.Squeezed(), pl.Element(NA_BAND_ROWS), pl.Element(GRID_W), pl.Element(NA_WIDTH)),
            lambda b, i: (b, jnp.clip(i * NA_ROW_BLOCK - half, 0, rows - NA_BAND_ROWS), 0, col))

    qblk = Z_NA // NA_WIDTH
    out = pl.pallas_call(
        functools.partial(_na_kernel, rows=rows),
        grid=(bsz, rows // NA_ROW_BLOCK),
        in_specs=[
            pl.BlockSpec((None, NA_ROW_BLOCK, GRID_W, NA_WIDTH), lambda b, i: (b, i, 0, qblk)),
            kv_spec(Z_NA + NA_WIDTH),
            kv_spec(Z_NA + 2 * NA_WIDTH),
            pl.BlockSpec(bias_tab.shape, lambda b, i: (0, 0, 0, 0)),
        ],
        out_specs=pl.BlockSpec((None, NA_ROW_BLOCK, GRID_W, NA_WIDTH), lambda b, i: (b, i, 0, 0)),
        out_shape=jax.ShapeDtypeStruct((bsz, rows, GRID_W, NA_WIDTH), BF16),
        scratch_shapes=[pltpu.VMEM((NA_BAND_ROWS, GRID_W, NA_WIDTH), BF16)] * 2,
        compiler_params=_cparams(("parallel", "arbitrary")),
        name="na_attn",
    )(z4, z4, z4, bias_tab)
    return out.reshape(bsz * seq, NA_WIDTH)


def _softplus(u):
    return jnp.maximum(u, 0.0) + jnp.log1p(jnp.exp(-jnp.abs(u)))


def _rwkv_kernel(zc_ref, zp_ref, zn_ref, cw_ref, d0_ref, d2_ref, a0_ref, a2_ref, g2_ref,
                 kk_ref, ka_ref, rk_ref, bones_ref, tri_ref,
                 y_ref, bonus_ref, gate_ref, s_ref, *bufs, tile, nt, reverse):
    i = pl.program_id(1)
    ti = nt - 1 - i if reverse else i

    @pl.when(i == 0)
    def _():
        s_ref[...] = jnp.zeros_like(s_ref)

    _rwkv_prep_tile(zc_ref, zp_ref, zn_ref, cw_ref, d0_ref, d2_ref, a0_ref, a2_ref, g2_ref,
                    kk_ref, ka_ref, rk_ref, bones_ref, tri_ref, bonus_ref, gate_ref, *bufs,
                    tile=tile, nt=nt, ti=ti, reverse=reverse)
    _rwkv_scan_tile(y_ref, s_ref, *bufs, tile=tile, reverse=reverse)


def _rwkv_prep_tile(zc_ref, zp_ref, zn_ref, cw_ref, d0_ref, d2_ref, a0_ref, a2_ref, g2_ref,
                    kk_ref, ka_ref, rk_ref, bones_ref, tri_ref, bonus_ref, gate_ref,
                    rt_ref, at_ref, bt_ref, kt_ref, bh_ref, kh_ref, v_ref, tot_ref,
                    *, tile, nt, ti, reverse):
    n_chunks = tile // CHUNK

    zc = zc_ref[...].astype(F32)
    prev_row = jnp.where(ti == 0, 0.0, zp_ref[...].astype(F32)[7:8, :])
    next_row = jnp.where(ti == nt - 1, 0.0, zn_ref[...].astype(F32)[0:1, :])
    zf = (pltpu.roll(zc, 1, axis=0) * cw_ref[0:1, :] + zc * cw_ref[1:2, :]
          + pltpu.roll(zc, tile - 1, axis=0) * cw_ref[2:3, :])
    rows8 = lax.broadcasted_iota(jnp.int32, (8, 1), 0)
    fix_first = jnp.where(rows8 == 0, (prev_row - zc[tile - 1:tile, :]) * cw_ref[0:1, :], 0.0)
    fix_last = jnp.where(rows8 == 7, (next_row - zc[0:1, :]) * cw_ref[2:3, :], 0.0)
    zf = jnp.concatenate([zf[0:8] + fix_first, zf[8:tile - 8], zf[tile - 8:tile] + fix_last], axis=0)

    o1, o2, o3 = RW_WIDTH, 2 * RW_WIDTH, 3 * RW_WIDTH
    o4 = o3 + 2 * DECAY_LORA
    o5 = o4 + 2 * AAA_LORA
    r = zf[:, 0:o1]
    k = zf[:, o1:o2]
    v = zf[:, o2:o3]
    xw = zf[:, o3:o4]
    xa = zf[:, o4:o5]
    xg = zf[:, o5:RW_COLS_PAD]

    bones = bones_ref[...]
    kk = k * kk_ref[...]
    kk = kk * lax.rsqrt(jnp.maximum(_dot((kk * kk).astype(BF16), bones), 1e-24))
    gate_ref[...] = _dot(jax.nn.sigmoid(xg).astype(BF16), g2_ref[...]).astype(gate_ref.dtype)

    w_log = -_softplus(-(d0_ref[...] + _dot(jnp.tanh(xw).astype(BF16), d2_ref[...]))) - 0.5
    logw = -jnp.exp(w_log)
    a = jax.nn.sigmoid(a0_ref[...] + _dot(xa.astype(BF16), a2_ref[...]))
    k_d = k * (1.0 + (a - 1.0) * ka_ref[...])
    b = kk * a
    bonus_ref[...] = (_dot((r * k_d * rk_ref[...]).astype(BF16), bones) * v).astype(bonus_ref.dtype)

    span = tri_ref.shape[0]
    g = jnp.concatenate([_sel_left(tri_ref[...], logw[r:r + span]) for r in range(0, tile, span)], axis=0)
    last = 0 if reverse else CHUNK - 1
    tot = jnp.concatenate(
        [jnp.broadcast_to(g[c * CHUNK + last:c * CHUNK + last + 1, :], (CHUNK, RW_WIDTH))
         for c in range(n_chunks)], axis=0)
    eng = jnp.exp(-g)
    ehat = jnp.exp(tot - g)
    rt_ref[...] = r * jnp.exp(g)
    at_ref[...] = -kk * jnp.exp(g - logw)
    bt_ref[...] = b * eng
    kt_ref[...] = k_d * eng
    bh_ref[...] = b * ehat
    kh_ref[...] = k_d * ehat
    v_ref[...] = v
    tot_ref[...] = tot


def _rwkv_scan_tile(y_ref, s_ref, rt_ref, at_ref, bt_ref, kt_ref, bh_ref, kh_ref, v_ref, tot_ref,
                    *, tile, reverse):
    n_chunks = tile // CHUNK

    ri = lax.broadcasted_iota(jnp.int32, (PAIR, PAIR), 0)
    ci = lax.broadcasted_iota(jnp.int32, (PAIR, PAIR), 1)
    same_head = jnp.where(jnp.right_shift(ri, 6) == jnp.right_shift(ci, 6), 1.0, 0.0).astype(F32)
    order = ci - ri if reverse else ri - ci
    m_strict = jnp.where(order > 0, same_head, 0.0)
    m_incl = jnp.where(order >= 0, same_head, 0.0)
    eye = jnp.where(ci == ri, 1.0, 0.0).astype(F32)
    lvl = []
    for q in range(INV_LEVELS):
        in_block = jnp.where(jnp.right_shift(ri, q + 1) == jnp.right_shift(ci, q + 1), 1.0, 0.0).astype(F32)
        lvl.append(jnp.where(jnp.right_shift(ri, q) != jnp.right_shift(ci, q), in_block, 0.0))
    lane = lax.broadcasted_iota(jnp.int32, (CHUNK, PAIR), 1)
    left = lane < RW_HEAD_DIM

    def stack(x):
        return jnp.concatenate([jnp.where(left, x, 0.0), jnp.where(left, 0.0, x)], axis=0)

    groups = [(c, p) for c in range(n_chunks) for p in range(N_PAIRS)]
    eye_b = eye.astype(BF16)

    def tile_of(ref, c, p):
        return ref[c * CHUNK:(c + 1) * CHUNK, p * PAIR:(p + 1) * PAIR]

    at_b = [stack(tile_of(at_ref, c, p)).astype(BF16) for c, p in groups]
    vs_b = [stack(tile_of(v_ref, c, p)).astype(BF16) for c, p in groups]
    o1_ = []
    for gi, (c, p) in enumerate(groups):
        bt = tile_of(bt_ref, c, p).astype(BF16)
        kt = tile_of(kt_ref, c, p).astype(BF16)
        lhs1 = jnp.concatenate([at_b[gi], stack(tile_of(rt_ref, c, p)).astype(BF16)], axis=0)
        o1_.append(_dot_nt(lhs1, jnp.concatenate([bt, bt, kt, kt], axis=0)))
    ms_b = m_strict.astype(BF16)
    mi_b = m_incl.astype(BF16)
    lvl_b = [m.astype(BF16) for m in lvl]
    n_b = [o[0:PAIR, 0:PAIR].astype(BF16) * ms_b for o in o1_]
    a_ak = [o[0:PAIR, PAIR:].astype(BF16) * ms_b for o in o1_]
    a_r = [jnp.concatenate([o[PAIR:, 0:PAIR].astype(BF16) * mi_b, o[PAIR:, PAIR:].astype(BF16) * mi_b], axis=1)
           for o in o1_]

    t_inv = [eye.astype(BF16) + n * lvl_b[0] for n in n_b]
    for q in range(1, INV_LEVELS):
        blk = 1 << q
        if blk < 16:
            xq = [_dot(n * lvl_b[q], t) for n, t in zip(n_b, t_inv)]
            t_inv = [t + _dot(t, x.astype(BF16)).astype(BF16) for t, x in zip(t_inv, xq)]
            continue
        n_blocks = PAIR // blk
        late = [k for k in range(n_blocks) if (k % 2 == 0) == reverse]

        def take(x):
            return jnp.concatenate([x[k * blk:(k + 1) * blk] for k in late], axis=0)

        def spread(x):
            zero = jnp.zeros((blk, PAIR), x.dtype)
            return jnp.concatenate(
                [x[late.index(k) * blk:(late.index(k) + 1) * blk] if k in late else zero
                 for k in range(n_blocks)], axis=0)

        xq = [_dot(take(n) * take(lvl_b[q]), t) for n, t in zip(n_b, t_inv)]
        upd = [_dot(take(t), spread(x.astype(BF16))) for t, x in zip(t_inv, xq)]
        t_inv = [t + spread(u.astype(BF16)) for t, u in zip(t_inv, upd)]

    av = [_dot(a, v_) for a, v_ in zip(a_ak, vs_b)]
    pu = [_dot(t, jnp.concatenate([a, x.astype(BF16)], axis=1))
          for t, a, x in zip(t_inv, at_b, av)]
    zero_b = jnp.zeros((PAIR, PAIR), BF16)
    qb, y_loc, n_st = [], [], []
    for gi, (c, p) in enumerate(groups):
        bh_t = stack(tile_of(bh_ref, c, p)).T
        kh_t = stack(tile_of(kh_ref, c, p)).T
        wmat = jnp.concatenate(
            [pu[gi].astype(BF16), jnp.concatenate([zero_b, vs_b[gi]], axis=1)], axis=0)
        lhs2 = jnp.concatenate([a_r[gi], jnp.concatenate([bh_t, kh_t], axis=1).astype(BF16)], axis=0)
        o2_ = _dot(lhs2, wmat)
        q_mat = stack(tile_of(rt_ref, c, p)) + o2_[0:PAIR, 0:PAIR]
        qb.append(jnp.concatenate([q_mat, o2_[PAIR:, 0:PAIR]], axis=0).astype(BF16))
        y_loc.append(o2_[0:PAIR, PAIR:])
        n_st.append(o2_[PAIR:, PAIR:])

    rep = PAIR // n_chunks
    w_tot = []
    for p in range(N_PAIRS):
        t_rows = jnp.concatenate(
            [tot_ref[c * CHUNK:c * CHUNK + rep, p * PAIR:(p + 1) * PAIR] for c in range(n_chunks)], axis=0)
        hi = t_rows.astype(BF16)
        r1 = t_rows - hi.astype(F32)
        mid = r1.astype(BF16)
        lo = (r1 - mid.astype(F32)).astype(BF16)
        cols = _dot_nt(eye_b, hi) + _dot_nt(eye_b, mid) + _dot_nt(eye_b, lo)
        w_tot.append([jnp.exp(cols[:, c * rep:c * rep + 1]) for c in range(n_chunks)])

    st = [s_ref[p] for p in range(N_PAIRS)]
    for c in (reversed(range(n_chunks)) if reverse else range(n_chunks)):
        o3_ = [_dot(qb[c * N_PAIRS + p], st[p].astype(BF16)) for p in range(N_PAIRS)]
        for p in range(N_PAIRS):
            gi = c * N_PAIRS + p
            y_bs = o3_[p][0:PAIR] + y_loc[gi]
            st[p] = st[p] * w_tot[p][c] + o3_[p][PAIR:] + n_st[gi]
            y_ref[c * CHUNK:(c + 1) * CHUNK, p * PAIR:(p + 1) * PAIR] = y_bs[0:CHUNK] + y_bs[CHUNK:]
    for p in range(N_PAIRS):
        s_ref[p] = st[p]


def _rwkv_scan(z, lp, bsz, seq, reverse):
    tile = min(512, seq)
    nt = seq // tile
    z3 = z.reshape(bsz, seq, Z_COLS)
    t8 = tile // 8
    d = int(reverse)

    def tidx(i):
        return nt - 1 - i if reverse else i

    span = min(256, tile)
    cidx = np.arange(span)
    same = (cidx[:, None] // CHUNK) == (cidx[None, :] // CHUNK)
    upto = cidx[None, :] >= cidx[:, None] if reverse else cidx[None, :] <= cidx[:, None]
    tri = jnp.asarray(same & upto, BF16)

    def full(shape):
        return pl.BlockSpec(shape, lambda b, i: (0,) * len(shape))

    def per_dir(shape):
        return pl.BlockSpec((None,) + shape, lambda b, i: (d,) + (0,) * len(shape))

    out_spec = pl.BlockSpec((None, tile, RW_WIDTH), lambda b, i: (b, tidx(i), 0))
    nat = pltpu.VMEM((tile, RW_WIDTH), F32)
    return pl.pallas_call(
        functools.partial(_rwkv_kernel, tile=tile, nt=nt, reverse=reverse),
        grid=(bsz, nt),
        in_specs=[
            pl.BlockSpec((None, tile, RW_COLS_PAD), lambda b, i: (b, tidx(i), 0)),
            pl.BlockSpec((None, 8, RW_COLS_PAD), lambda b, i: (b, jnp.maximum(tidx(i) * t8 - 1, 0), 0)),
            pl.BlockSpec((None, 8, RW_COLS_PAD),
                         lambda b, i: (b, jnp.minimum((tidx(i) + 1) * t8, seq // 8 - 1), 0)),
            full((3, RW_COLS_PAD)),
            per_dir((1, RW_WIDTH)), per_dir((2 * DECAY_LORA, RW_WIDTH)),
            per_dir((1, RW_WIDTH)), per_dir((2 * AAA_LORA, RW_WIDTH)),
            full((GATE_LORA_PAD, RW_WIDTH)),
            full((1, RW_WIDTH)), full((1, RW_WIDTH)), full((1, RW_WIDTH)),
            full((RW_WIDTH, RW_WIDTH)),
            full((span, span)),
        ],
        out_specs=[out_spec, out_spec, out_spec],
        out_shape=[jax.ShapeDtypeStruct((bsz, seq, RW_WIDTH), F32),
                   jax.ShapeDtypeStruct((bsz, seq, RW_WIDTH), BF16),
                   jax.ShapeDtypeStruct((bsz, seq, RW_WIDTH), BF16)],
        scratch_shapes=[pltpu.VMEM((N_PAIRS, PAIR, PAIR), F32)] + [nat] * 8,
        compiler_params=_cparams(("arbitrary", "arbitrary")),
        name="rwkv_bwd" if reverse else "rwkv_fwd",
    )(z3, z3, z3, lp["rw_conv"], lp["rw_decay0"], lp["rw_decay2"], lp["rw_a0"], lp["rw_a2"],
      lp["rw_g2"], lp["rw_k_k"], lp["rw_k_a"], lp["rw_r_k"], lp["bones"], tri)


def _memkv_kernel(m_ref, g_ref, w_ref, o_ref):
    o_ref[...] = _dot(_rms(m_ref[...], g_ref[...]).astype(BF16), w_ref[...]).astype(o_ref.dtype)


def _mem_kv(mem, gain, w):
    bsz = mem.shape[0]
    return pl.pallas_call(
        _memkv_kernel,
        grid=(bsz,),
        in_specs=[
            pl.BlockSpec((None, MEM_TOKENS, D_MODEL), lambda b: (b, 0, 0)),
            pl.BlockSpec((1, D_MODEL), lambda b: (0, 0)),
            pl.BlockSpec((D_MODEL, 2 * MEM_WIDTH), lambda b: (0, 0)),
        ],
        out_specs=pl.BlockSpec((None, MEM_TOKENS, 2 * MEM_WIDTH), lambda b: (b, 0, 0)),
        out_shape=jax.ShapeDtypeStruct((bsz, MEM_TOKENS, 2 * MEM_WIDTH), BF16),
        compiler_params=_cparams(("parallel",)),
        name="mem_kv",
    )(mem, gain, w)


def _memattn_kernel(q_ref, kv_ref, o_ref):
    scale = MEM_HEAD_DIM ** -0.5
    outs = []
    for h in range(MEM_HEADS):
        sl = slice(h * MEM_HEAD_DIM, (h + 1) * MEM_HEAD_DIM)
        vsl = slice(MEM_WIDTH + h * MEM_HEAD_DIM, MEM_WIDTH + (h + 1) * MEM_HEAD_DIM)
        s = _dot_nt(q_ref[:, sl], kv_ref[:, sl]) * scale
        m = jnp.max(s, axis=-1, keepdims=True)
        p = jnp.exp(s - m)
        l = jnp.sum(p, axis=-1, keepdims=True)
        outs.append(_dot(p.astype(BF16), kv_ref[:, vsl]) / l)
    o_ref[...] = jnp.concatenate(outs, axis=-1).astype(o_ref.dtype)


def _mem_attention(z, kv, bsz, seq):
    tm = min(1024, seq)
    z3 = z.reshape(bsz, seq, Z_COLS)
    out = pl.pallas_call(
        _memattn_kernel,
        grid=(bsz, seq // tm),
        in_specs=[
            pl.BlockSpec((None, tm, MEM_WIDTH), lambda b, i: (b, i, Z_MEM // MEM_WIDTH)),
            pl.BlockSpec((None, MEM_TOKENS, 2 * MEM_WIDTH), lambda b, i: (b, 0, 0)),
        ],
        out_specs=pl.BlockSpec((None, tm, MEM_WIDTH), lambda b, i: (b, i, 0)),
        out_shape=jax.ShapeDtypeStruct((bsz, seq, MEM_WIDTH), BF16),
        compiler_params=_cparams(("parallel", "parallel")),
        name="mem_attn",
    )(z3, kv)
    return out.reshape(bsz * seq, MEM_WIDTH)


def _merge_kernel(x_ref, ona_ref, y0_ref, y1_ref, b0_ref, b1_ref, gt_ref, omem_ref,
                  g0_ref, g1_ref, g2_ref, wb_ref, wo_ref, lw_ref, lb_ref, bones_ref, o_ref):
    bones = bones_ref[...]
    inv_n = 1.0 / RW_HEAD_DIM
    y = y0_ref[...] + y1_ref[...]
    mu = _dot(y.astype(BF16), bones) * inv_n
    dl = y - mu
    var = _dot((dl * dl).astype(BF16), bones) * inv_n
    yn = dl * lax.rsqrt(var + GN_EPS)
    bonus = b0_ref[...].astype(F32) + b1_ref[...].astype(F32)
    o_rw = (yn * lw_ref[...] + lb_ref[...] + bonus) * gt_ref[...].astype(F32)

    merged = jax.nn.sigmoid(g0_ref[...].astype(F32)) * _dot(ona_ref[...], wb_ref[0])
    merged = merged + jax.nn.sigmoid(g1_ref[...].astype(F32)) * _dot(o_rw.astype(BF16), wb_ref[1])
    merged = merged + jax.nn.sigmoid(g2_ref[...].astype(F32)) * _dot(omem_ref[...], wb_ref[2])
    o_ref[...] = x_ref[...] + _dot(merged.astype(BF16), wo_ref[...])


def _merge(x2d, z, o_na, rw_fwd, rw_bwd, o_mem, lp):
    t = x2d.shape[0]
    tm = min(512, t)
    y0, b0, gate = (a.reshape(t, RW_WIDTH) for a in rw_fwd)
    y1, b1, _ = (a.reshape(t, RW_WIDTH) for a in rw_bwd)
    gblk = Z_GATE // D_MODEL

    def tok(width):
        return pl.BlockSpec((tm, width), lambda i: (i, 0))

    def const(shape):
        return pl.BlockSpec(shape, lambda i: (0,) * len(shape))

    return pl.pallas_call(
        _merge_kernel,
        grid=(t // tm,),
        in_specs=[
            tok(D_MODEL), tok(NA_WIDTH), tok(RW_WIDTH), tok(RW_WIDTH), tok(RW_WIDTH), tok(RW_WIDTH), tok(RW_WIDTH),
            tok(MEM_WIDTH),
            pl.BlockSpec((tm, D_MODEL), lambda i: (i, gblk)),
            pl.BlockSpec((tm, D_MODEL), lambda i: (i, gblk + 1)),
            pl.BlockSpec((tm, D_MODEL), lambda i: (i, gblk + 2)),
            const((N_BRANCH, NA_WIDTH, D_MODEL)), const((D_MODEL, D_MODEL)),
            const((1, RW_WIDTH)), const((1, RW_WIDTH)), const((RW_WIDTH, RW_WIDTH)),
        ],
        out_specs=tok(D_MODEL),
        out_shape=jax.ShapeDtypeStruct((t, D_MODEL), F32),
        compiler_params=_cparams(("parallel",)),
        name="merge",
    )(x2d, o_na, y0, y1, b0, b1, gate, o_mem, z, z, z, lp["w_branch"], lp["w_out"],
      lp["rw_lnx_w"], lp["rw_lnx_b"], lp["bones"])


def _ffn_kernel(x_ref, xp_ref, xn_ref, g_ref, wv_ref, wg_ref, cv_ref, cg_ref, bv_ref, bg_ref, wd_ref,
                fg_ref, o_ref, h_ref, acc_ref, *, tm, tiles_per_seq, n_col_tiles, final_norm):
    i = pl.program_id(0)
    j = pl.program_id(1)
    ext = tm + 16

    n_parts = 4 if tm % 64 == 0 else 1
    pm = tm // n_parts

    def conv(u, cw, bias):
        return (pltpu.roll(u, 1, axis=0)[8:8 + pm] * cw[0:1, :] + u[8:8 + pm] * cw[1:2, :]
                + pltpu.roll(u, pm + 15, axis=0)[8:8 + pm] * cw[2:3, :] + bias)

    def body(first, last):
        ups = []
        for part in range(n_parts):
            if first:
                gain = g_ref[...]
                seq_pos = i % tiles_per_seq
                lo = part * pm + (8 if part else 0)
                hi = (part + 1) * pm + (8 if part + 1 < n_parts else 0)
                if part == 0:
                    h_ref[0:8, :] = jnp.where(seq_pos == 0, 0.0, _rms(xp_ref[...], gain)).astype(BF16)
                h_ref[8 + lo:8 + hi, :] = _rms(x_ref[lo:hi, :], gain).astype(BF16)
                if part == n_parts - 1:
                    h_ref[8 + tm:ext, :] = jnp.where(
                        seq_pos == tiles_per_seq - 1, 0.0, _rms(xn_ref[...], gain)).astype(BF16)
            hx = h_ref[part * pm:part * pm + pm + 16, :]
            ups.append((_dot(hx, wv_ref[...]), _dot(hx, wg_ref[...])))
        for part, (uv, ug) in enumerate(ups):
            u_val = conv(uv, cv_ref[...], bv_ref[...])
            u_gate = conv(ug, cg_ref[...], bg_ref[...])
            act = (u_gate * jax.nn.sigmoid(u_gate) * u_val).astype(BF16)
            down = _dot(act, wd_ref[...])
            rs = slice(part * pm, (part + 1) * pm)
            if first:
                acc_ref[rs, :] = down
            elif last:
                out = x_ref[rs, :] + (acc_ref[rs, :] + down)
                o_ref[rs, :] = _rms(out, fg_ref[...]) if final_norm else out
            else:
                acc_ref[rs, :] += down

    assert n_col_tiles >= 2
    pl.when(j == 0)(functools.partial(body, True, False))
    pl.when((j > 0) & (j < n_col_tiles - 1))(functools.partial(body, False, False))
    pl.when(j == n_col_tiles - 1)(functools.partial(body, False, True))


def _ffn(x2d, lp, final_gain, seq, final_norm):
    t = x2d.shape[0]
    tm = min(2048, seq)
    tn = 256
    nj = D_FF // tn
    t8 = tm // 8
    return pl.pallas_call(
        functools.partial(_ffn_kernel, tm=tm, tiles_per_seq=seq // tm, n_col_tiles=nj, final_norm=final_norm),
        grid=(t // tm, nj),
        in_specs=[
            pl.BlockSpec((tm, D_MODEL), lambda i, j: (i, 0)),
            pl.BlockSpec((8, D_MODEL), lambda i, j: (jnp.maximum(i * t8 - 1, 0), 0)),
            pl.BlockSpec((8, D_MODEL), lambda i, j: (jnp.minimum((i + 1) * t8, t // 8 - 1), 0)),
            pl.BlockSpec((1, D_MODEL), lambda i, j: (0, 0)),
            pl.BlockSpec((D_MODEL, tn), lambda i, j: (0, j)),
            pl.BlockSpec((D_MODEL, tn), lambda i, j: (0, nj + j)),
            pl.BlockSpec((3, tn), lambda i, j: (0, j)),
            pl.BlockSpec((3, tn), lambda i, j: (0, nj + j)),
            pl.BlockSpec((1, tn), lambda i, j: (0, j)),
            pl.BlockSpec((1, tn), lambda i, j: (0, nj + j)),
            pl.BlockSpec((tn, D_MODEL), lambda i, j: (j, 0)),
            pl.BlockSpec((1, D_MODEL), lambda i, j: (0, 0)),
        ],
        out_specs=pl.BlockSpec((tm, D_MODEL), lambda i, j: (i, 0)),
        out_shape=jax.ShapeDtypeStruct((t, D_MODEL), F32),
        scratch_shapes=[pltpu.VMEM((tm + 16, D_MODEL), BF16), pltpu.VMEM((tm, D_MODEL), F32)],
        compiler_params=_cparams(("parallel", "arbitrary")),
        name="ffn",
    )(x2d, x2d, x2d, lp["ffn_norm"], lp["w_up"], lp["w_up"], lp["ffn_conv"], lp["ffn_conv"],
      lp["ffn_conv_b"], lp["ffn_conv_b"], lp["w_down"], final_gain)


def _prep_layer(l, attn_norm, w_in, na_rpb, rw_conv, rw_decay0, rw_decay2, rw_a0, rw_a2, rw_g2, rw_k_k,
                rw_k_a, rw_r_k, rw_lnx_w, rw_lnx_b, mem_norm, w_mem_kv, w_branch, w_out, ffn_norm, w_up,
                ffn_conv, ffn_conv_b, w_down):
    c1 = 3 * NA_WIDTH
    c2 = c1 + RW_COLS
    w = w_in[l].astype(BF16)
    pad_cols = RW_COLS_PAD - RW_COLS
    w_new = jnp.concatenate(
        [w[:, c1:c2], jnp.zeros((D_MODEL, pad_cols), w.dtype), w[:, :c1], w[:, c2:]], axis=1)

    def lora_rows(m):
        zero = jnp.zeros_like(m[0])
        return jnp.stack([jnp.concatenate([m[0], zero], 0), jnp.concatenate([zero, m[1]], 0)]).astype(BF16)

    head = np.arange(RW_WIDTH) // RW_HEAD_DIM
    return dict(
        attn_norm=attn_norm[l][None, :],
        w_in=w_new.astype(BF16),
        na_bias=_na_bias_table(na_rpb[l]),
        rw_conv=jnp.pad(rw_conv[l], ((0, 0), (0, pad_cols))),
        rw_decay0=rw_decay0[l][:, None, :],
        rw_decay2=lora_rows(rw_decay2[l]),
        rw_a0=rw_a0[l][:, None, :],
        rw_a2=lora_rows(rw_a2[l]),
        rw_g2=jnp.pad(rw_g2[l], ((0, GATE_LORA_PAD - GATE_LORA), (0, 0))).astype(BF16),
        rw_k_k=rw_k_k[l][None, :],
        rw_k_a=rw_k_a[l][None, :],
        rw_r_k=rw_r_k[l].reshape(1, RW_WIDTH),
        rw_lnx_w=rw_lnx_w[l][None, :],
        rw_lnx_b=rw_lnx_b[l][None, :],
        bones=jnp.asarray(head[:, None] == head[None, :], BF16),
        mem_norm=mem_norm[l][None, :],
        w_mem_kv=w_mem_kv[l].astype(BF16),
        w_branch=w_branch[l].astype(BF16),
        w_out=w_out[l].astype(BF16),
        ffn_norm=ffn_norm[l][None, :],
        w_up=w_up[l].astype(BF16),
        ffn_conv=ffn_conv[l],
        ffn_conv_b=ffn_conv_b[l][None, :],
        w_down=w_down[l].astype(BF16),
    )


def _layer(x2d, mem, lp, bsz, seq, final_gain, final_norm):
    z = _in_proj(x2d, lp["attn_norm"], lp["w_in"])
    o_na = _na_attention(z, lp["na_bias"], bsz, seq)
    rw_fwd = _rwkv_scan(z, lp, bsz, seq, reverse=False)
    rw_bwd = _rwkv_scan(z, lp, bsz, seq, reverse=True)
    o_mem = _mem_attention(z, _mem_kv(mem, lp["mem_norm"], lp["w_mem_kv"]), bsz, seq)
    x2d = _merge(x2d, z, o_na, rw_fwd, rw_bwd, o_mem, lp)
    return _ffn(x2d, lp, final_gain, seq, final_norm)


def _trunk(x, mem, layers, final_gain):
    bsz, seq, _ = x.shape
    x2d = x.reshape(bsz * seq, D_MODEL)
    for l, lp in enumerate(layers):
        x2d = _layer(x2d, mem, lp, bsz, seq, final_gain, l == len(layers) - 1)
    return x2d.reshape(bsz, seq, D_MODEL)


def kernel(x_prompt, x_sample, mem_prompt, mem_sample, attn_norm, w_in, na_rpb, rw_conv, rw_decay0, rw_decay2, rw_a0, rw_a2, rw_g2, rw_k_k, rw_k_a, rw_r_k, rw_lnx_w, rw_lnx_b, mem_norm, w_mem_kv, w_branch, w_out, ffn_norm, w_up, ffn_conv, ffn_conv_b, w_down, final_norm):
    params = (attn_norm, w_in, na_rpb, rw_conv, rw_decay0, rw_decay2, rw_a0, rw_a2, rw_g2, rw_k_k, rw_k_a,
              rw_r_k, rw_lnx_w, rw_lnx_b, mem_norm, w_mem_kv, w_branch, w_out, ffn_norm, w_up, ffn_conv,
              ffn_conv_b, w_down)
    layers = [_prep_layer(l, *params) for l in range(attn_norm.shape[0])]
    final_gain = final_norm[None, :]
    y_prompt = _trunk(x_prompt, mem_prompt, layers, final_gain)
    y_sample = _trunk(x_sample, mem_sample, layers, final_gain)
    return (y_prompt, y_sample)
```

```python
import functools

import jax
import jax.numpy as jnp
import numpy as np
from jax import lax
from jax.experimental import pallas as pl
from jax.experimental.pallas import tpu as pltpu

F32 = jnp.float32
BF16 = jnp.bfloat16

D_MODEL = 1024
GRID_W = 64
NA_HEADS = 8
NA_HEAD_DIM = 64
NA_WIDTH = 512
NA_WIN_ROWS = 8
NA_WIN_COLS = 16
RW_HEADS = 8
RW_HEAD_DIM = 64
RW_WIDTH = 512
DECAY_LORA = 64
AAA_LORA = 64
GATE_LORA = 160
GATE_LORA_PAD = 256
MEM_TOKENS = 256
MEM_HEADS = 4
MEM_HEAD_DIM = 128
MEM_WIDTH = 512
N_BRANCH = 3
D_FF = 2816
RMS_EPS = 1e-6
GN_EPS = 64e-5

RW_COLS = 3 * RW_WIDTH + 2 * DECAY_LORA + 2 * AAA_LORA + GATE_LORA
RW_COLS_PAD = 3 * RW_WIDTH + 2 * DECAY_LORA + 2 * AAA_LORA + GATE_LORA_PAD
Z_RW = 0
Z_NA = RW_COLS_PAD
Z_MEM = Z_NA + 3 * NA_WIDTH
Z_GATE = Z_MEM + MEM_WIDTH
Z_COLS = Z_GATE + N_BRANCH * D_MODEL

CHUNK = 64
PAIR = 2 * RW_HEAD_DIM
N_PAIRS = RW_HEADS // 2
INV_LEVELS = 6
NEG_BIG = -1e30
VMEM_LIMIT = 56 * 1024 * 1024


def _cparams(sem):
    return pltpu.CompilerParams(dimension_semantics=sem, vmem_limit_bytes=VMEM_LIMIT)


def _rms(x, gain):
    ms = jnp.mean(x * x, axis=-1, keepdims=True)
    return x * lax.rsqrt(ms + RMS_EPS) * gain


def _dot(a, b):
    return jnp.dot(a, b, preferred_element_type=F32)


def _dot_nt(a, b):
    return lax.dot_general(a, b, (((1,), (1,)), ((), ())), preferred_element_type=F32)


def _split_hi_lo(x):
    hi = x.astype(BF16)
    lo = (x - hi.astype(F32)).astype(BF16)
    return hi, lo


def _sel_left(sel, x):
    hi, lo = _split_hi_lo(x)
    return _dot(sel, hi) + _dot(sel, lo)


def _sel_right(x, sel):
    hi, lo = _split_hi_lo(x)
    return _dot(hi, sel) + _dot(lo, sel)


def _inproj_kernel(x_ref, g_ref, w_ref, o_ref, h_ref):
    j = pl.program_id(1)
    tm = x_ref.shape[0]
    n_parts = 4 if tm % 64 == 0 else 1
    pm = tm // n_parts

    @pl.when(j == 0)
    def _():
        for part in range(n_parts):
            rs = slice(part * pm, (part + 1) * pm)
            h = _rms(x_ref[rs, :], g_ref[...]).astype(BF16)
            h_ref[rs, :] = h
            o_ref[rs, :] = _dot(h, w_ref[...]).astype(o_ref.dtype)

    @pl.when(j > 0)
    def _():
        o_ref[...] = _dot(h_ref[...], w_ref[...]).astype(o_ref.dtype)


def _in_proj(x2d, gain, w):
    t, n = x2d.shape[0], w.shape[1]
    tm = min(2048, t)
    tn = 1792
    return pl.pallas_call(
        _inproj_kernel,
        grid=(t // tm, n // tn),
        in_specs=[
            pl.BlockSpec((tm, D_MODEL), lambda i, j: (i, 0)),
            pl.BlockSpec((1, D_MODEL), lambda i, j: (0, 0)),
            pl.BlockSpec((D_MODEL, tn), lambda i, j: (0, j)),
        ],
        out_specs=pl.BlockSpec((tm, tn), lambda i, j: (i, j)),
        out_shape=jax.ShapeDtypeStruct((t, n), BF16),
        scratch_shapes=[pltpu.VMEM((tm, D_MODEL), BF16)],
        compiler_params=_cparams(("parallel", "arbitrary")),
        name="in_proj",
    )(x2d, gain, w)


NA_ROW_BLOCK = 16
NA_BAND_ROWS = NA_ROW_BLOCK + NA_WIN_ROWS - 1


def _na_kernel(q_ref, k_ref, v_ref, t2_ref, o_ref, vl_ref, vr_ref, *, rows):
    half = NA_WIN_ROWS // 2
    band = NA_WIN_ROWS * GRID_W
    i0 = pl.program_id(1) * NA_ROW_BLOCK
    block_start = jnp.clip(i0 - half, 0, rows - NA_BAND_ROWS)
    scale = NA_HEAD_DIM ** -0.5

    lane = lax.broadcasted_iota(jnp.int32, (GRID_W, NA_WIDTH), 1)
    first = jnp.where(lane % PAIR < NA_HEAD_DIM, 1.0, 0.0).astype(BF16)
    second = jnp.where(lane % PAIR < NA_HEAD_DIM, 0.0, 1.0).astype(BF16)
    for r in range(NA_BAND_ROWS):
        v_row = v_ref[r]
        vl_ref[r] = v_row * first
        vr_ref[r] = v_row * second

    rows_per_iter = 8

    def rows_step(it, carry):
        s, v_sel = [], []
        for u in range(rows_per_iter):
            m = it * rows_per_iter + u
            i = i0 + m
            r0 = jnp.clip(i - half, 0, rows - NA_WIN_ROWS)
            off = r0 - block_start
            dr0 = (NA_WIN_ROWS - 1) - (i - r0)
            q = q_ref[m] * scale
            q_sel = (q * first, q * second)
            kb = k_ref[pl.ds(off, NA_WIN_ROWS)].reshape(band, NA_WIDTH)
            v_sel.append((vl_ref[pl.ds(off, NA_WIN_ROWS)].reshape(band, NA_WIDTH),
                          vr_ref[pl.ds(off, NA_WIN_ROWS)].reshape(band, NA_WIDTH)))
            for h in range(NA_HEADS):
                ps = slice((h // 2) * PAIR, (h // 2 + 1) * PAIR)
                bias = jnp.concatenate([t2_ref[h, dr0 + 2 * jj] for jj in range(NA_WIN_ROWS // 2)], axis=1)
                s.append(_dot_nt(q_sel[h % 2][:, ps], kb[:, ps]) + bias)
        mx = [jnp.max(x, axis=-1, keepdims=True) for x in s]
        p = [jnp.exp(x - m_) for x, m_ in zip(s, mx)]
        inv = [1.0 / jnp.sum(x, axis=-1, keepdims=True) for x in p]
        for u in range(rows_per_iter):
            outs = []
            for pp in range(NA_HEADS // 2):
                ps = slice(pp * PAIR, (pp + 1) * PAIR)
                e = u * NA_HEADS + 2 * pp
                o_l = _dot(p[e].astype(BF16), v_sel[u][0][:, ps])
                o_r = _dot(p[e + 1].astype(BF16), v_sel[u][1][:, ps])
                outs.append(o_l * inv[e] + o_r * inv[e + 1])
            o_ref[it * rows_per_iter + u] = jnp.concatenate(outs, axis=-1).astype(o_ref.dtype)
        return carry

    lax.fori_loop(0, NA_ROW_BLOCK // rows_per_iter, rows_step, 0)


def _na_bias_table(rpb):
    x = np.arange(GRID_W)[:, None]
    c = np.arange(GRID_W)[None, :]
    c0 = np.clip(x - NA_WIN_COLS // 2, 0, GRID_W - NA_WIN_COLS)
    valid = ((c >= c0) & (c < c0 + NA_WIN_COLS)).reshape(-1)
    dc = np.clip(c - x + (NA_WIN_COLS - 1), 0, 2 * NA_WIN_COLS - 2).reshape(-1)
    onehot = (dc[None, :] == np.arange(2 * NA_WIN_COLS - 1)[:, None]) & valid[None, :]
    tab = jnp.einsum('hrd,dn->hrn', rpb, jnp.asarray(onehot, F32), precision=lax.Precision.HIGHEST)
    tab = tab + jnp.asarray(np.where(valid, 0.0, NEG_BIG), F32)
    tab = tab.reshape(NA_HEADS, 2 * NA_WIN_ROWS - 1, GRID_W, GRID_W)
    return jnp.concatenate([tab[:, :-1], tab[:, 1:]], axis=-1)


def _na_attention(z, bias_tab, bsz, seq):
    rows = seq // GRID_W
    assert rows >= NA_BAND_ROWS and rows % NA_ROW_BLOCK == 0
    z4 = z.reshape(bsz, rows, GRID_W, Z_COLS)
    half = NA_WIN_ROWS // 2

    def kv_spec(col):
        return pl.BlockSpec(
            (pl.Squeezed(), pl.Element(NA_BAND_ROWS), pl.Element(GRID_W), pl.Element(NA_WIDTH)),
            lambda b, i: (b, jnp.clip(i * NA_ROW_BLOCK - half, 0, rows - NA_BAND_ROWS), 0, col))

    qblk = Z_NA // NA_WIDTH
    out = pl.pallas_call(
        functools.partial(_na_kernel, rows=rows),
        grid=(bsz, rows // NA_ROW_BLOCK),
        in_specs=[
            pl.BlockSpec((None, NA_ROW_BLOCK, GRID_W, NA_WIDTH), lambda b, i: (b, i, 0, qblk)),
            kv_spec(Z_NA + NA_WIDTH),
            kv_spec(Z_NA + 2 * NA_WIDTH),
            pl.BlockSpec(bias_tab.shape, lambda b, i: (0, 0, 0, 0)),
        ],
        out_specs=pl.BlockSpec((None, NA_ROW_BLOCK, GRID_W, NA_WIDTH), lambda b, i: (b, i, 0, 0)),
        out_shape=jax.ShapeDtypeStruct((bsz, rows, GRID_W, NA_WIDTH), BF16),
        scratch_shapes=[pltpu.VMEM((NA_BAND_ROWS, GRID_W, NA_WIDTH), BF16)] * 2,
        compiler_params=_cparams(("parallel", "arbitrary")),
        name="na_attn",
    )(z4, z4, z4, bias_tab)
    return out.reshape(bsz * seq, NA_WIDTH)


def _softplus(u):
    return jnp.maximum(u, 0.0) + jnp.log1p(jnp.exp(-jnp.abs(u)))


def _rwkv_kernel(zc_ref, zp_ref, zn_ref, cw_ref, d0_ref, d2_ref, a0_ref, a2_ref, g2_ref,
                 kk_ref, ka_ref, rk_ref, bones_ref, tri_ref,
                 y_ref, bonus_ref, gate_ref, s_ref, *bufs, tile, nt, reverse):
    i = pl.program_id(1)
    ti = nt - 1 - i if reverse else i

    @pl.when(i == 0)
    def _():
        s_ref[...] = jnp.zeros_like(s_ref)

    _rwkv_prep_tile(zc_ref, zp_ref, zn_ref, cw_ref, d0_ref, d2_ref, a0_ref, a2_ref, g2_ref,
                    kk_ref, ka_ref, rk_ref, bones_ref, tri_ref, bonus_ref, gate_ref, *bufs,
                    tile=tile, nt=nt, ti=ti, reverse=reverse)
    _rwkv_scan_tile(y_ref, s_ref, *bufs, tile=tile, reverse=reverse)


def _rwkv_prep_tile(zc_ref, zp_ref, zn_ref, cw_ref, d0_ref, d2_ref, a0_ref, a2_ref, g2_ref,
                    kk_ref, ka_ref, rk_ref, bones_ref, tri_ref, bonus_ref, gate_ref,
                    rt_ref, at_ref, bt_ref, kt_ref, bh_ref, kh_ref, v_ref, tot_ref,
                    *, tile, nt, ti, reverse):
    n_chunks = tile // CHUNK

    zc = zc_ref[...].astype(F32)
    prev_row = jnp.where(ti == 0, 0.0, zp_ref[...].astype(F32)[7:8, :])
    next_row = jnp.where(ti == nt - 1, 0.0, zn_ref[...].astype(F32)[0:1, :])
    zf = (pltpu.roll(zc, 1, axis=0) * cw_ref[0:1, :] + zc * cw_ref[1:2, :]
          + pltpu.roll(zc, tile - 1, axis=0) * cw_ref[2:3, :])
    rows8 = lax.broadcasted_iota(jnp.int32, (8, 1), 0)
    fix_first = jnp.where(rows8 == 0, (prev_row - zc[tile - 1:tile, :]) * cw_ref[0:1, :], 0.0)
    fix_last = jnp.where(rows8 == 7, (next_row - zc[0:1, :]) * cw_ref[2:3, :], 0.0)
    zf = jnp.concatenate([zf[0:8] + fix_first, zf[8:tile - 8], zf[tile - 8:tile] + fix_last], axis=0)

    o1, o2, o3 = RW_WIDTH, 2 * RW_WIDTH, 3 * RW_WIDTH
    o4 = o3 + 2 * DECAY_LORA
    o5 = o4 + 2 * AAA_LORA
    r = zf[:, 0:o1]
    k = zf[:, o1:o2]
    v = zf[:, o2:o3]
    xw = zf[:, o3:o4]
    xa = zf[:, o4:o5]
    xg = zf[:, o5:RW_COLS_PAD]

    bones = bones_ref[...]
    kk = k * kk_ref[...]
    kk = kk * lax.rsqrt(jnp.maximum(_dot((kk * kk).astype(BF16), bones), 1e-24))
    gate_ref[...] = _dot(jax.nn.sigmoid(xg).astype(BF16), g2_ref[...]).astype(gate_ref.dtype)

    w_log = -_softplus(-(d0_ref[...] + _dot(jnp.tanh(xw).astype(BF16), d2_ref[...]))) - 0.5
    logw = -jnp.exp(w_log)
    a = jax.nn.sigmoid(a0_ref[...] + _dot(xa.astype(BF16), a2_ref[...]))
    k_d = k * (1.0 + (a - 1.0) * ka_ref[...])
    b = kk * a
    bonus_ref[...] = (_dot((r * k_d * rk_ref[...]).astype(BF16), bones) * v).astype(bonus_ref.dtype)

    span = tri_ref.shape[0]
    g = jnp.concatenate([_sel_left(tri_ref[...], logw[r:r + span]) for r in range(0, tile, span)], axis=0)
    last = 0 if reverse else CHUNK - 1
    tot = jnp.concatenate(
        [jnp.broadcast_to(g[c * CHUNK + last:c * CHUNK + last + 1, :], (CHUNK, RW_WIDTH))
         for c in range(n_chunks)], axis=0)
    eng = jnp.exp(-g)
    ehat = jnp.exp(tot - g)
    rt_ref[...] = r * jnp.exp(g)
    at_ref[...] = -kk * jnp.exp(g - logw)
    bt_ref[...] = b * eng
    kt_ref[...] = k_d * eng
    bh_ref[...] = b * ehat
    kh_ref[...] = k_d * ehat
    v_ref[...] = v
    tot_ref[...] = tot


def _rwkv_scan_tile(y_ref, s_ref, rt_ref, at_ref, bt_ref, kt_ref, bh_ref, kh_ref, v_ref, tot_ref,
                    *, tile, reverse):
    n_chunks = tile // CHUNK

    ri = lax.broadcasted_iota(jnp.int32, (PAIR, PAIR), 0)
    ci = lax.broadcasted_iota(jnp.int32, (PAIR, PAIR), 1)
    same_head = jnp.where(jnp.right_shift(ri, 6) == jnp.right_shift(ci, 6), 1.0, 0.0).astype(F32)
    order = ci - ri if reverse else ri - ci
    m_strict = jnp.where(order > 0, same_head, 0.0)
    m_incl = jnp.where(order >= 0, same_head, 0.0)
    eye = jnp.where(ci == ri, 1.0, 0.0).astype(F32)
    lvl = []
    for q in range(INV_LEVELS):
        in_block = jnp.where(jnp.right_shift(ri, q + 1) == jnp.right_shift(ci, q + 1), 1.0, 0.0).astype(F32)
        lvl.append(jnp.where(jnp.right_shift(ri, q) != jnp.right_shift(ci, q), in_block, 0.0))
    lane = lax.broadcasted_iota(jnp.int32, (CHUNK, PAIR), 1)
    left = lane < RW_HEAD_DIM

    def stack(x):
        return jnp.concatenate([jnp.where(left, x, 0.0), jnp.where(left, 0.0, x)], axis=0)

    groups = [(c, p) for c in range(n_chunks) for p in range(N_PAIRS)]
    eye_b = eye.astype(BF16)

    def tile_of(ref, c, p):
        return ref[c * CHUNK:(c + 1) * CHUNK, p * PAIR:(p + 1) * PAIR]

    at_b = [stack(tile_of(at_ref, c, p)).astype(BF16) for c, p in groups]
    vs_b = [stack(tile_of(v_ref, c, p)).astype(BF16) for c, p in groups]
    o1_ = []
    for gi, (c, p) in enumerate(groups):
        bt = tile_of(bt_ref, c, p).astype(BF16)
        kt = tile_of(kt_ref, c, p).astype(BF16)
        lhs1 = jnp.concatenate([at_b[gi], stack(tile_of(rt_ref, c, p)).astype(BF16)], axis=0)
        o1_.append(_dot_nt(lhs1, jnp.concatenate([bt, bt, kt, kt], axis=0)))
    ms_b = m_strict.astype(BF16)
    mi_b = m_incl.astype(BF16)
    lvl_b = [m.astype(BF16) for m in lvl]
    n_b = [o[0:PAIR, 0:PAIR].astype(BF16) * ms_b for o in o1_]
    a_ak = [o[0:PAIR, PAIR:].astype(BF16) * ms_b for o in o1_]
    a_r = [jnp.concatenate([o[PAIR:, 0:PAIR].astype(BF16) * mi_b, o[PAIR:, PAIR:].astype(BF16) * mi_b], axis=1)
           for o in o1_]

    t_inv = [eye.astype(BF16) + n * lvl_b[0] for n in n_b]
    for q in range(1, INV_LEVELS):
        blk = 1 << q
        if blk < 16:
            xq = [_dot(n * lvl_b[q], t) for n, t in zip(n_b, t_inv)]
            t_inv = [t + _dot(t, x.astype(BF16)).astype(BF16) for t, x in zip(t_inv, xq)]
            continue
        n_blocks = PAIR // blk
        late = [k for k in range(n_blocks) if (k % 2 == 0) == reverse]

        def take(x):
            return jnp.concatenate([x[k * blk:(k + 1) * blk] for k in late], axis=0)

        def spread(x):
            zero = jnp.zeros((blk, PAIR), x.dtype)
            return jnp.concatenate(
                [x[late.index(k) * blk:(late.index(k) + 1) * blk] if k in late else zero
                 for k in range(n_blocks)], axis=0)

        xq = [_dot(take(n) * take(lvl_b[q]), t) for n, t in zip(n_b, t_inv)]
        upd = [_dot(take(t), spread(x.astype(BF16))) for t, x in zip(t_inv, xq)]
        t_inv = [t + spread(u.astype(BF16)) for t, u in zip(t_inv, upd)]

    av = [_dot(a, v_) for a, v_ in zip(a_ak, vs_b)]
    pu = [_dot(t, jnp.concatenate([a, x.astype(BF16)], axis=1))
          for t, a, x in zip(t_inv, at_b, av)]
    zero_b = jnp.zeros((PAIR, PAIR), BF16)
    qb, y_loc, n_st = [], [], []
    for gi, (c, p) in enumerate(groups):
        bh_t = stack(tile_of(bh_ref, c, p)).T
        kh_t = stack(tile_of(kh_ref, c, p)).T
        wmat = jnp.concatenate(
            [pu[gi].astype(BF16), jnp.concatenate([zero_b, vs_b[gi]], axis=1)], axis=0)
        lhs2 = jnp.concatenate([a_r[gi], jnp.concatenate([bh_t, kh_t], axis=1).astype(BF16)], axis=0)
        o2_ = _dot(lhs2, wmat)
        q_mat = stack(tile_of(rt_ref, c, p)) + o2_[0:PAIR, 0:PAIR]
        qb.append(jnp.concatenate([q_mat, o2_[PAIR:, 0:PAIR]], axis=0).astype(BF16))
        y_loc.append(o2_[0:PAIR, PAIR:])
        n_st.append(o2_[PAIR:, PAIR:])

    rep = PAIR // n_chunks
    w_tot = []
    for p in range(N_PAIRS):
        t_rows = jnp.concatenate(
            [tot_ref[c * CHUNK:c * CHUNK + rep, p * PAIR:(p + 1) * PAIR] for c in range(n_chunks)], axis=0)
        hi = t_rows.astype(BF16)
        r1 = t_rows - hi.astype(F32)
        mid = r1.astype(BF16)
        lo = (r1 - mid.astype(F32)).astype(BF16)
        cols = _dot_nt(eye_b, hi) + _dot_nt(eye_b, mid) + _dot_nt(eye_b, lo)
        w_tot.append([jnp.exp(cols[:, c * rep:c * rep + 1]) for c in range(n_chunks)])

    st = [s_ref[p] for p in range(N_PAIRS)]
    for c in (reversed(range(n_chunks)) if reverse else range(n_chunks)):
        o3_ = [_dot(qb[c * N_PAIRS + p], st[p].astype(BF16)) for p in range(N_PAIRS)]
        for p in range(N_PAIRS):
            gi = c * N_PAIRS + p
            y_bs = o3_[p][0:PAIR] + y_loc[gi]
            st[p] = st[p] * w_tot[p][c] + o3_[p][PAIR:] + n_st[gi]
            y_ref[c * CHUNK:(c + 1) * CHUNK, p * PAIR:(p + 1) * PAIR] = (
                y_bs[0:CHUNK] + y_bs[CHUNK:]).astype(y_ref.dtype)
    for p in range(N_PAIRS):
        s_ref[p] = st[p]


def _rwkv_scan(z, lp, bsz, seq, reverse):
    tile = min(512, seq)
    nt = seq // tile
    z3 = z.reshape(bsz, seq, Z_COLS)
    t8 = tile // 8
    d = int(reverse)

    def tidx(i):
        return nt - 1 - i if reverse else i

    span = min(256, tile)
    cidx = np.arange(span)
    same = (cidx[:, None] // CHUNK) == (cidx[None, :] // CHUNK)
    upto = cidx[None, :] >= cidx[:, None] if reverse else cidx[None, :] <= cidx[:, None]
    tri = jnp.asarray(same & upto, BF16)

    def full(shape):
        return pl.BlockSpec(shape, lambda b, i: (0,) * len(shape))

    def per_dir(shape):
        return pl.BlockSpec((None,) + shape, lambda b, i: (d,) + (0,) * len(shape))

    out_spec = pl.BlockSpec((None, tile, RW_WIDTH), lambda b, i: (b, tidx(i), 0))
    nat = pltpu.VMEM((tile, RW_WIDTH), F32)
    return pl.pallas_call(
        functools.partial(_rwkv_kernel, tile=tile, nt=nt, reverse=reverse),
        grid=(bsz, nt),
        in_specs=[
            pl.BlockSpec((None, tile, RW_COLS_PAD), lambda b, i: (b, tidx(i), 0)),
            pl.BlockSpec((None, 8, RW_COLS_PAD), lambda b, i: (b, jnp.maximum(tidx(i) * t8 - 1, 0), 0)),
            pl.BlockSpec((None, 8, RW_COLS_PAD),
                         lambda b, i: (b, jnp.minimum((tidx(i) + 1) * t8, seq // 8 - 1), 0)),
            full((3, RW_COLS_PAD)),
            per_dir((1, RW_WIDTH)), per_dir((2 * DECAY_LORA, RW_WIDTH)),
            per_dir((1, RW_WIDTH)), per_dir((2 * AAA_LORA, RW_WIDTH)),
            full((GATE_LORA_PAD, RW_WIDTH)),
            full((1, RW_WIDTH)), full((1, RW_WIDTH)), full((1, RW_WIDTH)),
            full((RW_WIDTH, RW_WIDTH)),
            full((span, span)),
        ],
        out_specs=[out_spec, out_spec, out_spec],
        out_shape=[jax.ShapeDtypeStruct((bsz, seq, RW_WIDTH), BF16),
                   jax.ShapeDtypeStruct((bsz, seq, RW_WIDTH), BF16),
                   jax.ShapeDtypeStruct((bsz, seq, RW_WIDTH), BF16)],
        scratch_shapes=[pltpu.VMEM((N_PAIRS, PAIR, PAIR), F32)] + [nat] * 8,
        compiler_params=_cparams(("arbitrary", "arbitrary")),
        name="rwkv_bwd" if reverse else "rwkv_fwd",
    )(z3, z3, z3, lp["rw_conv"], lp["rw_decay0"], lp["rw_decay2"], lp["rw_a0"], lp["rw_a2"],
      lp["rw_g2"], lp["rw_k_k"], lp["rw_k_a"], lp["rw_r_k"], lp["bones"], tri)


def _memkv_kernel(m_ref, g_ref, w_ref, o_ref):
    o_ref[...] = _dot(_rms(m_ref[...], g_ref[...]).astype(BF16), w_ref[...]).astype(o_ref.dtype)


def _mem_kv(mem, gain, w):
    bsz = mem.shape[0]
    return pl.pallas_call(
        _memkv_kernel,
        grid=(bsz,),
        in_specs=[
            pl.BlockSpec((None, MEM_TOKENS, D_MODEL), lambda b: (b, 0, 0)),
            pl.BlockSpec((1, D_MODEL), lambda b: (0, 0)),
            pl.BlockSpec((D_MODEL, 2 * MEM_WIDTH), lambda b: (0, 0)),
        ],
        out_specs=pl.BlockSpec((None, MEM_TOKENS, 2 * MEM_WIDTH), lambda b: (b, 0, 0)),
        out_shape=jax.ShapeDtypeStruct((bsz, MEM_TOKENS, 2 * MEM_WIDTH), BF16),
        compiler_params=_cparams(("parallel",)),
        name="mem_kv",
    )(mem, gain, w)


def _memattn_kernel(q_ref, kv_ref, o_ref):
    scale = MEM_HEAD_DIM ** -0.5
    outs = []
    for h in range(MEM_HEADS):
        sl = slice(h * MEM_HEAD_DIM, (h + 1) * MEM_HEAD_DIM)
        vsl = slice(MEM_WIDTH + h * MEM_HEAD_DIM, MEM_WIDTH + (h + 1) * MEM_HEAD_DIM)
        s = _dot_nt(q_ref[:, sl], kv_ref[:, sl]) * scale
        m = jnp.max(s, axis=-1, keepdims=True)
        p = jnp.exp(s - m)
        l = jnp.sum(p, axis=-1, keepdims=True)
        outs.append(_dot(p.astype(BF16), kv_ref[:, vsl]) / l)
    o_ref[...] = jnp.concatenate(outs, axis=-1).astype(o_ref.dtype)


def _mem_attention(z, kv, bsz, seq):
    tm = min(1024, seq)
    z3 = z.reshape(bsz, seq, Z_COLS)
    out = pl.pallas_call(
        _memattn_kernel,
        grid=(bsz, seq // tm),
        in_specs=[
            pl.BlockSpec((None, tm, MEM_WIDTH), lambda b, i: (b, i, Z_MEM // MEM_WIDTH)),
            pl.BlockSpec((None, MEM_TOKENS, 2 * MEM_WIDTH), lambda b, i: (b, 0, 0)),
        ],
        out_specs=pl.BlockSpec((None, tm, MEM_WIDTH), lambda b, i: (b, i, 0)),
        out_shape=jax.ShapeDtypeStruct((bsz, seq, MEM_WIDTH), BF16),
        compiler_params=_cparams(("parallel", "parallel")),
        name="mem_attn",
    )(z3, kv)
    return out.reshape(bsz * seq, MEM_WIDTH)


def _merge_kernel(x_ref, ona_ref, y0_ref, y1_ref, b0_ref, b1_ref, gt_ref, omem_ref,
                  g0_ref, g1_ref, g2_ref, wb_ref, wo_ref, lw_ref, lb_ref, bones_ref, o_ref):
    bones = bones_ref[...]
    inv_n = 1.0 / RW_HEAD_DIM
    y = y0_ref[...].astype(F32) + y1_ref[...].astype(F32)
    mu = _dot(y.astype(BF16), bones) * inv_n
    dl = y - mu
    var = _dot((dl * dl).astype(BF16), bones) * inv_n
    yn = dl * lax.rsqrt(var + GN_EPS)
    bonus = b0_ref[...].astype(F32) + b1_ref[...].astype(F32)
    o_rw = (yn * lw_ref[...] + lb_ref[...] + bonus) * gt_ref[...].astype(F32)

    merged = jax.nn.sigmoid(g0_ref[...].astype(F32)) * _dot(ona_ref[...], wb_ref[0])
    merged = merged + jax.nn.sigmoid(g1_ref[...].astype(F32)) * _dot(o_rw.astype(BF16), wb_ref[1])
    merged = merged + jax.nn.sigmoid(g2_ref[...].astype(F32)) * _dot(omem_ref[...], wb_ref[2])
    o_ref[...] = x_ref[...] + _dot(merged.astype(BF16), wo_ref[...])


def _merge(x2d, z, o_na, rw_fwd, rw_bwd, o_mem, lp):
    t = x2d.shape[0]
    tm = min(512, t)
    y0, b0, gate = (a.reshape(t, RW_WIDTH) for a in rw_fwd)
    y1, b1, _ = (a.reshape(t, RW_WIDTH) for a in rw_bwd)
    gblk = Z_GATE // D_MODEL

    def tok(width):
        return pl.BlockSpec((tm, width), lambda i: (i, 0))

    def const(shape):
        return pl.BlockSpec(shape, lambda i: (0,) * len(shape))

    return pl.pallas_call(
        _merge_kernel,
        grid=(t // tm,),
        in_specs=[
            tok(D_MODEL), tok(NA_WIDTH), tok(RW_WIDTH), tok(RW_WIDTH), tok(RW_WIDTH), tok(RW_WIDTH), tok(RW_WIDTH),
            tok(MEM_WIDTH),
            pl.BlockSpec((tm, D_MODEL), lambda i: (i, gblk)),
            pl.BlockSpec((tm, D_MODEL), lambda i: (i, gblk + 1)),
            pl.BlockSpec((tm, D_MODEL), lambda i: (i, gblk + 2)),
            const((N_BRANCH, NA_WIDTH, D_MODEL)), const((D_MODEL, D_MODEL)),
            const((1, RW_WIDTH)), const((1, RW_WIDTH)), const((RW_WIDTH, RW_WIDTH)),
        ],
        out_specs=tok(D_MODEL),
        out_shape=jax.ShapeDtypeStruct((t, D_MODEL), F32),
        compiler_params=_cparams(("parallel",)),
        name="merge",
    )(x2d, o_na, y0, y1, b0, b1, gate, o_mem, z, z, z, lp["w_branch"], lp["w_out"],
      lp["rw_lnx_w"], lp["rw_lnx_b"], lp["bones"])


def _ffn_kernel(x_ref, xp_ref, xn_ref, g_ref, wv_ref, wg_ref, cv_ref, cg_ref, bv_ref, bg_ref, wd_ref,
                fg_ref, o_ref, h_ref, acc_ref, *, tm, tiles_per_seq, n_col_tiles, final_norm):
    i = pl.program_id(0)
    j = pl.program_id(1)
    ext = tm + 16

    n_parts = 4 if tm % 64 == 0 else 1
    pm = tm // n_parts

    def conv(u, cw, bias):
        return (pltpu.roll(u, 1, axis=0)[8:8 + pm] * cw[0:1, :] + u[8:8 + pm] * cw[1:2, :]
                + pltpu.roll(u, pm + 15, axis=0)[8:8 + pm] * cw[2:3, :] + bias)

    def body(first, last):
        ups = []
        for part in range(n_parts):
            if first:
                gain = g_ref[...]
                seq_pos = i % tiles_per_seq
                lo = part * pm + (8 if part else 0)
                hi = (part + 1) * pm + (8 if part + 1 < n_parts else 0)
                if part == 0:
                    h_ref[0:8, :] = jnp.where(seq_pos == 0, 0.0, _rms(xp_ref[...], gain)).astype(BF16)
                h_ref[8 + lo:8 + hi, :] = _rms(x_ref[lo:hi, :], gain).astype(BF16)
                if part == n_parts - 1:
                    h_ref[8 + tm:ext, :] = jnp.where(
                        seq_pos == tiles_per_seq - 1, 0.0, _rms(xn_ref[...], gain)).astype(BF16)
            hx = h_ref[part * pm:part * pm + pm + 16, :]
            ups.append((_dot(hx, wv_ref[...]), _dot(hx, wg_ref[...])))
        for part, (uv, ug) in enumerate(ups):
            u_val = conv(uv, cv_ref[...], bv_ref[...])
            u_gate = conv(ug, cg_ref[...], bg_ref[...])
            act = (u_gate * jax.nn.sigmoid(u_gate) * u_val).astype(BF16)
            down = _dot(act, wd_ref[...])
            rs = slice(part * pm, (part + 1) * pm)
            if first:
                acc_ref[rs, :] = down
            elif last:
                out = x_ref[rs, :] + (acc_ref[rs, :] + down)
                o_ref[rs, :] = _rms(out, fg_ref[...]) if final_norm else out
            else:
                acc_ref[rs, :] += down

    assert n_col_tiles >= 2
    pl.when(j == 0)(functools.partial(body, True, False))
    pl.when((j > 0) & (j < n_col_tiles - 1))(functools.partial(body, False, False))
    pl.when(j == n_col_tiles - 1)(functools.partial(body, False, True))


def _ffn(x2d, lp, final_gain, seq, final_norm):
    t = x2d.shape[0]
    tm = min(2048, seq)
    tn = 256
    nj = D_FF // tn
    t8 = tm // 8
    return pl.pallas_call(
        functools.partial(_ffn_kernel, tm=tm, tiles_per_seq=seq // tm, n_col_tiles=nj, final_norm=final_norm),
        grid=(t // tm, nj),
        in_specs=[
            pl.BlockSpec((tm, D_MODEL), lambda i, j: (i, 0)),
            pl.BlockSpec((8, D_MODEL), lambda i, j: (jnp.maximum(i * t8 - 1, 0), 0)),
            pl.BlockSpec((8, D_MODEL), lambda i, j: (jnp.minimum((i + 1) * t8, t // 8 - 1), 0)),
            pl.BlockSpec((1, D_MODEL), lambda i, j: (0, 0)),
            pl.BlockSpec((D_MODEL, tn), lambda i, j: (0, j)),
            pl.BlockSpec((D_MODEL, tn), lambda i, j: (0, nj + j)),
            pl.BlockSpec((3, tn), lambda i, j: (0, j)),
            pl.BlockSpec((3, tn), lambda i, j: (0, nj + j)),
            pl.BlockSpec((1, tn), lambda i, j: (0, j)),
            pl.BlockSpec((1, tn), lambda i, j: (0, nj + j)),
            pl.BlockSpec((tn, D_MODEL), lambda i, j: (j, 0)),
            pl.BlockSpec((1, D_MODEL), lambda i, j: (0, 0)),
        ],
        out_specs=pl.BlockSpec((tm, D_MODEL), lambda i, j: (i, 0)),
        out_shape=jax.ShapeDtypeStruct((t, D_MODEL), F32),
        scratch_shapes=[pltpu.VMEM((tm + 16, D_MODEL), BF16), pltpu.VMEM((tm, D_MODEL), F32)],
        compiler_params=_cparams(("parallel", "arbitrary")),
        name="ffn",
    )(x2d, x2d, x2d, lp["ffn_norm"], lp["w_up"], lp["w_up"], lp["ffn_conv"], lp["ffn_conv"],
      lp["ffn_conv_b"], lp["ffn_conv_b"], lp["w_down"], final_gain)


def _prep_layer(l, attn_norm, w_in, na_rpb, rw_conv, rw_decay0, rw_decay2, rw_a0, rw_a2, rw_g2, rw_k_k,
                rw_k_a, rw_r_k, rw_lnx_w, rw_lnx_b, mem_norm, w_mem_kv, w_branch, w_out, ffn_norm, w_up,
                ffn_conv, ffn_conv_b, w_down):
    c1 = 3 * NA_WIDTH
    c2 = c1 + RW_COLS
    w = w_in[l].astype(BF16)
    pad_cols = RW_COLS_PAD - RW_COLS
    w_new = jnp.concatenate(
        [w[:, c1:c2], jnp.zeros((D_MODEL, pad_cols), w.dtype), w[:, :c1], w[:, c2:]], axis=1)

    def lora_rows(m):
        zero = jnp.zeros_like(m[0])
        return jnp.stack([jnp.concatenate([m[0], zero], 0), jnp.concatenate([zero, m[1]], 0)]).astype(BF16)

    head = np.arange(RW_WIDTH) // RW_HEAD_DIM
    return dict(
        attn_norm=attn_norm[l][None, :],
        w_in=w_new.astype(BF16),
        na_bias=_na_bias_table(na_rpb[l]),
        rw_conv=jnp.pad(rw_conv[l], ((0, 0), (0, pad_cols))),
        rw_decay0=rw_decay0[l][:, None, :],
        rw_decay2=lora_rows(rw_decay2[l]),
        rw_a0=rw_a0[l][:, None, :],
        rw_a2=lora_rows(rw_a2[l]),
        rw_g2=jnp.pad(rw_g2[l], ((0, GATE_LORA_PAD - GATE_LORA), (0, 0))).astype(BF16),
        rw_k_k=rw_k_k[l][None, :],
        rw_k_a=rw_k_a[l][None, :],
        rw_r_k=rw_r_k[l].reshape(1, RW_WIDTH),
        rw_lnx_w=rw_lnx_w[l][None, :],
        rw_lnx_b=rw_lnx_b[l][None, :],
        bones=jnp.asarray(head[:, None] == head[None, :], BF16),
        mem_norm=mem_norm[l][None, :],
        w_mem_kv=w_mem_kv[l].astype(BF16),
        w_branch=w_branch[l].astype(BF16),
        w_out=w_out[l].astype(BF16),
        ffn_norm=ffn_norm[l][None, :],
        w_up=w_up[l].astype(BF16),
        ffn_conv=ffn_conv[l],
        ffn_conv_b=ffn_conv_b[l][None, :],
        w_down=w_down[l].astype(BF16),
    )


def _layer(x2d, mem, lp, bsz, seq, final_gain, final_norm):
    z = _in_proj(x2d, lp["attn_norm"], lp["w_in"])
    o_na = _na_attention(z, lp["na_bias"], bsz, seq)
    rw_fwd = _rwkv_scan(z, lp, bsz, seq, reverse=False)
    rw_bwd = _rwkv_scan(z, lp, bsz, seq, reverse=True)
    o_mem = _mem_attention(z, _mem_kv(mem, lp["mem_norm"], lp["w_mem_kv"]), bsz, seq)
    x2d = _merge(x2d, z, o_na, rw_fwd, rw_bwd, o_mem, lp)
    return _ffn(x2d, lp, final_gain, seq, final_norm)


def _trunk(x, mem, layers, final_gain):
    bsz, seq, _ = x.shape
    x2d = x.reshape(bsz * seq, D_MODEL)
    for l, lp in enumerate(layers):
        x2d = _layer(x2d, mem, lp, bsz, seq, final_gain, l == len(layers) - 1)
    return x2d.reshape(bsz, seq, D_MODEL)


def kernel(x_prompt, x_sample, mem_prompt, mem_sample, attn_norm, w_in, na_rpb, rw_conv, rw_decay0, rw_decay2, rw_a0, rw_a2, rw_g2, rw_k_k, rw_k_a, rw_r_k, rw_lnx_w, rw_lnx_b, mem_norm, w_mem_kv, w_branch, w_out, ffn_norm, w_up, ffn_conv, ffn_conv_b, w_down, final_norm):
    params = (attn_norm, w_in, na_rpb, rw_conv, rw_decay0, rw_decay2, rw_a0, rw_a2, rw_g2, rw_k_k, rw_k_a,
              rw_r_k, rw_lnx_w, rw_lnx_b, mem_norm, w_mem_kv, w_branch, w_out, ffn_norm, w_up, ffn_conv,
              ffn_conv_b, w_down)
    layers = [_prep_layer(l, *params) for l in range(attn_norm.shape[0])]
    final_gain = final_norm[None, :]
    y_prompt = _trunk(x_prompt, mem_prompt, layers, final_gain)
    y_sample = _trunk(x_sample, mem_sample, layers, final_gain)
    return (y_prompt, y_sample)
```
